```python
import jax, jax.numpy as jnp
from jax import lax
import numpy as np

D_MODEL = 1024
BATCH = 2
SEQ = 8192
DEPTH = 2

GRID_W = 64
CTX_LEN = 256
N_EVEN = (DEPTH + 1) // 2
N_ODD = DEPTH // 2
N_MOD = 6
EPS = 1e-6
NEG = -1e30
D_LRU = D_MODEL // 2
LRU_HEADS = 8
LRU_HEAD_DIM = D_LRU // LRU_HEADS
LRU_C = 8.0
CONV_A_W = 4
D_SC = D_MODEL // 2
CONV_B_W = 3
RC_IN_WIDTH = 2 * D_LRU + 3 * D_SC
HEAD_DIM = 64
N_Q_HEADS = D_MODEL // HEAD_DIM
N_KV_HEADS = 4
GQA_GROUP = N_Q_HEADS // N_KV_HEADS
D_Q = N_Q_HEADS * HEAD_DIM
D_KV = N_KV_HEADS * HEAD_DIM
WINDOW = 128
BLOCK = 128
ROPE_BASE = 10000.0
ROPE_FREQS = HEAD_DIM // 4
D_FF = -(-8 * D_MODEL // (3 * 256)) * 256

kernel_name = "hybrid_rglru_shortconv_swa_dit_prefix"


def rmsnorm(x, g):
    x32 = x.astype(jnp.float32)
    y = x32 * lax.rsqrt(jnp.mean(x32 * x32, axis=-1, keepdims=True) + EPS)
    return (y * g.astype(jnp.float32)).astype(x.dtype)


def modulate(h, shift, scale):
    return h * (1 + scale) + shift


def swiglu(h, w_in, w_out):
    g, u = jnp.split(h @ w_in, 2, axis=-1)
    return (jax.nn.silu(g) * u) @ w_out


def depthwise_conv(x, w, pad):
    return lax.conv_general_dilated(
        x, w[:, None, :], window_strides=(1,), padding=[pad],
        dimension_numbers=('NWC', 'WIO', 'NWC'), feature_group_count=x.shape[-1])


def block_diag(x, w, b):
    B_, T, _ = x.shape
    xh = x.reshape(B_, T, LRU_HEADS, LRU_HEAD_DIM)
    return (jnp.einsum('bthi,hij->bthj', xh, w) + b).reshape(B_, T, D_LRU)


def rglru_coeffs(xc, r_w, r_b, i_w, i_b, lam):
    r = jax.nn.sigmoid(block_diag(xc, r_w, r_b).astype(jnp.float32))
    ig = jax.nn.sigmoid(block_diag(xc, i_w, i_b).astype(jnp.float32))
    log_a = -LRU_C * r * jax.nn.softplus(-lam.astype(jnp.float32))
    a = jnp.exp(log_a)
    mult = jnp.sqrt(jnp.maximum(-jnp.expm1(2.0 * log_a), 1e-12))
    return a, mult * ig * xc.astype(jnp.float32)


def _lin_combine(left, right):
    a_l, b_l = left
    a_r, b_r = right
    return a_r * a_l, a_r * b_l + b_r


def linear_scan(a, b, h0):
    a_cum, b_cum = lax.associative_scan(_lin_combine, (a, b), axis=1)
    h = a_cum * h0[:, None, :] + b_cum
    return h, h[:, -1]


def bidir_rglru(xc_c, xc_l, r_w, r_b, i_w, i_b, lam):
    outs_c, outs_l = [], []
    for d in range(2):
        ac, bc = rglru_coeffs(xc_c, r_w[d], r_b[d], i_w[d], i_b[d], lam[d])
        al, bl = rglru_coeffs(xc_l, r_w[d], r_b[d], i_w[d], i_b[d], lam[d])
        if d == 1:
            ac, bc, al, bl = (jnp.flip(t, axis=1) for t in (ac, bc, al, bl))
        hc, hc_last = linear_scan(ac, bc, jnp.zeros_like(ac[:, 0]))
        hl, _ = linear_scan(al, bl, hc_last)
        if d == 1:
            hc, hl = jnp.flip(hc, axis=1), jnp.flip(hl, axis=1)
        outs_c.append(hc)
        outs_l.append(hl)
    return outs_c[0] + outs_c[1], outs_l[0] + outs_l[1]


def rc_mixer(hc, hl, w_in, conv_a_w, conv_a_b, r_w, r_b, i_w, i_b, lam, conv_b_w, w_out, ctx_out):
    splits = [D_LRU, 2 * D_LRU, 2 * D_LRU + D_SC, 2 * D_LRU + 2 * D_SC]
    xa_c, ga_c, bg_c, cg_c, v_c = jnp.split(hc @ w_in, splits, axis=-1)
    xa_l, ga_l, bg_l, cg_l, v_l = jnp.split(hl @ w_in, splits, axis=-1)
    xa_c = depthwise_conv(xa_c, conv_a_w, (2, 1)) + conv_a_b
    xa_l = depthwise_conv(xa_l, conv_a_w, (2, 1)) + conv_a_b
    hsum_c, hsum_l = bidir_rglru(xa_c, xa_l, r_w, r_b, i_w, i_b, lam)

    def merge(hsum, ga, bg, cg, v):
        ya = hsum.astype(ga.dtype) * jax.nn.gelu(ga)
        yb = bg * depthwise_conv(cg * v, conv_b_w, (1, 1))
        return jnp.concatenate([ya, yb], axis=-1) @ w_out

    yl = merge(hsum_l, ga_l, bg_l, cg_l, v_l)
    yc = merge(hsum_c, ga_c, bg_c, cg_c, v_c) if ctx_out else None
    return yc, yl


def rope_2d(x, cos, sin):
    S = x.shape[1]
    bshape = (1, S) + (1,) * (x.ndim - 3) + (2, ROPE_FREQS)
    c, s = cos.reshape(bshape), sin.reshape(bshape)
    xr = x.reshape(x.shape[:-1] + (2, 2, ROPE_FREQS))
    x1, x2 = xr[..., 0, :], xr[..., 1, :]
    out = jnp.stack([x1 * c - x2 * s, x2 * c + x1 * s], axis=-2)
    return out.reshape(x.shape)


def attn_mixer(hc, hl, w_qkv, sink, w_out, cos, sin, ctx_out):
    B_, S, _ = hl.shape
    T = hc.shape[1]
    nb = S // BLOCK
    scale = HEAD_DIM ** -0.5

    def qkv(h):
        L = h.shape[1]
        q, k, v = jnp.split(h @ w_qkv, [D_Q, D_Q + D_KV], axis=-1)
        return (q.reshape(B_, L, N_KV_HEADS, GQA_GROUP, HEAD_DIM),
                k.reshape(B_, L, N_KV_HEADS, HEAD_DIM),
                v.reshape(B_, L, N_KV_HEADS, HEAD_DIM))

    ql, kl, vl = qkv(hl)
    qc, kc, vc = qkv(hc)
    ql, kl = rope_2d(ql, cos, sin), rope_2d(kl, cos, sin)
    sink32 = sink.astype(jnp.float32).reshape(N_KV_HEADS, GQA_GROUP, 1, 1)

    qb = ql.reshape(B_, nb, BLOCK, N_KV_HEADS, GQA_GROUP, HEAD_DIM) * scale

    def band(t):
        tb = t.reshape(B_, nb, BLOCK, N_KV_HEADS, HEAD_DIM)
        tp = jnp.pad(tb, ((0, 0), (1, 1), (0, 0), (0, 0), (0, 0)))
        return jnp.concatenate([tp[:, :-2], tp[:, 1:-1], tp[:, 2:]], axis=2)

    kw, vw = band(kl), band(vl)
    qi = jnp.arange(BLOCK)[:, None]
    kj = jnp.arange(3 * BLOCK)[None, :]
    in_window = jnp.abs(kj - BLOCK - qi) <= WINDOW
    kblk = jnp.arange(nb)[:, None] + (jnp.arange(3 * BLOCK) // BLOCK)[None, :] - 1
    valid = (kblk >= 0) & (kblk < nb)
    mask = in_window[None] & valid[:, None, :]

    s_win = jnp.einsum('bnqhgd,bnkhd->bnhgqk', qb, kw).astype(jnp.float32)
    s_win = jnp.where(mask[None, :, None, None], s_win, NEG)
    s_ctx = jnp.einsum('bnqhgd,bchd->bnhgqc', qb, kc).astype(jnp.float32)
    m = jnp.maximum(jnp.maximum(s_win.max(-1, keepdims=True), s_ctx.max(-1, keepdims=True)), sink32)
    e_win = jnp.exp(s_win - m)
    e_ctx = jnp.exp(s_ctx - m)
    denom = e_win.sum(-1, keepdims=True) + e_ctx.sum(-1, keepdims=True) + jnp.exp(sink32 - m)
    o = (jnp.einsum('bnhgqk,bnkhd->bnhgqd', e_win.astype(vl.dtype), vw)
         + jnp.einsum('bnhgqc,bchd->bnhgqd', e_ctx.astype(vl.dtype), vc))
    o = (o / denom.astype(o.dtype)).transpose(0, 1, 4, 2, 3, 5).reshape(B_, S, D_Q)
    yl = o @ w_out

    yc = None
    if ctx_out:
        sc = jnp.einsum('bihgd,bjhd->bhgij', qc * scale, kc).astype(jnp.float32)
        mc = jnp.maximum(sc.max(-1, keepdims=True), sink32)
        ec = jnp.exp(sc - mc)
        dc = ec.sum(-1, keepdims=True) + jnp.exp(sink32 - mc)
        oc = jnp.einsum('bhgij,bjhd->bhgid', ec.astype(vc.dtype), vc) / dc.astype(vc.dtype)
        yc = oc.transpose(0, 3, 1, 2, 4).reshape(B_, T, D_Q) @ w_out
    return yc, yl


def setup_inputs(seed: int = 0) -> dict:
    key = jax.random.key(seed)
    ks = iter(jax.random.split(key, 40))
    D = D_MODEL

    def nrm(shape, scale):
        return jax.random.normal(next(ks), shape, jnp.float32) * scale

    a_c = jax.random.uniform(next(ks), (N_EVEN, 2, D_LRU), jnp.float32, minval=0.9, maxval=0.999)
    sig = a_c ** (1.0 / LRU_C)
    rc_lambda = jnp.log(sig) - jnp.log1p(-sig)
    return {
        "x": nrm((BATCH, SEQ, D), 1.0),
        "c": nrm((BATCH, D), 1.0),
        "ctx": nrm((BATCH, CTX_LEN, D), 1.0),
        "c_ctx": nrm((D,), 1.0),
        "ada_w": nrm((DEPTH, D, N_MOD * D), 0.5 * D ** -0.5),
        "ada_b": nrm((DEPTH, N_MOD * D), 0.02),
        "norm_mix_g": 1.0 + nrm((DEPTH, D), 0.05),
        "norm_ffn_g": 1.0 + nrm((DEPTH, D), 0.05),
        "norm_final_g": 1.0 + nrm((D,), 0.05),
        "ffn_w_in": nrm((DEPTH, D, 2 * D_FF), D ** -0.5),
        "ffn_w_out": nrm((DEPTH, D_FF, D), D_FF ** -0.5),
        "rc_w_in": nrm((N_EVEN, D, RC_IN_WIDTH), D ** -0.5),
        "rc_conv_a_w": nrm((N_EVEN, CONV_A_W, D_LRU), CONV_A_W ** -0.5),
        "rc_conv_a_b": nrm((N_EVEN, D_LRU), 0.02),
        "rc_gate_r_w": nrm((N_EVEN, 2, LRU_HEADS, LRU_HEAD_DIM, LRU_HEAD_DIM), LRU_HEAD_DIM ** -0.5),
        "rc_gate_r_b": nrm((N_EVEN, 2, LRU_HEADS, LRU_HEAD_DIM), 0.02),
        "rc_gate_i_w": nrm((N_EVEN, 2, LRU_HEADS, LRU_HEAD_DIM, LRU_HEAD_DIM), LRU_HEAD_DIM ** -0.5),
        "rc_gate_i_b": nrm((N_EVEN, 2, LRU_HEADS, LRU_HEAD_DIM), 0.02),
        "rc_lambda": rc_lambda,
        "rc_conv_b_w": nrm((N_EVEN, CONV_B_W, D_SC), CONV_B_W ** -0.5),
        "rc_w_out": nrm((N_EVEN, D_LRU + D_SC, D), (D_LRU + D_SC) ** -0.5),
        "at_w_qkv": nrm((N_ODD, D, D_Q + 2 * D_KV), D ** -0.5),
        "at_sink": nrm((N_ODD, N_Q_HEADS), 0.5),
        "at_w_out": nrm((N_ODD, D_Q, D), D_Q ** -0.5),
    }


def reference(x, c, ctx, c_ctx, ada_w, ada_b, norm_mix_g, norm_ffn_g, norm_final_g,
              ffn_w_in, ffn_w_out, rc_w_in, rc_conv_a_w, rc_conv_a_b, rc_gate_r_w, rc_gate_r_b,
              rc_gate_i_w, rc_gate_i_b, rc_lambda, rc_conv_b_w, rc_w_out,
              at_w_qkv, at_sink, at_w_out):
    S = x.shape[1]
    ROWS = S // GRID_W
    row = jnp.repeat(jnp.arange(ROWS), GRID_W).astype(jnp.float32)
    col = jnp.tile(jnp.arange(GRID_W), ROWS).astype(jnp.float32)
    inv_freq = ROPE_BASE ** (-jnp.arange(ROPE_FREQS, dtype=jnp.float32) / ROPE_FREQS)
    ang = jnp.stack([row[:, None] * inv_freq, col[:, None] * inv_freq], axis=1)
    cos, sin = jnp.cos(ang).astype(x.dtype), jnp.sin(ang).astype(x.dtype)

    silu_c = jax.nn.silu(c)
    silu_cc = jax.nn.silu(c_ctx)
    xl, xc = x, ctx
    for i in range(DEPTH):
        last = i == DEPTH - 1
        j = i // 2
        mod_l = jnp.split((silu_c @ ada_w[i] + ada_b[i])[:, None, :], N_MOD, axis=-1)
        mod_c = jnp.split(silu_cc @ ada_w[i] + ada_b[i], N_MOD, axis=-1)
        sh1_l, sc1_l, g1_l, sh2_l, sc2_l, g2_l = mod_l
        sh1_c, sc1_c, g1_c, sh2_c, sc2_c, g2_c = mod_c

        hl = modulate(rmsnorm(xl, norm_mix_g[i]), sh1_l, sc1_l)
        hc = modulate(rmsnorm(xc, norm_mix_g[i]), sh1_c, sc1_c)
        if i % 2 == 0:
            yc, yl = rc_mixer(hc, hl, rc_w_in[j], rc_conv_a_w[j], rc_conv_a_b[j],
                              rc_gate_r_w[j], rc_gate_r_b[j], rc_gate_i_w[j], rc_gate_i_b[j],
                              rc_lambda[j], rc_conv_b_w[j], rc_w_out[j], not last)
        else:
            yc, yl = attn_mixer(hc, hl, at_w_qkv[j], at_sink[j], at_w_out[j], cos, sin, not last)

        xl = xl + g1_l * yl
        hl = modulate(rmsnorm(xl, norm_ffn_g[i]), sh2_l, sc2_l)
        xl = xl + g2_l * swiglu(hl, ffn_w_in[i], ffn_w_out[i])
        if not last:
            xc = xc + g1_c * yc
            hc = modulate(rmsnorm(xc, norm_ffn_g[i]), sh2_c, sc2_c)
            xc = xc + g2_c * swiglu(hc, ffn_w_in[i], ffn_w_out[i])
    return rmsnorm(xl, norm_final_g)
```

```python
import functools

import jax
import jax.numpy as jnp
from jax import lax
from jax.experimental import pallas as pl
from jax.experimental.pallas import tpu as pltpu

D_MODEL = 1024
N_MOD = 6
EPS = 1e-6
NEG = -1e30
D_LRU = 512
D_SC = 512
LRU_HEADS = 8
LRU_HEAD_DIM = 64
LRU_C = 8.0
RC_IN_WIDTH = 2 * D_LRU + 3 * D_SC
HEAD_DIM = 64
N_Q_HEADS = 16
N_KV_HEADS = 4
GQA_GROUP = 4
WINDOW = 128
GRID_W = 64
ROPE_BASE = 10000.0
ROPE_FREQS = 16
D_FF = 2816

V7X_LANES = 128
V7X_SUBLANES = 8
V7X_MXU_DIM = 256
V7X_VMEM_LIMIT = 56 * 1024 * 1024

HALO = V7X_SUBLANES
FFN_CHUNKS = ((0, 1024), (1024, 1024), (2048, 768))

_BF16 = jnp.bfloat16
_F32 = jnp.float32


def _dot(a, b):
    return jnp.dot(a, b, preferred_element_type=_F32)


def _dot_t(a, b):
    return lax.dot_general(a, b, (((1,), (1,)), ((), ())), preferred_element_type=_F32)


def _sigmoid(x):
    return 0.5 * jnp.tanh(0.5 * x) + 0.5


def _gelu_tanh(x):
    return 0.5 * x * (1.0 + jnp.tanh(0.7978845608028654 * (x + 0.044715 * (x * x * x))))


def _rms_mod(x, gain, shift, scale):
    ms = jnp.mean(x * x, axis=-1, keepdims=True)
    return (x * lax.rsqrt(ms + EPS) * gain) * (1.0 + scale) + shift


def _const_spec(shape):
    nd = len(shape)
    return pl.BlockSpec(shape, lambda *_: (0,) * nd, pipeline_mode=pl.Buffered(1))


def _params(sem):
    return pltpu.CompilerParams(dimension_semantics=sem, vmem_limit_bytes=V7X_VMEM_LIMIT)


def _mod_kernel(c_ref, w_ref, b_ref, o_ref):
    c = c_ref[...]
    s = (c * _sigmoid(c)).astype(_BF16)
    o_ref[0] = _dot(s, w_ref[0].astype(_BF16)) + b_ref[0]


def _modulation(cond, ada_w, ada_b):
    depth, d, n = ada_w.shape
    bn = 1536
    return pl.pallas_call(
        _mod_kernel,
        grid=(depth, n // bn),
        in_specs=[
            pl.BlockSpec((8, d), lambda l, j: (0, 0)),
            pl.BlockSpec((1, d, bn), lambda l, j: (l, 0, j)),
            pl.BlockSpec((1, 1, bn), lambda l, j: (l, 0, j)),
        ],
        out_specs=pl.BlockSpec((1, 8, bn), lambda l, j: (l, 0, j)),
        out_shape=jax.ShapeDtypeStruct((depth, 8, n), _F32),
        compiler_params=_params(("arbitrary", "arbitrary")),
        name="adaln_mod",
    )(cond, ada_w, ada_b.reshape(depth, 1, n))


def _proj_kernel(x_ref, mod_ref, g_ref, w_ref, o_ref):
    h = _rms_mod(x_ref[...], g_ref[...], mod_ref[0, 0:1, :], mod_ref[0, 1:2, :]).astype(_BF16)
    o_ref[...] = _dot(h, w_ref[...])


def _project(x2d, mod, gain, w, tm, row_of_tile):
    n, d = x2d.shape
    nout = w.shape[1]
    return pl.pallas_call(
        _proj_kernel,
        grid=(n // tm,),
        in_specs=[
            pl.BlockSpec((tm, d), lambda i: (i, 0)),
            pl.BlockSpec((1, N_MOD, d), lambda i: (row_of_tile(i), 0, 0)),
            _const_spec((1, d)),
            _const_spec((d, nout)),
        ],
        out_specs=pl.BlockSpec((tm, nout), lambda i: (i, 0)),
        out_shape=jax.ShapeDtypeStruct((n, nout), _F32),
        compiler_params=_params(("parallel",)),
        name="rc_in_proj",
    )(x2d, mod, gain, w)


def _scan_kernel(xm_f, xp_f, xn_f, xm_b, xp_b, xn_b, cw_ref, cb_ref, wg_ref, bg_ref, lam_ref, h0_ref,
                 hf_ref, hb_ref, hl_ref, a_scr, b_scr, carry_scr, *, nc, tc):
    j = pl.program_id(1)

    @pl.when(j == 0)
    def _():
        carry_scr[...] = h0_ref[0]

    cw = cw_ref[...]
    cb = cb_ref[...]

    def coeffs(d, xm, xp, xn, has_prev, has_next):
        prev = jnp.where(has_prev, xp[...], 0.0)
        nxt = jnp.where(has_next, xn[...], 0.0)
        xext = jnp.concatenate([prev, xm[...], nxt], axis=0)
        xc = cb
        for k in range(4):
            xc = xc + cw[k:k + 1, :] * xext[HALO - 2 + k:HALO - 2 + k + tc, :]
        lam = lam_ref[d]
        nlam = -lam
        softplus = jnp.maximum(nlam, 0.0) + jnp.log(1.0 + jnp.exp(-jnp.abs(nlam)))
        rate = -LRU_C * softplus
        half = V7X_MXU_DIM
        for hh in range(D_LRU // half):
            cols = slice(hh * half, (hh + 1) * half)
            z = _dot(xc[:, cols].astype(_BF16), wg_ref[d, hh]) + bg_ref[d, hh]
            r = _sigmoid(z[:, :half])
            ig = _sigmoid(z[:, half:])
            log_a = r * rate[:, cols]
            a = jnp.exp(log_a)
            mult = jnp.sqrt(jnp.maximum(1.0 - a * a, 1e-12))
            a_scr[d, :, cols] = a
            b_scr[d, :, cols] = mult * ig * xc[:, cols]

    coeffs(0, xm_f, xp_f, xn_f, j > 0, j < nc - 1)
    coeffs(1, xm_b, xp_b, xn_b, j < nc - 1, j > 0)

    row = lax.broadcasted_iota(jnp.int32, (V7X_SUBLANES, D_LRU), 0)
    ngroups = tc // V7X_SUBLANES

    def body(g, carry):
        cf, cbk = carry
        r0 = pl.multiple_of(g * V7X_SUBLANES, V7X_SUBLANES)
        a = a_scr[0, pl.ds(r0, V7X_SUBLANES), :]
        b = b_scr[0, pl.ds(r0, V7X_SUBLANES), :]
        for s in (1, 2, 4):
            keep = row >= s
            b = b + a * jnp.where(keep, pltpu.roll(b, s, 0), 0.0)
            a = a * jnp.where(keep, pltpu.roll(a, s, 0), 1.0)
        h = a * cf + b
        hf_ref[pl.ds(r0, V7X_SUBLANES), :] = h
        cf = jnp.broadcast_to(h[V7X_SUBLANES - 1:V7X_SUBLANES, :], h.shape)
        r1 = pl.multiple_of((ngroups - 1 - g) * V7X_SUBLANES, V7X_SUBLANES)
        a = a_scr[1, pl.ds(r1, V7X_SUBLANES), :]
        b = b_scr[1, pl.ds(r1, V7X_SUBLANES), :]
        for s in (1, 2, 4):
            keep = row < V7X_SUBLANES - s
            b = b + a * jnp.where(keep, pltpu.roll(b, V7X_SUBLANES - s, 0), 0.0)
            a = a * jnp.where(keep, pltpu.roll(a, V7X_SUBLANES - s, 0), 1.0)
        h = a * cbk + b
        hb_ref[pl.ds(r1, V7X_SUBLANES), :] = h
        cbk = jnp.broadcast_to(h[0:1, :], h.shape)
        return cf, cbk

    cf, cbk = lax.fori_loop(0, ngroups, body, (carry_scr[0], carry_scr[1]), unroll=4)
    carry_scr[0] = cf
    carry_scr[1] = cbk
    hl_ref[0, 0] = cf
    hl_ref[0, 1] = cbk


def _scan(p, h0, conv_w, conv_b, w_gate, b_gate, lam, batch, seq, tc):
    n = p.shape[0]
    nc = seq // tc
    tb = tc // HALO
    sb = seq // HALO
    last = n // HALO - 1

    def fwd(b, j):
        return j

    def bwd(b, j):
        return nc - 1 - j

    def main(cf):
        return pl.BlockSpec((tc, D_LRU), lambda b, j: (b * nc + cf(b, j), 0))

    def prev(cf):
        return pl.BlockSpec((HALO, D_LRU), lambda b, j: (jnp.maximum(b * sb + cf(b, j) * tb - 1, 0), 0))

    def nxt(cf):
        return pl.BlockSpec((HALO, D_LRU), lambda b, j: (jnp.minimum(b * sb + (cf(b, j) + 1) * tb, last), 0))

    kernel = functools.partial(_scan_kernel, nc=nc, tc=tc)
    return pl.pallas_call(
        kernel,
        grid=(batch, nc),
        in_specs=[
            main(fwd), prev(fwd), nxt(fwd), main(bwd), prev(bwd), nxt(bwd),
            _const_spec((4, D_LRU)),
            _const_spec((1, D_LRU)),
            _const_spec(w_gate.shape),
            _const_spec(b_gate.shape),
            _const_spec((2, 1, D_LRU)),
            pl.BlockSpec((1, 2, V7X_SUBLANES, D_LRU), lambda b, j: (b, 0, 0, 0)),
        ],
        out_specs=[
            pl.BlockSpec((tc, D_LRU), lambda b, j: (b * nc + j, 0)),
            pl.BlockSpec((tc, D_LRU), lambda b, j: (b * nc + nc - 1 - j, 0)),
            pl.BlockSpec((1, 2, V7X_SUBLANES, D_LRU), lambda b, j: (b, 0, 0, 0)),
        ],
        out_shape=[
            jax.ShapeDtypeStruct((n, D_LRU), _F32),
            jax.ShapeDtypeStruct((n, D_LRU), _F32),
            jax.ShapeDtypeStruct((batch, 2, V7X_SUBLANES, D_LRU), _F32),
        ],
        scratch_shapes=[
            pltpu.VMEM((2, tc, D_LRU), _F32),
            pltpu.VMEM((2, tc, D_LRU), _F32),
            pltpu.VMEM((2, V7X_SUBLANES, D_LRU), _F32),
        ],
        compiler_params=_params(("arbitrary", "arbitrary")),
        name="rglru_scan",
    )(p, p, p, p, p, p, conv_w, conv_b, w_gate, b_gate, lam, h0)


def _ffn_tail(x, y, mod_ref, gffn_ref, w1_ref, w2_ref, gfin_ref):
    x1 = x + mod_ref[0, 2:3, :] * y
    h = _rms_mod(x1, gffn_ref[...], mod_ref[0, 3:4, :], mod_ref[0, 4:5, :]).astype(_BF16)
    acc = None
    for c0, cw in FFN_CHUNKS:
        g = _dot(h, w1_ref[:, c0:c0 + cw])
        u = _dot(h, w1_ref[:, D_FF + c0:D_FF + c0 + cw])
        act = (g * _sigmoid(g) * u).astype(_BF16)
        part = _dot(act, w2_ref[c0:c0 + cw, :])
        acc = part if acc is None else acc + part
    x2 = x1 + mod_ref[0, 5:6, :] * acc
    if gfin_ref is not None:
        ms = jnp.mean(x2 * x2, axis=-1, keepdims=True)
        x2 = x2 * lax.rsqrt(ms + EPS) * gfin_ref[...]
    return x2


def _rc_out_kernel(x_ref, hf_ref, hb_ref, ga_ref, bg_ref, cgm, cgp, cgn, vm, vp, vn,
                   cbw_ref, wo_ref, mod_ref, gffn_ref, w1_ref, w2_ref, o_ref, *, nt, tm):
    i = pl.program_id(0)
    has_prev = (i % nt) > 0
    has_next = (i % nt) < nt - 1
    ya = ((hf_ref[...] + hb_ref[...]) * _gelu_tanh(ga_ref[...])).astype(_BF16)
    prev = jnp.where(has_prev, cgp[...] * vp[...], 0.0)
    nxt = jnp.where(has_next, cgn[...] * vn[...], 0.0)
    uext = jnp.concatenate([prev, cgm[...] * vm[...], nxt], axis=0)
    cbw = cbw_ref[...]
    conv = None
    for k in range(3):
        term = cbw[k:k + 1, :] * uext[HALO - 1 + k:HALO - 1 + k + tm, :]
        conv = term if conv is None else conv + term
    yb = (bg_ref[...] * conv).astype(_BF16)
    y = _dot(ya, wo_ref[0:D_LRU, :]) + _dot(yb, wo_ref[D_LRU:D_LRU + D_SC, :])
    o_ref[...] = _ffn_tail(x_ref[...], y, mod_ref, gffn_ref, w1_ref, w2_ref, None)


def _rc_out(x2d, hf, hb, p, conv_b_w, w_out, mod, gffn, w1, w2, tm, seq, row_of_tile):
    n, d = x2d.shape
    nt = seq // tm
    tb = tm // HALO
    last = n // HALO - 1

    def col(c):
        return pl.BlockSpec((tm, D_SC), lambda i: (i, c))

    def prev(c):
        return pl.BlockSpec((HALO, D_SC), lambda i: (jnp.maximum(i * tb - 1, 0), c))

    def nxt(c):
        return pl.BlockSpec((HALO, D_SC), lambda i: (jnp.minimum((i + 1) * tb, last), c))

    kernel = functools.partial(_rc_out_kernel, nt=nt, tm=tm)
    return pl.pallas_call(
        kernel,
        grid=(n // tm,),
        in_specs=[
            pl.BlockSpec((tm, d), lambda i: (i, 0)),
            pl.BlockSpec((tm, D_LRU), lambda i: (i, 0)),
            pl.BlockSpec((tm, D_LRU), lambda i: (i, 0)),
            col(1), col(2), col(3), prev(3), nxt(3), col(4), prev(4), nxt(4),
            _const_spec((3, D_SC)),
            _const_spec(w_out.shape),
            pl.BlockSpec((1, N_MOD, d), lambda i: (row_of_tile(i), 0, 0)),
            _const_spec((1, d)),
            _const_spec(w1.shape),
            _const_spec(w2.shape),
        ],
        out_specs=pl.BlockSpec((tm, d), lambda i: (i, 0)),
        out_shape=jax.ShapeDtypeStruct((n, d), _F32),
        compiler_params=_params(("parallel",)),
        name="rc_out_ffn",
    )(x2d, hf, hb, p, p, p, p, p, p, p, p, conv_b_w, w_out, mod, gffn, w1, w2)


def _attn_out_kernel(x_ref, o_in_ref, wo_ref, mod_ref, gffn_ref, w1_ref, w2_ref, gfin_ref, o_ref):
    y = _dot(o_in_ref[...], wo_ref[...])
    o_ref[...] = _ffn_tail(x_ref[...], y, mod_ref, gffn_ref, w1_ref, w2_ref, gfin_ref)


def _attn_out(x2d, o, w_out, mod, gffn, w1, w2, gfin, tm, row_of_tile):
    n, d = x2d.shape
    return pl.pallas_call(
        _attn_out_kernel,
        grid=(n // tm,),
        in_specs=[
            pl.BlockSpec((tm, d), lambda i: (i, 0)),
            pl.BlockSpec((tm, d), lambda i: (i, 0)),
            _const_spec(w_out.shape),
            pl.BlockSpec((1, N_MOD, d), lambda i: (row_of_tile(i), 0, 0)),
            _const_spec((1, d)),
            _const_spec(w1.shape),
            _const_spec(w2.shape),
            _const_spec((1, d)),
        ],
        out_specs=pl.BlockSpec((tm, d), lambda i: (i, 0)),
        out_shape=jax.ShapeDtypeStruct((n, d), _F32),
        compiler_params=_params(("parallel",)),
        name="attn_out_ffn",
    )(x2d, o, w_out, mod, gffn, w1, w2, gfin)


def _rope(x, cos, sin_signed, lo_half):
    partner = jnp.where(lo_half, pltpu.roll(x, V7X_LANES - ROPE_FREQS, 1), pltpu.roll(x, ROPE_FREQS, 1))
    return x * cos + partner * sin_signed


def _qkv_kernel(x_ref, mod_ref, g_ref, w_ref, cos_ref, sin_ref, q_ref, k_ref, v_ref):
    h = _rms_mod(x_ref[...], g_ref[...], mod_ref[0, 0:1, :], mod_ref[0, 1:2, :]).astype(_BF16)
    cos = cos_ref[...]
    sin = sin_ref[...]
    lane = lax.broadcasted_iota(jnp.int32, cos.shape, 1)
    lo_half = (lane % (2 * ROPE_FREQS)) < ROPE_FREQS
    dq = N_Q_HEADS * HEAD_DIM
    dk = N_KV_HEADS * V7X_LANES
    scale = HEAD_DIM ** -0.5
    q = _dot(h, w_ref[:, 0:dq])
    for c in range(dq // V7X_LANES):
        cols = slice(c * V7X_LANES, (c + 1) * V7X_LANES)
        q_ref[:, cols] = (_rope(q[:, cols], cos, sin, lo_half) * scale).astype(_BF16)
    k = _dot(h, w_ref[:, dq:dq + dk])
    for c in range(dk // V7X_LANES):
        cols = slice(c * V7X_LANES, (c + 1) * V7X_LANES)
        k_ref[:, cols] = _rope(k[:, cols], cos, sin, lo_half).astype(_BF16)
    v_ref[...] = _dot(h, w_ref[:, dq + dk:dq + 2 * dk]).astype(_BF16)


def _qkv(x2d, mod, gain, w, cos_t, sin_t, tm, seq, row_of_tile):
    n, d = x2d.shape
    nt = seq // tm
    dq = N_Q_HEADS * HEAD_DIM
    dk = N_KV_HEADS * V7X_LANES
    return pl.pallas_call(
        _qkv_kernel,
        grid=(n // tm,),
        in_specs=[
            pl.BlockSpec((tm, d), lambda i: (i, 0)),
            pl.BlockSpec((1, N_MOD, d), lambda i: (row_of_tile(i), 0, 0)),
            _const_spec((1, d)),
            _const_spec(w.shape),
            pl.BlockSpec((tm, V7X_LANES), lambda i: (i % nt, 0)),
            pl.BlockSpec((tm, V7X_LANES), lambda i: (i % nt, 0)),
        ],
        out_specs=[
            pl.BlockSpec((tm, dq), lambda i: (i, 0)),
            pl.BlockSpec((tm, dk), lambda i: (i, 0)),
            pl.BlockSpec((tm, dk), lambda i: (i, 0)),
        ],
        out_shape=[
            jax.ShapeDtypeStruct((n, dq), _BF16),
            jax.ShapeDtypeStruct((n, dk), _BF16),
            jax.ShapeDtypeStruct((n, dk), _BF16),
        ],
        compiler_params=_params(("parallel",)),
        name="qkv_rope",
    )(x2d, mod, gain, w, cos_t, sin_t)


def _kv_kernel(x_ref, mod_ref, g_ref, w_ref, k_ref, v_ref):
    h = _rms_mod(x_ref[...], g_ref[...], mod_ref[0, 0:1, :], mod_ref[0, 1:2, :]).astype(_BF16)
    dk = N_KV_HEADS * V7X_LANES
    k_ref[...] = _dot(h, w_ref[:, 0:dk]).astype(_BF16)
    v_ref[...] = _dot(h, w_ref[:, dk:2 * dk]).astype(_BF16)


def _ctx_kv(x2d, mod, gain, w):
    n, d = x2d.shape
    dk = N_KV_HEADS * V7X_LANES
    return pl.pallas_call(
        _kv_kernel,
        grid=(1,),
        in_specs=[
            pl.BlockSpec((n, d), lambda i: (0, 0)),
            pl.BlockSpec((1, N_MOD, d), lambda i: (0, 0, 0)),
            _const_spec((1, d)),
            _const_spec(w.shape),
        ],
        out_specs=[pl.BlockSpec((n, dk), lambda i: (0, 0)), pl.BlockSpec((n, dk), lambda i: (0, 0))],
        out_shape=[jax.ShapeDtypeStruct((n, dk), _BF16), jax.ShapeDtypeStruct((n, dk), _BF16)],
        compiler_params=_params(("arbitrary",)),
        name="ctx_kv",
    )(x2d, mod, gain, w)


def _attn_kernel(sink_ref, q_ref, kp_ref, kc_ref, kn_ref, vp_ref, vc_ref, vn_ref, kx_ref, vx_ref, o_ref,
                 *, tq, seq):
    n = pl.program_id(1)
    blk = WINDOW
    kext = jnp.concatenate([kp_ref[...], kc_ref[...], kn_ref[...]], axis=0)
    vext = jnp.concatenate([vp_ref[...], vc_ref[...], vn_ref[...]], axis=0)
    qi = lax.broadcasted_iota(jnp.int32, (blk, 3 * blk), 0)
    kj = lax.broadcasted_iota(jnp.int32, (blk, 3 * blk), 1)
    in_window = jnp.abs(kj - blk - qi) <= WINDOW
    lane = lax.broadcasted_iota(jnp.int32, (blk, V7X_LANES), 1)
    lo = lane < HEAD_DIM
    for j in range(tq // blk):
        base = n * tq + (j - 1) * blk
        pos = kj + base
        mask = in_window & (pos >= 0) & (pos < seq)
        rows = slice(j * blk, (j + 1) * blk)
        for hkv in range(N_KV_HEADS):
            kcols = slice(hkv * V7X_LANES, (hkv + 1) * V7X_LANES)
            kwin = kext[j * blk:(j + 3) * blk, kcols]
            vwin = vext[j * blk:(j + 3) * blk, kcols]
            kx = kx_ref[0, :, kcols]
            vx = vx_ref[0, :, kcols]
            parts = []
            for c in range(2):
                qc = q_ref[rows, (2 * hkv + c) * V7X_LANES:(2 * hkv + c + 1) * V7X_LANES]
                parts.append(jnp.where(lo, qc, jnp.zeros_like(qc)))
                parts.append(jnp.where(lo, jnp.zeros_like(qc), qc))
            lhs = jnp.concatenate(parts, axis=0)
            s_win = _dot_t(lhs, kwin)
            s_ctx = _dot_t(lhs, kx)
            p_win, p_ctx, inv = [], [], []
            for g in range(GQA_GROUP):
                sink = sink_ref[hkv * GQA_GROUP + g]
                grow = slice(g * blk, (g + 1) * blk)
                sw = jnp.where(mask, s_win[grow], NEG)
                sc = s_ctx[grow]
                m = jnp.maximum(jnp.maximum(jnp.max(sw, axis=-1, keepdims=True),
                                            jnp.max(sc, axis=-1, keepdims=True)), sink)
                ew = jnp.exp(sw - m)
                ec = jnp.exp(sc - m)
                den = (jnp.sum(ew, axis=-1, keepdims=True) + jnp.sum(ec, axis=-1, keepdims=True)
                       + jnp.exp(sink - m))
                p_win.append(ew.astype(_BF16))
                p_ctx.append(ec.astype(_BF16))
                inv.append(1.0 / den)
            o = _dot(jnp.concatenate(p_win, axis=0), vwin) + _dot(jnp.concatenate(p_ctx, axis=0), vx)
            og = [o[g * blk:(g + 1) * blk] * inv[g] for g in range(GQA_GROUP)]
            for c in range(2):
                ocol = jnp.where(lo, og[2 * c], og[2 * c + 1])
                o_ref[rows, (2 * hkv + c) * V7X_LANES:(2 * hkv + c + 1) * V7X_LANES] = ocol.astype(_BF16)


def _attention(sink, q, kd, vd, kx, vx, batch, seq, tq):
    n, dq = q.shape
    dk = kd.shape[1]
    ctx_len = kx.shape[1]
    nq = seq // tq
    hb = tq // WINDOW
    sb = seq // WINDOW
    last = n // WINDOW - 1

    def cur():
        return pl.BlockSpec((tq, dk), lambda b, i: (b * nq + i, 0))

    def prev():
        return pl.BlockSpec((WINDOW, dk), lambda b, i: (jnp.maximum(b * sb + i * hb - 1, 0), 0))

    def nxt():
        return pl.BlockSpec((WINDOW, dk), lambda b, i: (jnp.minimum(b * sb + (i + 1) * hb, last), 0))

    kernel = functools.partial(_attn_kernel, tq=tq, seq=seq)
    return pl.pallas_call(
        kernel,
        grid=(batch, nq),
        in_specs=[
            pl.BlockSpec(memory_space=pltpu.SMEM),
            pl.BlockSpec((tq, dq), lambda b, i: (b * nq + i, 0)),
            prev(), cur(), nxt(), prev(), cur(), nxt(),
            pl.BlockSpec((1, ctx_len, dk), lambda b, i: (b, 0, 0)),
            pl.BlockSpec((1, ctx_len, dk), lambda b, i: (b, 0, 0)),
        ],
        out_specs=pl.BlockSpec((tq, dq), lambda b, i: (b * nq + i, 0)),
        out_shape=jax.ShapeDtypeStruct((n, dq), _BF16),
        compiler_params=_params(("parallel", "parallel")),
        name="band_attn",
    )(sink, q, kd, kd, kd, vd, vd, vd, kx, vx)


def _gate_weights(r_w, r_b, i_w, i_b):
    heads_per_half = V7X_MXU_DIM // LRU_HEAD_DIM
    eye = jnp.eye(heads_per_half, dtype=_F32)

    def halves(w):
        w = w.reshape(2, D_LRU // V7X_MXU_DIM, heads_per_half, LRU_HEAD_DIM, LRU_HEAD_DIM)
        bd = jnp.einsum('dxhij,hk->dxhikj', w, eye)
        return bd.reshape(2, D_LRU // V7X_MXU_DIM, V7X_MXU_DIM, V7X_MXU_DIM)

    w = jnp.concatenate([halves(r_w), halves(i_w)], axis=-1).astype(_BF16)
    rb = r_b.reshape(2, D_LRU // V7X_MXU_DIM, 1, V7X_MXU_DIM)
    ib = i_b.reshape(2, D_LRU // V7X_MXU_DIM, 1, V7X_MXU_DIM)
    return w, jnp.concatenate([rb, ib], axis=-1)


def _dup_heads(w):
    d = w.shape[0]
    w = w.reshape(d, N_KV_HEADS, 1, HEAD_DIM)
    return jnp.broadcast_to(w, (d, N_KV_HEADS, 2, HEAD_DIM)).reshape(d, N_KV_HEADS * V7X_LANES)


def _rope_tables(seq):
    rows = seq // GRID_W
    row = jnp.repeat(jnp.arange(rows), GRID_W).astype(_F32)
    col = jnp.tile(jnp.arange(GRID_W), rows).astype(_F32)
    inv_freq = ROPE_BASE ** (-jnp.arange(ROPE_FREQS, dtype=_F32) / ROPE_FREQS)
    ang = jnp.stack([row[:, None] * inv_freq, col[:, None] * inv_freq], axis=1)
    cos, sin = jnp.cos(ang), jnp.sin(ang)
    cos_h = jnp.broadcast_to(cos[:, :, None, :], (seq, 2, 2, ROPE_FREQS)).reshape(seq, HEAD_DIM)
    sin_h = jnp.stack([-sin, sin], axis=2).reshape(seq, HEAD_DIM)
    return jnp.tile(cos_h, (1, 2)), jnp.tile(sin_h, (1, 2))


def kernel(x, c, ctx, c_ctx, ada_w, ada_b, norm_mix_g, norm_ffn_g, norm_final_g, ffn_w_in, ffn_w_out,
           rc_w_in, rc_conv_a_w, rc_conv_a_b, rc_gate_r_w, rc_gate_r_b, rc_gate_i_w, rc_gate_i_b,
           rc_lambda, rc_conv_b_w, rc_w_out, at_w_qkv, at_sink, at_w_out):
    batch, seq, d = x.shape
    ctx_len = ctx.shape[1]
    tm = 512
    tq = 512
    tc = 512
    assert seq % tm == 0 and seq % tq == 0 and seq % tc == 0 and ctx_len % V7X_LANES == 0

    xl = x.reshape(batch * seq, d)
    xc = ctx.reshape(batch * ctx_len, d)

    cond = jnp.zeros((8, d), _F32).at[0].set(c_ctx).at[1:1 + batch].set(c)
    mod = _modulation(cond, ada_w, ada_b).reshape(ada_w.shape[0], 8, N_MOD, d)

    lat_row = lambda i: 1 + i // (seq // tm)
    ctx_row = lambda i: 0

    w1 = ffn_w_in.astype(_BF16)
    w2 = ffn_w_out.astype(_BF16)

    w_in = rc_w_in[0].astype(_BF16)
    w_gate, b_gate = _gate_weights(rc_gate_r_w[0], rc_gate_r_b[0], rc_gate_i_w[0], rc_gate_i_b[0])
    lam = rc_lambda[0].reshape(2, 1, D_LRU)
    conv_a_b = rc_conv_a_b[0].reshape(1, D_LRU)
    w_out0 = rc_w_out[0].astype(_BF16)
    gmix0 = norm_mix_g[0].reshape(1, d)
    gffn0 = norm_ffn_g[0].reshape(1, d)

    p_c = _project(xc, mod[0], gmix0, w_in, batch * ctx_len, ctx_row)
    p_l = _project(xl, mod[0], gmix0, w_in, tm, lat_row)

    h0 = jnp.zeros((batch, 2, V7X_SUBLANES, D_LRU), _F32)
    hf_c, hb_c, h_ctx = _scan(p_c, h0, rc_conv_a_w[0], conv_a_b, w_gate, b_gate, lam, batch, ctx_len, ctx_len)
    hf_l, hb_l, _ = _scan(p_l, h_ctx, rc_conv_a_w[0], conv_a_b, w_gate, b_gate, lam, batch, seq, tc)

    xc = _rc_out(xc, hf_c, hb_c, p_c, rc_conv_b_w[0], w_out0, mod[0], gffn0, w1[0], w2[0],
                 ctx_len, ctx_len, ctx_row)
    xl = _rc_out(xl, hf_l, hb_l, p_l, rc_conv_b_w[0], w_out0, mod[0], gffn0, w1[0], w2[0],
                 tm, seq, lat_row)

    dq = N_Q_HEADS * HEAD_DIM
    dkv = N_KV_HEADS * HEAD_DIM
    wq, wk, wv = at_w_qkv[0][:, :dq], at_w_qkv[0][:, dq:dq + dkv], at_w_qkv[0][:, dq + dkv:]
    wk2, wv2 = _dup_heads(wk), _dup_heads(wv)
    w_qkv = jnp.concatenate([wq, wk2, wv2], axis=1).astype(_BF16)
    w_kv = jnp.concatenate([wk2, wv2], axis=1).astype(_BF16)
    gmix1 = norm_mix_g[1].reshape(1, d)
    gffn1 = norm_ffn_g[1].reshape(1, d)
    cos_t, sin_t = _rope_tables(seq)

    q, kd, vd = _qkv(xl, mod[1], gmix1, w_qkv, cos_t, sin_t, tm, seq, lat_row)
    kx, vx = _ctx_kv(xc, mod[1], gmix1, w_kv)
    kx = kx.reshape(batch, ctx_len, -1)
    vx = vx.reshape(batch, ctx_len, -1)
    o = _attention(at_sink[0], q, kd, vd, kx, vx, batch, seq, tq)

    out = _attn_out(xl, o, at_w_out[0].astype(_BF16), mod[1], gffn1, w1[1], w2[1],
                    norm_final_g.reshape(1, d), tm, lat_row)
    return out.reshape(batch, seq, d)
```

```python
import functools

import jax
import jax.numpy as jnp
from jax import lax
from jax.experimental import pallas as pl
from jax.experimental.pallas import tpu as pltpu

D_MODEL = 1024
N_MOD = 6
EPS = 1e-6
NEG = -1e30
D_LRU = 512
D_SC = 512
LRU_HEADS = 8
LRU_HEAD_DIM = 64
LRU_C = 8.0
RC_IN_WIDTH = 2 * D_LRU + 3 * D_SC
HEAD_DIM = 64
N_Q_HEADS = 16
N_KV_HEADS = 4
GQA_GROUP = 4
WINDOW = 128
GRID_W = 64
ROPE_BASE = 10000.0
ROPE_FREQS = 16
D_FF = 2816
LOG2E = 1.4426950408889634

V7X_LANES = 128
V7X_SUBLANES = 8
V7X_MXU_DIM = 256
V7X_VMEM_LIMIT = 56 * 1024 * 1024

HALO = V7X_SUBLANES
FFN_CHUNKS = ((0, 1024), (1024, 1024), (2048, 768))

_BF16 = jnp.bfloat16
_F32 = jnp.float32


def _dot(a, b):
    return jnp.dot(a, b, preferred_element_type=_F32)


def _dot_t(a, b):
    return lax.dot_general(a, b, (((1,), (1,)), ((), ())), preferred_element_type=_F32)


def _sigmoid(x):
    return 0.5 * jnp.tanh(0.5 * x) + 0.5


def _gelu_tanh(x):
    return 0.5 * x * (1.0 + jnp.tanh(0.7978845608028654 * (x + 0.044715 * (x * x * x))))


def _rms_mod(x, gain, shift, scale):
    ms = jnp.mean(x * x, axis=-1, keepdims=True)
    return (x * lax.rsqrt(ms + EPS) * gain) * (1.0 + scale) + shift


def _const_spec(shape):
    nd = len(shape)
    return pl.BlockSpec(shape, lambda *_: (0,) * nd, pipeline_mode=pl.Buffered(1))


def _params(sem):
    return pltpu.CompilerParams(dimension_semantics=sem, vmem_limit_bytes=V7X_VMEM_LIMIT)


def _mod_kernel(c_ref, w_ref, b_ref, o_ref):
    c = c_ref[...]
    s = (c * _sigmoid(c)).astype(_BF16)
    o_ref[0] = _dot(s, w_ref[0].astype(_BF16)) + b_ref[0]


def _modulation(cond, ada_w, ada_b):
    depth, d, n = ada_w.shape
    bn = 1536
    return pl.pallas_call(
        _mod_kernel,
        grid=(depth, n // bn),
        in_specs=[
            pl.BlockSpec((8, d), lambda l, j: (0, 0)),
            pl.BlockSpec((1, d, bn), lambda l, j: (l, 0, j)),
            pl.BlockSpec((1, 1, bn), lambda l, j: (l, 0, j)),
        ],
        out_specs=pl.BlockSpec((1, 8, bn), lambda l, j: (l, 0, j)),
        out_shape=jax.ShapeDtypeStruct((depth, 8, n), _F32),
        compiler_params=_params(("arbitrary", "arbitrary")),
        name="adaln_mod",
    )(cond, ada_w, ada_b.reshape(depth, 1, n))


def _proj_kernel(x_ref, mod_ref, g_ref, w_ref, o_ref):
    h = _rms_mod(x_ref[...], g_ref[...], mod_ref[0, 0:1, :], mod_ref[0, 1:2, :]).astype(_BF16)
    o_ref[...] = _dot(h, w_ref[...])


def _project(x2d, mod, gain, w, tm, row_of_tile):
    n, d = x2d.shape
    nout = w.shape[1]
    return pl.pallas_call(
        _proj_kernel,
        grid=(n // tm,),
        in_specs=[
            pl.BlockSpec((tm, d), lambda i: (i, 0)),
            pl.BlockSpec((1, N_MOD, d), lambda i: (row_of_tile(i), 0, 0)),
            _const_spec((1, d)),
            _const_spec((d, nout)),
        ],
        out_specs=pl.BlockSpec((tm, nout), lambda i: (i, 0)),
        out_shape=jax.ShapeDtypeStruct((n, nout), _F32),
        compiler_params=_params(("parallel",)),
        name="rc_in_proj",
    )(x2d, mod, gain, w)


def _scan_kernel(xm_f, xp_f, xn_f, xm_b, xp_b, xn_b, cw_ref, cb_ref, wg_ref, bg_ref, lam_ref, h0_ref,
                 hf_ref, hb_ref, hl_ref, a_scr, b_scr, carry_scr, *, nc, tc):
    j = pl.program_id(1)

    @pl.when(j == 0)
    def _():
        carry_scr[...] = h0_ref[0]

    cw = cw_ref[...]
    cb = cb_ref[...]

    def coeffs(d, xm, xp, xn, has_prev, has_next):
        prev = jnp.where(has_prev, xp[...], 0.0)
        nxt = jnp.where(has_next, xn[...], 0.0)
        xext = jnp.concatenate([prev, xm[...], nxt], axis=0)
        xc = cb
        for k in range(4):
            xc = xc + cw[k:k + 1, :] * xext[HALO - 2 + k:HALO - 2 + k + tc, :]
        lam = lam_ref[d]
        nlam = -lam
        softplus = jnp.maximum(nlam, 0.0) + jnp.log(1.0 + jnp.exp(-jnp.abs(nlam)))
        rate = -LRU_C * softplus
        half = V7X_MXU_DIM
        for hh in range(D_LRU // half):
            cols = slice(hh * half, (hh + 1) * half)
            z = _dot(xc[:, cols].astype(_BF16), wg_ref[d, hh]) + bg_ref[d, hh]
            r = 0.5 * jnp.tanh(z[:, :half]) + 0.5
            ig = 0.5 * jnp.tanh(z[:, half:]) + 0.5
            log_a = r * rate[:, cols]
            a = jnp.exp(log_a)
            one_m_a2 = jnp.maximum(1.0 - a * a, 1e-12)
            mult = one_m_a2 * lax.rsqrt(one_m_a2)
            a_scr[d, :, cols] = a
            b_scr[d, :, cols] = mult * ig * xc[:, cols]

    coeffs(0, xm_f, xp_f, xn_f, j > 0, j < nc - 1)
    coeffs(1, xm_b, xp_b, xn_b, j < nc - 1, j > 0)

    row = lax.broadcasted_iota(jnp.int32, (V7X_SUBLANES, D_LRU), 0)
    ngroups = tc // V7X_SUBLANES

    def body(g, carry):
        cf, cbk = carry
        r0 = pl.multiple_of(g * V7X_SUBLANES, V7X_SUBLANES)
        a = a_scr[0, pl.ds(r0, V7X_SUBLANES), :]
        b = b_scr[0, pl.ds(r0, V7X_SUBLANES), :]
        for s in (1, 2, 4):
            keep = row >= s
            b = b + a * jnp.where(keep, pltpu.roll(b, s, 0), 0.0)
            a = a * jnp.where(keep, pltpu.roll(a, s, 0), 1.0)
        h = a * cf + b
        hf_ref[pl.ds(r0, V7X_SUBLANES), :] = h
        cf = jnp.broadcast_to(h[V7X_SUBLANES - 1:V7X_SUBLANES, :], h.shape)
        r1 = pl.multiple_of((ngroups - 1 - g) * V7X_SUBLANES, V7X_SUBLANES)
        a = a_scr[1, pl.ds(r1, V7X_SUBLANES), :]
        b = b_scr[1, pl.ds(r1, V7X_SUBLANES), :]
        for s in (1, 2, 4):
            keep = row < V7X_SUBLANES - s
            b = b + a * jnp.where(keep, pltpu.roll(b, V7X_SUBLANES - s, 0), 0.0)
            a = a * jnp.where(keep, pltpu.roll(a, V7X_SUBLANES - s, 0), 1.0)
        h = a * cbk + b
        hb_ref[pl.ds(r1, V7X_SUBLANES), :] = h
        cbk = jnp.broadcast_to(h[0:1, :], h.shape)
        return cf, cbk

    cf, cbk = lax.fori_loop(0, ngroups, body, (carry_scr[0], carry_scr[1]), unroll=4)
    carry_scr[0] = cf
    carry_scr[1] = cbk
    hl_ref[0, 0] = cf
    hl_ref[0, 1] = cbk


def _scan(p, h0, conv_w, conv_b, w_gate, b_gate, lam, batch, seq, tc):
    n = p.shape[0]
    nc = seq // tc
    tb = tc // HALO
    sb = seq // HALO
    last = n // HALO - 1

    def fwd(b, j):
        return j

    def bwd(b, j):
        return nc - 1 - j

    def main(cf):
        return pl.BlockSpec((tc, D_LRU), lambda b, j: (b * nc + cf(b, j), 0))

    def prev(cf):
        return pl.BlockSpec((HALO, D_LRU), lambda b, j: (jnp.maximum(b * sb + cf(b, j) * tb - 1, 0), 0))

    def nxt(cf):
        return pl.BlockSpec((HALO, D_LRU), lambda b, j: (jnp.minimum(b * sb + (cf(b, j) + 1) * tb, last), 0))

    kernel = functools.partial(_scan_kernel, nc=nc, tc=tc)
    return pl.pallas_call(
        kernel,
        grid=(batch, nc),
        in_specs=[
            main(fwd), prev(fwd), nxt(fwd), main(bwd), prev(bwd), nxt(bwd),
            _const_spec((4, D_LRU)),
            _const_spec((1, D_LRU)),
            _const_spec(w_gate.shape),
            _const_spec(b_gate.shape),
            _const_spec((2, 1, D_LRU)),
            pl.BlockSpec((1, 2, V7X_SUBLANES, D_LRU), lambda b, j: (b, 0, 0, 0)),
        ],
        out_specs=[
            pl.BlockSpec((tc, D_LRU), lambda b, j: (b * nc + j, 0)),
            pl.BlockSpec((tc, D_LRU), lambda b, j: (b * nc + nc - 1 - j, 0)),
            pl.BlockSpec((1, 2, V7X_SUBLANES, D_LRU), lambda b, j: (b, 0, 0, 0)),
        ],
        out_shape=[
            jax.ShapeDtypeStruct((n, D_LRU), _F32),
            jax.ShapeDtypeStruct((n, D_LRU), _F32),
            jax.ShapeDtypeStruct((batch, 2, V7X_SUBLANES, D_LRU), _F32),
        ],
        scratch_shapes=[
            pltpu.VMEM((2, tc, D_LRU), _F32),
            pltpu.VMEM((2, tc, D_LRU), _F32),
            pltpu.VMEM((2, V7X_SUBLANES, D_LRU), _F32),
        ],
        compiler_params=_params(("arbitrary", "arbitrary")),
        name="rglru_scan",
    )(p, p, p, p, p, p, conv_w, conv_b, w_gate, b_gate, lam, h0)


def _ffn_tail(x, y, mod_ref, gffn_ref, w1_ref, w2_ref, gfin_ref):
    x1 = x + mod_ref[0, 2:3, :] * y
    h = _rms_mod(x1, gffn_ref[...], mod_ref[0, 3:4, :], mod_ref[0, 4:5, :]).astype(_BF16)
    acc = None
    for c0, cw in FFN_CHUNKS:
        g = _dot(h, w1_ref[:, c0:c0 + cw])
        u = _dot(h, w1_ref[:, D_FF + c0:D_FF + c0 + cw])
        act = (g * _sigmoid(g) * u).astype(_BF16)
        part = _dot(act, w2_ref[c0:c0 + cw, :])
        acc = part if acc is None else acc + part
    x2 = x1 + mod_ref[0, 5:6, :] * acc
    if gfin_ref is not None:
        ms = jnp.mean(x2 * x2, axis=-1, keepdims=True)
        x2 = x2 * lax.rsqrt(ms + EPS) * gfin_ref[...]
    return x2


def _rc_out_kernel(x_ref, hf_ref, hb_ref, ga_ref, bg_ref, cgm, cgp, cgn, vm, vp, vn,
                   cbw_ref, wo_ref, mod_ref, gffn_ref, w1_ref, w2_ref, o_ref, *, nt, tm):
    i = pl.program_id(0)
    has_prev = (i % nt) > 0
    has_next = (i % nt) < nt - 1
    ya = ((hf_ref[...] + hb_ref[...]) * _gelu_tanh(ga_ref[...])).astype(_BF16)
    prev = jnp.where(has_prev, cgp[...] * vp[...], 0.0)
    nxt = jnp.where(has_next, cgn[...] * vn[...], 0.0)
    uext = jnp.concatenate([prev, cgm[...] * vm[...], nxt], axis=0)
    cbw = cbw_ref[...]
    conv = None
    for k in range(3):
        term = cbw[k:k + 1, :] * uext[HALO - 1 + k:HALO - 1 + k + tm, :]
        conv = term if conv is None else conv + term
    yb = (bg_ref[...] * conv).astype(_BF16)
    y = _dot(ya, wo_ref[0:D_LRU, :]) + _dot(yb, wo_ref[D_LRU:D_LRU + D_SC, :])
    o_ref[...] = _ffn_tail(x_ref[...], y, mod_ref, gffn_ref, w1_ref, w2_ref, None)


def _rc_out(x2d, hf, hb, p, conv_b_w, w_out, mod, gffn, w1, w2, tm, seq, row_of_tile):
    n, d = x2d.shape
    nt = seq // tm
    tb = tm // HALO
    last = n // HALO - 1

    def col(c):
        return pl.BlockSpec((tm, D_SC), lambda i: (i, c))

    def prev(c):
        return pl.BlockSpec((HALO, D_SC), lambda i: (jnp.maximum(i * tb - 1, 0), c))

    def nxt(c):
        return pl.BlockSpec((HALO, D_SC), lambda i: (jnp.minimum((i + 1) * tb, last), c))

    kernel = functools.partial(_rc_out_kernel, nt=nt, tm=tm)
    return pl.pallas_call(
        kernel,
        grid=(n // tm,),
        in_specs=[
            pl.BlockSpec((tm, d), lambda i: (i, 0)),
            pl.BlockSpec((tm, D_LRU), lambda i: (i, 0)),
            pl.BlockSpec((tm, D_LRU), lambda i: (i, 0)),
            col(1), col(2), col(3), prev(3), nxt(3), col(4), prev(4), nxt(4),
            _const_spec((3, D_SC)),
            _const_spec(w_out.shape),
            pl.BlockSpec((1, N_MOD, d), lambda i: (row_of_tile(i), 0, 0)),
            _const_spec((1, d)),
            _const_spec(w1.shape),
            _const_spec(w2.shape),
        ],
        out_specs=pl.BlockSpec((tm, d), lambda i: (i, 0)),
        out_shape=jax.ShapeDtypeStruct((n, d), _F32),
        compiler_params=_params(("parallel",)),
        name="rc_out_ffn",
    )(x2d, hf, hb, p, p, p, p, p, p, p, p, conv_b_w, w_out, mod, gffn, w1, w2)


def _attn_out_kernel(x_ref, o_in_ref, wo_ref, mod_ref, gffn_ref, w1_ref, w2_ref, gfin_ref, o_ref):
    y = _dot(o_in_ref[...], wo_ref[...])
    o_ref[...] = _ffn_tail(x_ref[...], y, mod_ref, gffn_ref, w1_ref, w2_ref, gfin_ref)


def _attn_out(x2d, o, w_out, mod, gffn, w1, w2, gfin, tm, row_of_tile):
    n, d = x2d.shape
    return pl.pallas_call(
        _attn_out_kernel,
        grid=(n // tm,),
        in_specs=[
            pl.BlockSpec((tm, d), lambda i: (i, 0)),
            pl.BlockSpec((tm, d), lambda i: (i, 0)),
            _const_spec(w_out.shape),
            pl.BlockSpec((1, N_MOD, d), lambda i: (row_of_tile(i), 0, 0)),
            _const_spec((1, d)),
            _const_spec(w1.shape),
            _const_spec(w2.shape),
            _const_spec((1, d)),
        ],
        out_specs=pl.BlockSpec((tm, d), lambda i: (i, 0)),
        out_shape=jax.ShapeDtypeStruct((n, d), _F32),
        compiler_params=_params(("parallel",)),
        name="attn_out_ffn",
    )(x2d, o, w_out, mod, gffn, w1, w2, gfin)


def _rope(x, cos, sin_signed, lo_half):
    partner = jnp.where(lo_half, pltpu.roll(x, V7X_LANES - ROPE_FREQS, 1), pltpu.roll(x, ROPE_FREQS, 1))
    return x * cos + partner * sin_signed


def _qkv_kernel(x_ref, mod_ref, g_ref, w_ref, cos_ref, sin_ref, q_ref, k_ref, v_ref):
    h = _rms_mod(x_ref[...], g_ref[...], mod_ref[0, 0:1, :], mod_ref[0, 1:2, :]).astype(_BF16)
    cos = cos_ref[...]
    sin = sin_ref[...]
    lane = lax.broadcasted_iota(jnp.int32, cos.shape, 1)
    lo_half = (lane % (2 * ROPE_FREQS)) < ROPE_FREQS
    dq = N_Q_HEADS * HEAD_DIM
    dk = N_KV_HEADS * V7X_LANES
    scale = HEAD_DIM ** -0.5 * LOG2E
    q = _dot(h, w_ref[:, 0:dq])
    for c in range(dq // V7X_LANES):
        cols = slice(c * V7X_LANES, (c + 1) * V7X_LANES)
        q_ref[:, cols] = (_rope(q[:, cols], cos, sin, lo_half) * scale).astype(_BF16)
    k = _dot(h, w_ref[:, dq:dq + dk])
    for c in range(dk // V7X_LANES):
        cols = slice(c * V7X_LANES, (c + 1) * V7X_LANES)
        k_ref[:, cols] = _rope(k[:, cols], cos, sin, lo_half).astype(_BF16)
    v_ref[...] = _dot(h, w_ref[:, dq + dk:dq + 2 * dk]).astype(_BF16)


def _qkv(x2d, mod, gain, w, cos_t, sin_t, tm, seq, row_of_tile):
    n, d = x2d.shape
    nt = seq // tm
    dq = N_Q_HEADS * HEAD_DIM
    dk = N_KV_HEADS * V7X_LANES
    return pl.pallas_call(
        _qkv_kernel,
        grid=(n // tm,),
        in_specs=[
            pl.BlockSpec((tm, d), lambda i: (i, 0)),
            pl.BlockSpec((1, N_MOD, d), lambda i: (row_of_tile(i), 0, 0)),
            _const_spec((1, d)),
            _const_spec(w.shape),
            pl.BlockSpec((tm, V7X_LANES), lambda i: (i % nt, 0)),
            pl.BlockSpec((tm, V7X_LANES), lambda i: (i % nt, 0)),
        ],
        out_specs=[
            pl.BlockSpec((tm, dq), lambda i: (i, 0)),
            pl.BlockSpec((tm, dk), lambda i: (i, 0)),
            pl.BlockSpec((tm, dk), lambda i: (i, 0)),
        ],
        out_shape=[
            jax.ShapeDtypeStruct((n, dq), _BF16),
            jax.ShapeDtypeStruct((n, dk), _BF16),
            jax.ShapeDtypeStruct((n, dk), _BF16),
        ],
        compiler_params=_params(("parallel",)),
        name="qkv_rope",
    )(x2d, mod, gain, w, cos_t, sin_t)


def _kv_kernel(x_ref, mod_ref, g_ref, w_ref, k_ref, v_ref):
    h = _rms_mod(x_ref[...], g_ref[...], mod_ref[0, 0:1, :], mod_ref[0, 1:2, :]).astype(_BF16)
    dk = N_KV_HEADS * V7X_LANES
    k_ref[...] = _dot(h, w_ref[:, 0:dk]).astype(_BF16)
    v_ref[...] = _dot(h, w_ref[:, dk:2 * dk]).astype(_BF16)


def _ctx_kv(x2d, mod, gain, w):
    n, d = x2d.shape
    dk = N_KV_HEADS * V7X_LANES
    return pl.pallas_call(
        _kv_kernel,
        grid=(1,),
        in_specs=[
            pl.BlockSpec((n, d), lambda i: (0, 0)),
            pl.BlockSpec((1, N_MOD, d), lambda i: (0, 0, 0)),
            _const_spec((1, d)),
            _const_spec(w.shape),
        ],
        out_specs=[pl.BlockSpec((n, dk), lambda i: (0, 0)), pl.BlockSpec((n, dk), lambda i: (0, 0))],
        out_shape=[jax.ShapeDtypeStruct((n, dk), _BF16), jax.ShapeDtypeStruct((n, dk), _BF16)],
        compiler_params=_params(("arbitrary",)),
        name="ctx_kv",
    )(x2d, mod, gain, w)


def _attn_kernel(sink_ref, q_ref, kp_ref, kc_ref, kn_ref, vp_ref, vc_ref, vn_ref, kx_ref, vx_ref, o_ref,
                 *, tq, nq):
    n = pl.program_id(1)
    blk = WINDOW
    nsub = tq // blk
    kext = jnp.concatenate([kp_ref[...], kc_ref[...], kn_ref[...]], axis=0)
    vext = jnp.concatenate([vp_ref[...], vc_ref[...], vn_ref[...]], axis=0)
    qi = lax.broadcasted_iota(jnp.int32, (blk, blk), 0)
    kj = lax.broadcasted_iota(jnp.int32, (blk, blk), 1)
    lane = lax.broadcasted_iota(jnp.int32, (blk, V7X_LANES), 1)
    lo = lane < HEAD_DIM
    ones_win = jnp.ones((3 * blk, V7X_LANES), _BF16)
    ones_ctx = jnp.ones((kx_ref.shape[1], V7X_LANES), _BF16)
    for j in range(nsub):
        keep_prev = kj >= qi
        keep_next = kj <= qi
        if j == 0:
            keep_prev = keep_prev & (n > 0)
        if j == nsub - 1:
            keep_next = keep_next & (n < nq - 1)
        rows = slice(j * blk, (j + 1) * blk)
        for hkv in range(N_KV_HEADS):
            kcols = slice(hkv * V7X_LANES, (hkv + 1) * V7X_LANES)
            kwin = kext[j * blk:(j + 3) * blk, kcols]
            vwin = jnp.concatenate([vext[j * blk:(j + 3) * blk, kcols], ones_win], axis=1)
            kx = kx_ref[0, :, kcols]
            vx = jnp.concatenate([vx_ref[0, :, kcols], ones_ctx], axis=1)
            parts = []
            for c in range(2):
                qc = q_ref[rows, (2 * hkv + c) * V7X_LANES:(2 * hkv + c + 1) * V7X_LANES]
                parts.append(jnp.where(lo, qc, jnp.zeros_like(qc)))
                parts.append(jnp.where(lo, jnp.zeros_like(qc), qc))
            lhs = jnp.concatenate(parts, axis=0)
            s_win = _dot_t(lhs, kwin)
            s_ctx = _dot_t(lhs, kx)
            p_win, p_ctx, esink = [], [], []
            for g in range(GQA_GROUP):
                sink = sink_ref[hkv * GQA_GROUP + g] * LOG2E
                grow = slice(g * blk, (g + 1) * blk)
                s0 = jnp.where(keep_prev, s_win[grow, 0:blk], NEG)
                s1 = s_win[grow, blk:2 * blk]
                s2 = jnp.where(keep_next, s_win[grow, 2 * blk:3 * blk], NEG)
                sc = [s_ctx[grow, i * V7X_LANES:(i + 1) * V7X_LANES] for i in range(s_ctx.shape[1] // V7X_LANES)]
                mm = jnp.maximum(jnp.maximum(s0, s1), s2)
                for t in sc:
                    mm = jnp.maximum(mm, t)
                m = jnp.maximum(jnp.max(mm, axis=-1, keepdims=True), sink)
                p_win.append(jnp.concatenate([jnp.exp2(t - m).astype(_BF16) for t in (s0, s1, s2)], axis=1))
                p_ctx.append(jnp.concatenate([jnp.exp2(t - m).astype(_BF16) for t in sc], axis=1))
                esink.append(jnp.exp2(sink - m))
            o = _dot(jnp.concatenate(p_win, axis=0), vwin) + _dot(jnp.concatenate(p_ctx, axis=0), vx)
            og = []
            for g in range(GQA_GROUP):
                grow = slice(g * blk, (g + 1) * blk)
                den = o[grow, V7X_LANES:2 * V7X_LANES] + esink[g]
                og.append(o[grow, 0:V7X_LANES] * (1.0 / den))
            for c in range(2):
                ocol = jnp.where(lo, og[2 * c], og[2 * c + 1])
                o_ref[rows, (2 * hkv + c) * V7X_LANES:(2 * hkv + c + 1) * V7X_LANES] = ocol.astype(_BF16)


def _attention(sink, q, kd, vd, kx, vx, batch, seq, tq):
    n, dq = q.shape
    dk = kd.shape[1]
    ctx_len = kx.shape[1]
    nq = seq // tq
    hb = tq // WINDOW
    sb = seq // WINDOW
    last = n // WINDOW - 1

    def cur():
        return pl.BlockSpec((tq, dk), lambda b, i: (b * nq + i, 0))

    def prev():
        return pl.BlockSpec((WINDOW, dk), lambda b, i: (jnp.maximum(b * sb + i * hb - 1, 0), 0))

    def nxt():
        return pl.BlockSpec((WINDOW, dk), lambda b, i: (jnp.minimum(b * sb + (i + 1) * hb, last), 0))

    kernel = functools.partial(_attn_kernel, tq=tq, nq=nq)
    return pl.pallas_call(
        kernel,
        grid=(batch, nq),
        in_specs=[
            pl.BlockSpec(memory_space=pltpu.SMEM),
            pl.BlockSpec((tq, dq), lambda b, i: (b * nq + i, 0)),
            prev(), cur(), nxt(), prev(), cur(), nxt(),
            pl.BlockSpec((1, ctx_len, dk), lambda b, i: (b, 0, 0)),
            pl.BlockSpec((1, ctx_len, dk), lambda b, i: (b, 0, 0)),
        ],
        out_specs=pl.BlockSpec((tq, dq), lambda b, i: (b * nq + i, 0)),
        out_shape=jax.ShapeDtypeStruct((n, dq), _BF16),
        compiler_params=_params(("parallel", "parallel")),
        name="band_attn",
    )(sink, q, kd, kd, kd, vd, vd, vd, kx, vx)


def _gate_weights(r_w, r_b, i_w, i_b):
    heads_per_half = V7X_MXU_DIM // LRU_HEAD_DIM
    eye = jnp.eye(heads_per_half, dtype=_F32)

    def halves(w):
        w = w.reshape(2, D_LRU // V7X_MXU_DIM, heads_per_half, LRU_HEAD_DIM, LRU_HEAD_DIM)
        bd = jnp.einsum('dxhij,hk->dxhikj', w, eye)
        return bd.reshape(2, D_LRU // V7X_MXU_DIM, V7X_MXU_DIM, V7X_MXU_DIM)

    w = (0.5 * jnp.concatenate([halves(r_w), halves(i_w)], axis=-1)).astype(_BF16)
    rb = r_b.reshape(2, D_LRU // V7X_MXU_DIM, 1, V7X_MXU_DIM)
    ib = i_b.reshape(2, D_LRU // V7X_MXU_DIM, 1, V7X_MXU_DIM)
    return w, 0.5 * jnp.concatenate([rb, ib], axis=-1)


def _dup_heads(w):
    d = w.shape[0]
    w = w.reshape(d, N_KV_HEADS, 1, HEAD_DIM)
    return jnp.broadcast_to(w, (d, N_KV_HEADS, 2, HEAD_DIM)).reshape(d, N_KV_HEADS * V7X_LANES)


def _rope_tables(seq):
    rows = seq // GRID_W
    row = jnp.repeat(jnp.arange(rows), GRID_W).astype(_F32)
    col = jnp.tile(jnp.arange(GRID_W), rows).astype(_F32)
    inv_freq = ROPE_BASE ** (-jnp.arange(ROPE_FREQS, dtype=_F32) / ROPE_FREQS)
    ang = jnp.stack([row[:, None] * inv_freq, col[:, None] * inv_freq], axis=1)
    cos, sin = jnp.cos(ang), jnp.sin(ang)
    cos_h = jnp.broadcast_to(cos[:, :, None, :], (seq, 2, 2, ROPE_FREQS)).reshape(seq, HEAD_DIM)
    sin_h = jnp.stack([-sin, sin], axis=2).reshape(seq, HEAD_DIM)
    return jnp.tile(cos_h, (1, 2)), jnp.tile(sin_h, (1, 2))


def kernel(x, c, ctx, c_ctx, ada_w, ada_b, norm_mix_g, norm_ffn_g, norm_final_g, ffn_w_in, ffn_w_out,
           rc_w_in, rc_conv_a_w, rc_conv_a_b, rc_gate_r_w, rc_gate_r_b, rc_gate_i_w, rc_gate_i_b,
           rc_lambda, rc_conv_b_w, rc_w_out, at_w_qkv, at_sink, at_w_out):
    batch, seq, d = x.shape
    ctx_len = ctx.shape[1]
    tm = 512
    tq = 512
    tc = 512
    assert seq % tm == 0 and seq % tq == 0 and seq % tc == 0 and ctx_len % V7X_LANES == 0

    xl = x.reshape(batch * seq, d)
    xc = ctx.reshape(batch * ctx_len, d)

    cond = jnp.zeros((8, d), _F32).at[0].set(c_ctx).at[1:1 + batch].set(c)
    mod = _modulation(cond, ada_w, ada_b).reshape(ada_w.shape[0], 8, N_MOD, d)

    lat_row = lambda i: 1 + i // (seq // tm)
    ctx_row = lambda i: 0

    w1 = [ffn_w_in[i].astype(_BF16) for i in range(ffn_w_in.shape[0])]
    w2 = [ffn_w_out[i].astype(_BF16) for i in range(ffn_w_out.shape[0])]

    w_in = rc_w_in[0].astype(_BF16)
    w_gate, b_gate = _gate_weights(rc_gate_r_w[0], rc_gate_r_b[0], rc_gate_i_w[0], rc_gate_i_b[0])
    lam = rc_lambda[0].reshape(2, 1, D_LRU)
    conv_a_b = rc_conv_a_b[0].reshape(1, D_LRU)
    w_out0 = rc_w_out[0].astype(_BF16)
    gmix0 = norm_mix_g[0].reshape(1, d)
    gffn0 = norm_ffn_g[0].reshape(1, d)

    p_c = _project(xc, mod[0], gmix0, w_in, batch * ctx_len, ctx_row)
    p_l = _project(xl, mod[0], gmix0, w_in, tm, lat_row)

    h0 = jnp.zeros((batch, 2, V7X_SUBLANES, D_LRU), _F32)
    hf_c, hb_c, h_ctx = _scan(p_c, h0, rc_conv_a_w[0], conv_a_b, w_gate, b_gate, lam, batch, ctx_len, ctx_len)
    hf_l, hb_l, _ = _scan(p_l, h_ctx, rc_conv_a_w[0], conv_a_b, w_gate, b_gate, lam, batch, seq, tc)

    xc = _rc_out(xc, hf_c, hb_c, p_c, rc_conv_b_w[0], w_out0, mod[0], gffn0, w1[0], w2[0],
                 ctx_len, ctx_len, ctx_row)
    xl = _rc_out(xl, hf_l, hb_l, p_l, rc_conv_b_w[0], w_out0, mod[0], gffn0, w1[0], w2[0],
                 tm, seq, lat_row)

    dq = N_Q_HEADS * HEAD_DIM
    dkv = N_KV_HEADS * HEAD_DIM
    wq, wk, wv = at_w_qkv[0][:, :dq], at_w_qkv[0][:, dq:dq + dkv], at_w_qkv[0][:, dq + dkv:]
    wk2, wv2 = _dup_heads(wk), _dup_heads(wv)
    w_qkv = jnp.concatenate([wq, wk2, wv2], axis=1).astype(_BF16)
    w_kv = jnp.concatenate([wk2, wv2], axis=1).astype(_BF16)
    gmix1 = norm_mix_g[1].reshape(1, d)
    gffn1 = norm_ffn_g[1].reshape(1, d)
    cos_t, sin_t = _rope_tables(seq)

    q, kd, vd = _qkv(xl, mod[1], gmix1, w_qkv, cos_t, sin_t, tm, seq, lat_row)
    kx, vx = _ctx_kv(xc, mod[1], gmix1, w_kv)
    kx = kx.reshape(batch, ctx_len, -1)
    vx = vx.reshape(batch, ctx_len, -1)
    o = _attention(at_sink[0], q, kd, vd, kx, vx, batch, seq, tq)

    out = _attn_out(xl, o, at_w_out[0].astype(_BF16), mod[1], gffn1, w1[1], w2[1],
                    norm_final_g.reshape(1, d), tm, lat_row)
    return out.reshape(batch, seq, d)
```

```python
import functools

import jax
import jax.numpy as jnp
from jax import lax
from jax.experimental import pallas as pl
from jax.experimental.pallas import tpu as pltpu

D_MODEL = 1024
N_MOD = 6
EPS = 1e-6
NEG = -1e30
D_LRU = 512
D_SC = 512
LRU_HEADS = 8
LRU_HEAD_DIM = 64
LRU_C = 8.0
RC_IN_WIDTH = 2 * D_LRU + 3 * D_SC
HEAD_DIM = 64
N_Q_HEADS = 16
N_KV_HEADS = 4
GQA_GROUP = 4
WINDOW = 128
GRID_W = 64
ROPE_BASE = 10000.0
ROPE_FREQS = 16
D_FF = 2816
LOG2E = 1.4426950408889634

V7X_LANES = 128
V7X_SUBLANES = 8
V7X_MXU_DIM = 256
V7X_VMEM_LIMIT = 56 * 1024 * 1024

HALO = V7X_SUBLANES
N_SLAB = D_LRU // V7X_LANES
SCAN_SHIFTS = (1, 2, 4)
FFN_CHUNKS = ((0, 1024), (1024, 1024), (2048, 768))

_BF16 = jnp.bfloat16
_F32 = jnp.float32


def _dot(a, b):
    return jnp.dot(a, b, preferred_element_type=_F32)


def _dot_t(a, b):
    return lax.dot_general(a, b, (((1,), (1,)), ((), ())), preferred_element_type=_F32)


def _sigmoid(x):
    return 0.5 * jnp.tanh(0.5 * x) + 0.5


def _gelu_tanh(x):
    return 0.5 * x * (1.0 + jnp.tanh(0.7978845608028654 * (x + 0.044715 * (x * x * x))))


def _rms_mod(x, gain, shift, scale):
    ms = jnp.mean(x * x, axis=-1, keepdims=True)
    return (x * lax.rsqrt(ms + EPS) * gain) * (1.0 + scale) + shift


def _const_spec(shape, index=None):
    idx = (0,) * len(shape) if index is None else index
    return pl.BlockSpec(shape, lambda *_: idx, pipeline_mode=pl.Buffered(1))


def _params(sem):
    return pltpu.CompilerParams(dimension_semantics=sem, vmem_limit_bytes=V7X_VMEM_LIMIT)


def _mod_kernel(c_ref, w_ref, b_ref, o_ref):
    c = c_ref[...]
    s = (c * _sigmoid(c)).astype(_BF16)
    o_ref[0] = _dot(s, w_ref[0].astype(_BF16)) + b_ref[0]


def _modulation(cond, ada_w, ada_b):
    depth, d, n = ada_w.shape
    bn = 1536
    return pl.pallas_call(
        _mod_kernel,
        grid=(depth, n // bn),
        in_specs=[
            pl.BlockSpec((8, d), lambda l, j: (0, 0)),
            pl.BlockSpec((1, d, bn), lambda l, j: (l, 0, j)),
            pl.BlockSpec((1, 1, bn), lambda l, j: (l, 0, j)),
        ],
        out_specs=pl.BlockSpec((1, 8, bn), lambda l, j: (l, 0, j)),
        out_shape=jax.ShapeDtypeStruct((depth, 8, n), _F32),
        compiler_params=_params(("arbitrary", "arbitrary")),
        name="adaln_mod",
    )(cond, ada_w, ada_b.reshape(depth, 1, n))


def _proj_kernel(x_ref, mod_ref, g_ref, w_ref, xa_ref, rest_ref):
    h = _rms_mod(x_ref[...], g_ref[...], mod_ref[0, 0:1, :], mod_ref[0, 1:2, :]).astype(_BF16)
    xa = _dot(h, w_ref[:, 0:D_LRU])
    for c in range(N_SLAB):
        xa_ref[c] = xa[:, c * V7X_LANES:(c + 1) * V7X_LANES]
    rest_ref[...] = _dot(h, w_ref[:, D_LRU:])


def _project(x2d, mod, gain, w, tm, row_of_tile):
    n, d = x2d.shape
    nrest = w.shape[1] - D_LRU
    return pl.pallas_call(
        _proj_kernel,
        grid=(n // tm,),
        in_specs=[
            pl.BlockSpec((tm, d), lambda i: (i, 0)),
            pl.BlockSpec((1, N_MOD, d), lambda i: (row_of_tile(i), 0, 0)),
            _const_spec((1, d)),
            _const_spec(w.shape),
        ],
        out_specs=[
            pl.BlockSpec((N_SLAB, tm, V7X_LANES), lambda i: (0, i, 0)),
            pl.BlockSpec((tm, nrest), lambda i: (i, 0)),
        ],
        out_shape=[
            jax.ShapeDtypeStruct((N_SLAB, n, V7X_LANES), _F32),
            jax.ShapeDtypeStruct((n, nrest), _F32),
        ],
        compiler_params=_params(("parallel",)),
        name="rc_in_proj",
    )(x2d, mod, gain, w)


def _scan_kernel(xm_f, xp_f, xn_f, xm_b, xp_b, xn_b, cw_ref, cb_ref, wg_ref, bg_ref, lam_ref, h0_ref,
                 hf_ref, hb_ref, hl_ref, xs_scr, carry_scr, *level_scr, nc, tc):
    j = pl.program_id(1)
    lo = HALO
    ngroups = tc // V7X_SUBLANES
    nlev = len(SCAN_SHIFTS)
    a_lv = list(level_scr[:nlev])
    b_lv = list(level_scr[nlev:])
    pad_rows = (slice(0, HALO), slice(lo + tc, lo + tc + HALO))

    @pl.when(j == 0)
    def _():
        carry_scr[...] = h0_ref[0]
        for d in range(2):
            for ref in a_lv:
                ref[d, :, pad_rows[d], :] = jnp.ones((N_SLAB, HALO, V7X_LANES), _F32)
            for ref in b_lv:
                ref[d, :, pad_rows[d], :] = jnp.zeros((N_SLAB, HALO, V7X_LANES), _F32)

    cw = cw_ref[...]
    cb = cb_ref[...]

    def coeffs(d, xm, xp, xn, has_prev, has_next):
        xs_scr[d, :, 0:HALO, :] = jnp.where(has_prev, xp[...], 0.0)
        xs_scr[d, :, lo:lo + tc, :] = xm[...]
        xs_scr[d, :, lo + tc:lo + tc + HALO, :] = jnp.where(has_next, xn[...], 0.0)
        xcs = []
        for c in range(N_SLAB):
            lanes = slice(c * V7X_LANES, (c + 1) * V7X_LANES)
            acc = cb[:, lanes]
            for k in range(4):
                acc = acc + cw[k:k + 1, lanes] * xs_scr[d, c, lo - 2 + k:lo - 2 + k + tc, :]
            xcs.append(acc)
        lam = lam_ref[d]
        nlam = -lam
        softplus = jnp.maximum(nlam, 0.0) + jnp.log(1.0 + jnp.exp(-jnp.abs(nlam)))
        hrate = (-0.5 * LRU_C * LOG2E) * softplus
        half = V7X_MXU_DIM
        per_half = half // V7X_LANES
        for hh in range(D_LRU // half):
            xch = jnp.concatenate(xcs[hh * per_half:(hh + 1) * per_half], axis=1)
            z = _dot(xch.astype(_BF16), wg_ref[d, hh]) + bg_ref[d, hh]
            for cc in range(per_half):
                c = hh * per_half + cc
                lanes = slice(c * V7X_LANES, (c + 1) * V7X_LANES)
                tr = jnp.tanh(z[:, cc * V7X_LANES:(cc + 1) * V7X_LANES])
                ig = 0.5 * jnp.tanh(z[:, half + cc * V7X_LANES:half + (cc + 1) * V7X_LANES]) + 0.5
                a = jnp.exp2(tr * hrate[:, lanes] + hrate[:, lanes])
                one_m_a2 = jnp.maximum(1.0 - a * a, 1e-12)
                mult = one_m_a2 * lax.rsqrt(one_m_a2)
                a_lv[0][d, c, lo:lo + tc, :] = a
                b_lv[0][d, c, lo:lo + tc, :] = mult * ig * xcs[c]

    coeffs(0, xm_f, xp_f, xn_f, j > 0, j < nc - 1)
    coeffs(1, xm_b, xp_b, xn_b, j < nc - 1, j > 0)

    def step(g, carry):
        out = []
        for d, h_ref in ((0, hf_ref), (1, hb_ref)):
            grp = g if d == 0 else ngroups - 1 - g
            row0 = pl.multiple_of(lo + grp * V7X_SUBLANES, V7X_SUBLANES)
            cur = pl.ds(row0, V7X_SUBLANES)
            for c in range(N_SLAB):
                a = a_lv[0][d, c, cur, :]
                b = b_lv[0][d, c, cur, :]
                for lvl, shift in enumerate(SCAN_SHIFTS):
                    sh = pl.ds(row0 + (shift if d else -shift), V7X_SUBLANES)
                    b = a * b_lv[lvl][d, c, sh, :] + b
                    a = a * a_lv[lvl][d, c, sh, :]
                    if lvl + 1 < nlev:
                        a_lv[lvl + 1][d, c, cur, :] = a
                        b_lv[lvl + 1][d, c, cur, :] = b
                h = a * carry[d * N_SLAB + c] + b
                out_rows = pl.ds(pl.multiple_of(grp * V7X_SUBLANES, V7X_SUBLANES), V7X_SUBLANES)
                h_ref[out_rows, c * V7X_LANES:(c + 1) * V7X_LANES] = h
                out.append(h)
        return tuple(out)

    init = tuple(carry_scr[d, :, c * V7X_LANES:(c + 1) * V7X_LANES]
                 for d in range(2) for c in range(N_SLAB))
    carry = lax.fori_loop(0, ngroups, step, init, unroll=4)

    edge_row = (V7X_SUBLANES - 1, 0)
    data_edge = (slice(tc, tc + HALO), slice(lo, lo + HALO))
    for d in range(2):
        for c in range(N_SLAB):
            lanes = slice(c * V7X_LANES, (c + 1) * V7X_LANES)
            h = carry[d * N_SLAB + c]
            carry_scr[d, :, lanes] = h
            hl_ref[0, d, :, lanes] = jnp.broadcast_to(h[edge_row[d]:edge_row[d] + 1, :], h.shape)
        for ref in a_lv + b_lv:
            ref[d, :, pad_rows[d], :] = ref[d, :, data_edge[d], :]


def _scan(xa, h0, conv_w, conv_b, w_gate, b_gate, lam, batch, seq, tc):
    n = xa.shape[1]
    nc = seq // tc
    tb = tc // HALO
    sb = seq // HALO
    last = n // HALO - 1

    def fwd(b, j):
        return j

    def bwd(b, j):
        return nc - 1 - j

    def main(cf):
        return pl.BlockSpec((N_SLAB, tc, V7X_LANES), lambda b, j: (0, b * nc + cf(b, j), 0))

    def prev(cf):
        return pl.BlockSpec((N_SLAB, HALO, V7X_LANES),
                            lambda b, j: (0, jnp.maximum(b * sb + cf(b, j) * tb - 1, 0), 0))

    def nxt(cf):
        return pl.BlockSpec((N_SLAB, HALO, V7X_LANES),
                            lambda b, j: (0, jnp.minimum(b * sb + (cf(b, j) + 1) * tb, last), 0))

    kernel = functools.partial(_scan_kernel, nc=nc, tc=tc)
    nlev = len(SCAN_SHIFTS)
    return pl.pallas_call(
        kernel,
        grid=(batch, nc),
        in_specs=[
            main(fwd), prev(fwd), nxt(fwd), main(bwd), prev(bwd), nxt(bwd),
            _const_spec((4, D_LRU)),
            _const_spec((1, D_LRU)),
            _const_spec(w_gate.shape),
            _const_spec(b_gate.shape),
            _const_spec((2, 1, D_LRU)),
            pl.BlockSpec((1, 2, V7X_SUBLANES, D_LRU), lambda b, j: (b, 0, 0, 0)),
        ],
        out_specs=[
            pl.BlockSpec((tc, D_LRU), lambda b, j: (b * nc + j, 0)),
            pl.BlockSpec((tc, D_LRU), lambda b, j: (b * nc + nc - 1 - j, 0)),
            pl.BlockSpec((1, 2, V7X_SUBLANES, D_LRU), lambda b, j: (b, 0, 0, 0)),
        ],
        out_shape=[
            jax.ShapeDtypeStruct((n, D_LRU), _F32),
            jax.ShapeDtypeStruct((n, D_LRU), _F32),
            jax.ShapeDtypeStruct((batch, 2, V7X_SUBLANES, D_LRU), _F32),
        ],
        scratch_shapes=[
            pltpu.VMEM((2, N_SLAB, tc + 2 * HALO, V7X_LANES), _F32),
            pltpu.VMEM((2, V7X_SUBLANES, D_LRU), _F32),
        ] + [pltpu.VMEM((2, N_SLAB, tc + 2 * HALO, V7X_LANES), _F32) for _ in range(2 * nlev)],
        compiler_params=_params(("arbitrary", "arbitrary")),
        name="rglru_scan",
    )(xa, xa, xa, xa, xa, xa, conv_w, conv_b, w_gate, b_gate, lam, h0)


def _ffn_tail(x, y, mod_ref, gffn_ref, w1_ref, w2_ref, gfin_ref):
    x1 = x + mod_ref[0, 2:3, :] * y
    h = _rms_mod(x1, gffn_ref[...], mod_ref[0, 3:4, :], mod_ref[0, 4:5, :]).astype(_BF16)
    acc = None
    for c0, cw in FFN_CHUNKS:
        g = _dot(h, w1_ref[0, :, c0:c0 + cw])
        u = _dot(h, w1_ref[0, :, D_FF + c0:D_FF + c0 + cw])
        act = (g * _sigmoid(g) * u).astype(_BF16)
        part = _dot(act, w2_ref[0, c0:c0 + cw, :])
        acc = part if acc is None else acc + part
    x2 = x1 + mod_ref[0, 5:6, :] * acc
    if gfin_ref is not None:
        ms = jnp.mean(x2 * x2, axis=-1, keepdims=True)
        x2 = x2 * lax.rsqrt(ms + EPS) * gfin_ref[...]
    return x2


def _ffn_weight_specs(w1, w2, layer):
    return [_const_spec((1,) + w1.shape[1:], (layer, 0, 0)), _const_spec((1,) + w2.shape[1:], (layer, 0, 0))]


def _rc_out_kernel(x_ref, hf_ref, hb_ref, ga_ref, bg_ref, cgm, cgp, cgn, vm, vp, vn,
                   cbw_ref, wo_ref, mod_ref, gffn_ref, w1_ref, w2_ref, o_ref, *, nt, tm):
    i = pl.program_id(0)
    has_prev = (i % nt) > 0
    has_next = (i % nt) < nt - 1
    ya = ((hf_ref[...] + hb_ref[...]) * _gelu_tanh(ga_ref[...])).astype(_BF16)
    prev = jnp.where(has_prev, cgp[...] * vp[...], 0.0)
    nxt = jnp.where(has_next, cgn[...] * vn[...], 0.0)
    uext = jnp.concatenate([prev, cgm[...] * vm[...], nxt], axis=0)
    cbw = cbw_ref[...]
    conv = None
    for k in range(3):
        term = cbw[k:k + 1, :] * uext[HALO - 1 + k:HALO - 1 + k + tm, :]
        conv = term if conv is None else conv + term
    yb = (bg_ref[...] * conv).astype(_BF16)
    y = _dot(ya, wo_ref[0:D_LRU, :]) + _dot(yb, wo_ref[D_LRU:D_LRU + D_SC, :])
    o_ref[...] = _ffn_tail(x_ref[...], y, mod_ref, gffn_ref, w1_ref, w2_ref, None)


def _rc_out(x2d, hf, hb, p, conv_b_w, w_out, mod, gffn, w1, w2, layer, tm, seq, row_of_tile):
    n, d = x2d.shape
    nt = seq // tm
    tb = tm // HALO
    last = n // HALO - 1

    def col(c):
        return pl.BlockSpec((tm, D_SC), lambda i: (i, c))

    def prev(c):
        return pl.BlockSpec((HALO, D_SC), lambda i: (jnp.maximum(i * tb - 1, 0), c))

    def nxt(c):
        return pl.BlockSpec((HALO, D_SC), lambda i: (jnp.minimum((i + 1) * tb, last), c))

    kernel = functools.partial(_rc_out_kernel, nt=nt, tm=tm)
    return pl.pallas_call(
        kernel,
        grid=(n // tm,),
        in_specs=[
            pl.BlockSpec((tm, d), lambda i: (i, 0)),
            pl.BlockSpec((tm, D_LRU), lambda i: (i, 0)),
            pl.BlockSpec((tm, D_LRU), lambda i: (i, 0)),
            col(0), col(1), col(2), prev(2), nxt(2), col(3), prev(3), nxt(3),
            _const_spec((3, D_SC)),
            _const_spec(w_out.shape),
            pl.BlockSpec((1, N_MOD, d), lambda i: (row_of_tile(i), 0, 0)),
            _const_spec((1, d)),
        ] + _ffn_weight_specs(w1, w2, layer),
        out_specs=pl.BlockSpec((tm, d), lambda i: (i, 0)),
        out_shape=jax.ShapeDtypeStruct((n, d), _F32),
        compiler_params=_params(("parallel",)),
        name="rc_out_ffn",
    )(x2d, hf, hb, p, p, p, p, p, p, p, p, conv_b_w, w_out, mod, gffn, w1, w2)


def _attn_out_kernel(x_ref, o_in_ref, wo_ref, mod_ref, gffn_ref, gfin_ref, w1_ref, w2_ref, o_ref):
    y = _dot(o_in_ref[...], wo_ref[...])
    o_ref[...] = _ffn_tail(x_ref[...], y, mod_ref, gffn_ref, w1_ref, w2_ref, gfin_ref)


def _attn_out(x2d, o, w_out, mod, gffn, w1, w2, layer, gfin, tm, row_of_tile):
    n, d = x2d.shape
    return pl.pallas_call(
        _attn_out_kernel,
        grid=(n // tm,),
        in_specs=[
            pl.BlockSpec((tm, d), lambda i: (i, 0)),
            pl.BlockSpec((tm, d), lambda i: (i, 0)),
            _const_spec(w_out.shape),
            pl.BlockSpec((1, N_MOD, d), lambda i: (row_of_tile(i), 0, 0)),
            _const_spec((1, d)),
            _const_spec((1, d)),
        ] + _ffn_weight_specs(w1, w2, layer),
        out_specs=pl.BlockSpec((tm, d), lambda i: (i, 0)),
        out_shape=jax.ShapeDtypeStruct((n, d), _F32),
        compiler_params=_params(("parallel",)),
        name="attn_out_ffn",
    )(x2d, o, w_out, mod, gffn, gfin, w1, w2)


def _rope(x, cos, sin_signed, lo_half):
    partner = jnp.where(lo_half, pltpu.roll(x, V7X_LANES - ROPE_FREQS, 1), pltpu.roll(x, ROPE_FREQS, 1))
    return x * cos + partner * sin_signed


def _qkv_kernel(x_ref, mod_ref, g_ref, w_ref, cos_ref, sin_ref, q_ref, k_ref, v_ref):
    h = _rms_mod(x_ref[...], g_ref[...], mod_ref[0, 0:1, :], mod_ref[0, 1:2, :]).astype(_BF16)
    cos = cos_ref[...]
    sin = sin_ref[...]
    lane = lax.broadcasted_iota(jnp.int32, cos.shape, 1)
    lo_half = (lane % (2 * ROPE_FREQS)) < ROPE_FREQS
    dq = N_Q_HEADS * HEAD_DIM
    dk = N_KV_HEADS * V7X_LANES
    scale = HEAD_DIM ** -0.5 * LOG2E
    q = _dot(h, w_ref[:, 0:dq])
    for c in range(dq // V7X_LANES):
        cols = slice(c * V7X_LANES, (c + 1) * V7X_LANES)
        q_ref[:, cols] = (_rope(q[:, cols], cos, sin, lo_half) * scale).astype(_BF16)
    k = _dot(h, w_ref[:, dq:dq + dk])
    for c in range(dk // V7X_LANES):
        cols = slice(c * V7X_LANES, (c + 1) * V7X_LANES)
        k_ref[:, cols] = _rope(k[:, cols], cos, sin, lo_half).astype(_BF16)
    v_ref[...] = _dot(h, w_ref[:, dq + dk:dq + 2 * dk]).astype(_BF16)


def _qkv(x2d, mod, gain, w, cos_t, sin_t, tm, seq, row_of_tile):
    n, d = x2d.shape
    nt = seq // tm
    dq = N_Q_HEADS * HEAD_DIM
    dk = N_KV_HEADS * V7X_LANES
    return pl.pallas_call(
        _qkv_kernel,
        grid=(n // tm,),
        in_specs=[
            pl.BlockSpec((tm, d), lambda i: (i, 0)),
            pl.BlockSpec((1, N_MOD, d), lambda i: (row_of_tile(i), 0, 0)),
            _const_spec((1, d)),
            _const_spec(w.shape),
            pl.BlockSpec((tm, V7X_LANES), lambda i: (i % nt, 0)),
            pl.BlockSpec((tm, V7X_LANES), lambda i: (i % nt, 0)),
        ],
        out_specs=[
            pl.BlockSpec((tm, dq), lambda i: (i, 0)),
            pl.BlockSpec((tm, dk), lambda i: (i, 0)),
            pl.BlockSpec((tm, dk), lambda i: (i, 0)),
        ],
        out_shape=[
            jax.ShapeDtypeStruct((n, dq), _BF16),
            jax.ShapeDtypeStruct((n, dk), _BF16),
            jax.ShapeDtypeStruct((n, dk), _BF16),
        ],
        compiler_params=_params(("parallel",)),
        name="qkv_rope",
    )(x2d, mod, gain, w, cos_t, sin_t)


def _kv_kernel(x_ref, mod_ref, g_ref, w_ref, k_ref, v_ref):
    h = _rms_mod(x_ref[...], g_ref[...], mod_ref[0, 0:1, :], mod_ref[0, 1:2, :]).astype(_BF16)
    dk = N_KV_HEADS * V7X_LANES
    k_ref[...] = _dot(h, w_ref[:, 0:dk]).astype(_BF16)
    v_ref[...] = _dot(h, w_ref[:, dk:2 * dk]).astype(_BF16)


def _ctx_kv(x2d, mod, gain, w):
    n, d = x2d.shape
    dk = N_KV_HEADS * V7X_LANES
    return pl.pallas_call(
        _kv_kernel,
        grid=(1,),
        in_specs=[
            pl.BlockSpec((n, d), lambda i: (0, 0)),
            pl.BlockSpec((1, N_MOD, d), lambda i: (0, 0, 0)),
            _const_spec((1, d)),
            _const_spec(w.shape),
        ],
        out_specs=[pl.BlockSpec((n, dk), lambda i: (0, 0)), pl.BlockSpec((n, dk), lambda i: (0, 0))],
        out_shape=[jax.ShapeDtypeStruct((n, dk), _BF16), jax.ShapeDtypeStruct((n, dk), _BF16)],
        compiler_params=_params(("arbitrary",)),
        name="ctx_kv",
    )(x2d, mod, gain, w)


def _attn_kernel(sink_ref, q_ref, kp_ref, kc_ref, kn_ref, vp_ref, vc_ref, vn_ref, kx_ref, vx_ref, o_ref,
                 *, tq, nq):
    n = pl.program_id(1)
    blk = WINDOW
    nsub = tq // blk
    kext = jnp.concatenate([kp_ref[...], kc_ref[...], kn_ref[...]], axis=0)
    vext = jnp.concatenate([vp_ref[...], vc_ref[...], vn_ref[...]], axis=0)
    qi = lax.broadcasted_iota(jnp.int32, (blk, blk), 0)
    kj = lax.broadcasted_iota(jnp.int32, (blk, blk), 1)
    lane = lax.broadcasted_iota(jnp.int32, (blk, V7X_LANES), 1)
    lo = lane < HEAD_DIM
    ones_win = jnp.ones((3 * blk, V7X_LANES), _BF16)
    ones_ctx = jnp.ones((kx_ref.shape[1], V7X_LANES), _BF16)
    for j in range(nsub):
        keep_prev = kj >= qi
        keep_next = kj <= qi
        if j == 0:
            keep_prev = keep_prev & (n > 0)
        if j == nsub - 1:
            keep_next = keep_next & (n < nq - 1)
        rows = slice(j * blk, (j + 1) * blk)
        for hkv in range(N_KV_HEADS):
            kcols = slice(hkv * V7X_LANES, (hkv + 1) * V7X_LANES)
            kwin = kext[j * blk:(j + 3) * blk, kcols]
            vwin = jnp.concatenate([vext[j * blk:(j + 3) * blk, kcols], ones_win], axis=1)
            kx = kx_ref[0, :, kcols]
            vx = jnp.concatenate([vx_ref[0, :, kcols], ones_ctx], axis=1)
            parts = []
            for c in range(2):
                qc = q_ref[rows, (2 * hkv + c) * V7X_LANES:(2 * hkv + c + 1) * V7X_LANES]
                parts.append(jnp.where(lo, qc, jnp.zeros_like(qc)))
                parts.append(jnp.where(lo, jnp.zeros_like(qc), qc))
            lhs = jnp.concatenate(parts, axis=0)
            s_win = _dot_t(lhs, kwin)
            s_ctx = _dot_t(lhs, kx)
            p_win, p_ctx, esink = [], [], []
            for g in range(GQA_GROUP):
                sink = sink_ref[hkv * GQA_GROUP + g] * LOG2E
                grow = slice(g * blk, (g + 1) * blk)
                s0 = jnp.where(keep_prev, s_win[grow, 0:blk], NEG)
                s1 = s_win[grow, blk:2 * blk]
                s2 = jnp.where(keep_next, s_win[grow, 2 * blk:3 * blk], NEG)
                sc = [s_ctx[grow, i * V7X_LANES:(i + 1) * V7X_LANES] for i in range(s_ctx.shape[1] // V7X_LANES)]
                mm = jnp.maximum(jnp.maximum(s0, s1), s2)
                for t in sc:
                    mm = jnp.maximum(mm, t)
                m = jnp.maximum(jnp.max(mm, axis=-1, keepdims=True), sink)
                p_win.append(jnp.concatenate([jnp.exp2(t - m).astype(_BF16) for t in (s0, s1, s2)], axis=1))
                p_ctx.append(jnp.concatenate([jnp.exp2(t - m).astype(_BF16) for t in sc], axis=1))
                esink.append(jnp.exp2(sink - m))
            o = _dot(jnp.concatenate(p_win, axis=0), vwin) + _dot(jnp.concatenate(p_ctx, axis=0), vx)
            og = []
            for g in range(GQA_GROUP):
                grow = slice(g * blk, (g + 1) * blk)
                den = o[grow, V7X_LANES:2 * V7X_LANES] + esink[g]
                og.append(o[grow, 0:V7X_LANES] * (1.0 / den))
            for c in range(2):
                ocol = jnp.where(lo, og[2 * c], og[2 * c + 1])
                o_ref[rows, (2 * hkv + c) * V7X_LANES:(2 * hkv + c + 1) * V7X_LANES] = ocol.astype(_BF16)


def _attention(sink, q, kd, vd, kx, vx, batch, seq, tq):
    n, dq = q.shape
    dk = kd.shape[1]
    ctx_len = kx.shape[1]
    nq = seq // tq
    hb = tq // WINDOW
    sb = seq // WINDOW
    last = n // WINDOW - 1

    def cur():
        return pl.BlockSpec((tq, dk), lambda b, i: (b * nq + i, 0))

    def prev():
        return pl.BlockSpec((WINDOW, dk), lambda b, i: (jnp.maximum(b * sb + i * hb - 1, 0), 0))

    def nxt():
        return pl.BlockSpec((WINDOW, dk), lambda b, i: (jnp.minimum(b * sb + (i + 1) * hb, last), 0))

    kernel = functools.partial(_attn_kernel, tq=tq, nq=nq)
    return pl.pallas_call(
        kernel,
        grid=(batch, nq),
        in_specs=[
            pl.BlockSpec(memory_space=pltpu.SMEM),
            pl.BlockSpec((tq, dq), lambda b, i: (b * nq + i, 0)),
            prev(), cur(), nxt(), prev(), cur(), nxt(),
            pl.BlockSpec((1, ctx_len, dk), lambda b, i: (b, 0, 0)),
            pl.BlockSpec((1, ctx_len, dk), lambda b, i: (b, 0, 0)),
        ],
        out_specs=pl.BlockSpec((tq, dq), lambda b, i: (b * nq + i, 0)),
        out_shape=jax.ShapeDtypeStruct((n, dq), _BF16),
        compiler_params=_params(("parallel", "parallel")),
        name="band_attn",
    )(sink, q, kd, kd, kd, vd, vd, vd, kx, vx)


def _gate_weights(r_w, r_b, i_w, i_b):
    heads_per_half = V7X_MXU_DIM // LRU_HEAD_DIM
    eye = jnp.eye(heads_per_half, dtype=_F32)

    def halves(w):
        w = w.reshape(2, D_LRU // V7X_MXU_DIM, heads_per_half, LRU_HEAD_DIM, LRU_HEAD_DIM)
        bd = jnp.einsum('dxhij,hk->dxhikj', w, eye)
        return bd.reshape(2, D_LRU // V7X_MXU_DIM, V7X_MXU_DIM, V7X_MXU_DIM)

    w = (0.5 * jnp.concatenate([halves(r_w), halves(i_w)], axis=-1)).astype(_BF16)
    rb = r_b.reshape(2, D_LRU // V7X_MXU_DIM, 1, V7X_MXU_DIM)
    ib = i_b.reshape(2, D_LRU // V7X_MXU_DIM, 1, V7X_MXU_DIM)
    return w, 0.5 * jnp.concatenate([rb, ib], axis=-1)


def _dup_heads(w):
    d = w.shape[0]
    w = w.reshape(d, N_KV_HEADS, 1, HEAD_DIM)
    return jnp.broadcast_to(w, (d, N_KV_HEADS, 2, HEAD_DIM)).reshape(d, N_KV_HEADS * V7X_LANES)


def _rope_tables(seq):
    pos = lax.broadcasted_iota(jnp.int32, (seq, V7X_LANES), 0)
    lane = lax.broadcasted_iota(jnp.int32, (seq, V7X_LANES), 1)
    row = (pos // GRID_W).astype(_F32)
    col = (pos % GRID_W).astype(_F32)
    inv_freq = ROPE_BASE ** (-(lane % ROPE_FREQS).astype(_F32) / ROPE_FREQS)
    ang = jnp.where((lane % HEAD_DIM) < 2 * ROPE_FREQS, row, col) * inv_freq
    sign = jnp.where((lane % (2 * ROPE_FREQS)) < ROPE_FREQS, -1.0, 1.0)
    return jnp.cos(ang), jnp.sin(ang) * sign


def kernel(x, c, ctx, c_ctx, ada_w, ada_b, norm_mix_g, norm_ffn_g, norm_final_g, ffn_w_in, ffn_w_out,
           rc_w_in, rc_conv_a_w, rc_conv_a_b, rc_gate_r_w, rc_gate_r_b, rc_gate_i_w, rc_gate_i_b,
           rc_lambda, rc_conv_b_w, rc_w_out, at_w_qkv, at_sink, at_w_out):
    batch, seq, d = x.shape
    ctx_len = ctx.shape[1]
    tm = 512
    tq = 512
    tc = 512
    assert seq % tm == 0 and seq % tq == 0 and seq % tc == 0 and ctx_len % V7X_LANES == 0

    xl = x.reshape(batch * seq, d)
    xc = ctx.reshape(batch * ctx_len, d)

    cond = jnp.concatenate([c_ctx[None], c, jnp.zeros((8 - 1 - batch, d), _F32)], axis=0)
    mod = _modulation(cond, ada_w, ada_b).reshape(ada_w.shape[0], 8, N_MOD, d)

    def lat_row(tile):
        return lambda i: 1 + i // (seq // tile)

    ctx_row = lambda i: 0

    w1 = ffn_w_in.astype(_BF16)
    w2 = ffn_w_out.astype(_BF16)

    w_in = rc_w_in[0].astype(_BF16)
    w_gate, b_gate = _gate_weights(rc_gate_r_w[0], rc_gate_r_b[0], rc_gate_i_w[0], rc_gate_i_b[0])
    lam = rc_lambda[0].reshape(2, 1, D_LRU)
    conv_a_b = rc_conv_a_b[0].reshape(1, D_LRU)
    w_out0 = rc_w_out[0].astype(_BF16)
    gmix0 = norm_mix_g[0].reshape(1, d)
    gffn0 = norm_ffn_g[0].reshape(1, d)

    xa_c, p_c = _project(xc, mod[0], gmix0, w_in, batch * ctx_len, ctx_row)
    xa_l, p_l = _project(xl, mod[0], gmix0, w_in, tm, lat_row(tm))

    h0 = jnp.zeros((batch, 2, V7X_SUBLANES, D_LRU), _F32)
    hf_c, hb_c, h_ctx = _scan(xa_c, h0, rc_conv_a_w[0], conv_a_b, w_gate, b_gate, lam, batch, ctx_len, ctx_len)
    hf_l, hb_l, _ = _scan(xa_l, h_ctx, rc_conv_a_w[0], conv_a_b, w_gate, b_gate, lam, batch, seq, tc)

    xc = _rc_out(xc, hf_c, hb_c, p_c, rc_conv_b_w[0], w_out0, mod[0], gffn0, w1, w2, 0,
                 ctx_len, ctx_len, ctx_row)
    xl = _rc_out(xl, hf_l, hb_l, p_l, rc_conv_b_w[0], w_out0, mod[0], gffn0, w1, w2, 0,
                 tm, seq, lat_row(tm))

    dq = N_Q_HEADS * HEAD_DIM
    dkv = N_KV_HEADS * HEAD_DIM
    wq, wk, wv = at_w_qkv[0][:, :dq], at_w_qkv[0][:, dq:dq + dkv], at_w_qkv[0][:, dq + dkv:]
    wk2, wv2 = _dup_heads(wk), _dup_heads(wv)
    w_qkv = jnp.concatenate([wq, wk2, wv2], axis=1).astype(_BF16)
    w_kv = jnp.concatenate([wk2, wv2], axis=1).astype(_BF16)
    gmix1 = norm_mix_g[1].reshape(1, d)
    gffn1 = norm_ffn_g[1].reshape(1, d)
    cos_t, sin_t = _rope_tables(seq)

    q, kd, vd = _qkv(xl, mod[1], gmix1, w_qkv, cos_t, sin_t, tm, seq, lat_row(tm))
    kx, vx = _ctx_kv(xc, mod[1], gmix1, w_kv)
    kx = kx.reshape(batch, ctx_len, -1)
    vx = vx.reshape(batch, ctx_len, -1)
    o = _attention(at_sink[0], q, kd, vd, kx, vx, batch, seq, tq)

    out = _attn_out(xl, o, at_w_out[0].astype(_BF16), mod[1], gffn1, w1, w2, 1,
                    norm_final_g.reshape(1, d), 1024, lat_row(1024))
    return out.reshape(batch, seq, d)
```

```python
import functools

import jax
import jax.numpy as jnp
from jax import lax
from jax.experimental import pallas as pl
from jax.experimental.pallas import tpu as pltpu

D_MODEL = 1024
N_MOD = 6
EPS = 1e-6
NEG = -1e30
D_LRU = 512
D_SC = 512
LRU_HEADS = 8
LRU_HEAD_DIM = 64
LRU_C = 8.0
RC_IN_WIDTH = 2 * D_LRU + 3 * D_SC
HEAD_DIM = 64
N_Q_HEADS = 16
N_KV_HEADS = 4
GQA_GROUP = 4
WINDOW = 128
GRID_W = 64
ROPE_BASE = 10000.0
ROPE_FREQS = 16
D_FF = 2816
LOG2E = 1.4426950408889634

V7X_LANES = 128
V7X_SUBLANES = 8
V7X_MXU_DIM = 256
V7X_VMEM_LIMIT = 60 * 1024 * 1024

HALO = V7X_SUBLANES
HALO_BF16 = 2 * V7X_SUBLANES
N_SLAB = D_LRU // V7X_LANES
SCAN_SHIFTS = (1, 2, 4)
FFN_CHUNKS = ((0, 1024), (1024, 1024), (2048, 768))
FFN_SUB_ROWS = 512

_BF16 = jnp.bfloat16
_F32 = jnp.float32


def _dot(a, b):
    return jnp.dot(a, b, preferred_element_type=_F32)


def _dot_t(a, b):
    return lax.dot_general(a, b, (((1,), (1,)), ((), ())), preferred_element_type=_F32)


def _sigmoid(x):
    return 0.5 * jnp.tanh(0.5 * x) + 0.5


def _gelu_tanh(x):
    return 0.5 * x * (1.0 + jnp.tanh(0.7978845608028654 * (x + 0.044715 * (x * x * x))))


def _rms_mod(x, gain, shift, scale):
    ms = jnp.mean(x * x, axis=-1, keepdims=True)
    return (x * lax.rsqrt(ms + EPS) * gain) * (1.0 + scale) + shift


def _const_spec(shape, index=None):
    idx = (0,) * len(shape) if index is None else index
    return pl.BlockSpec(shape, lambda *_: idx, pipeline_mode=pl.Buffered(1))


def _params(sem):
    return pltpu.CompilerParams(dimension_semantics=sem, vmem_limit_bytes=V7X_VMEM_LIMIT)


def _mod_kernel(c_ref, w_ref, b_ref, o_ref):
    c = c_ref[...]
    s = (c * _sigmoid(c)).astype(_BF16)
    o_ref[0] = _dot(s, w_ref[0].astype(_BF16)) + b_ref[0]


def _modulation(cond, ada_w, ada_b):
    depth, d, n = ada_w.shape
    bn = 1536
    return pl.pallas_call(
        _mod_kernel,
        grid=(depth, n // bn),
        in_specs=[
            pl.BlockSpec((8, d), lambda l, j: (0, 0)),
            pl.BlockSpec((1, d, bn), lambda l, j: (l, 0, j)),
            pl.BlockSpec((1, 1, bn), lambda l, j: (l, 0, j)),
        ],
        out_specs=pl.BlockSpec((1, 8, bn), lambda l, j: (l, 0, j)),
        out_shape=jax.ShapeDtypeStruct((depth, 8, n), _F32),
        compiler_params=_params(("arbitrary", "arbitrary")),
        name="adaln_mod",
    )(cond, ada_w, ada_b.reshape(depth, 1, n))


def _proj_kernel(x_ref, mod_ref, g_ref, w_ref, xa_ref, rest_ref):
    h = _rms_mod(x_ref[...], g_ref[...], mod_ref[0, 0:1, :], mod_ref[0, 1:2, :]).astype(_BF16)
    xa = _dot(h, w_ref[:, 0:D_LRU])
    for c in range(N_SLAB):
        xa_ref[c] = xa[:, c * V7X_LANES:(c + 1) * V7X_LANES]
    ga = _dot(h, w_ref[:, D_LRU:2 * D_LRU])
    rest_ref[:, 0:D_LRU] = _gelu_tanh(ga).astype(_BF16)
    rest_ref[:, D_LRU:D_LRU + D_SC] = _dot(h, w_ref[:, 2 * D_LRU:2 * D_LRU + D_SC]).astype(_BF16)
    cv = _dot(h, w_ref[:, 2 * D_LRU + D_SC:2 * D_LRU + 3 * D_SC])
    rest_ref[:, D_LRU + D_SC:D_LRU + 2 * D_SC] = (cv[:, 0:D_SC] * cv[:, D_SC:2 * D_SC]).astype(_BF16)


def _project(x2d, mod, gain, w, tm, row_of_tile):
    n, d = x2d.shape
    nrest = D_LRU + 2 * D_SC
    return pl.pallas_call(
        _proj_kernel,
        grid=(n // tm,),
        in_specs=[
            pl.BlockSpec((tm, d), lambda i: (i, 0)),
            pl.BlockSpec((1, N_MOD, d), lambda i: (row_of_tile(i), 0, 0)),
            _const_spec((1, d)),
            _const_spec(w.shape),
        ],
        out_specs=[
            pl.BlockSpec((N_SLAB, tm, V7X_LANES), lambda i: (0, i, 0)),
            pl.BlockSpec((tm, nrest), lambda i: (i, 0)),
        ],
        out_shape=[
            jax.ShapeDtypeStruct((N_SLAB, n, V7X_LANES), _F32),
            jax.ShapeDtypeStruct((n, nrest), _BF16),
        ],
        compiler_params=_params(("parallel",)),
        name="rc_in_proj",
    )(x2d, mod, gain, w)


def _scan_kernel(xm_f, xp_f, xn_f, xm_b, xp_b, xn_b, cw_ref, cb_ref, wg_ref, bg_ref, lam_ref, h0_ref,
                 hf_ref, hb_ref, hl_ref, xs_scr, carry_scr, *level_scr, nc, tc):
    j = pl.program_id(1)
    lo = HALO
    ngroups = tc // V7X_SUBLANES
    nlev = len(SCAN_SHIFTS)
    a_lv = list(level_scr[:nlev])
    b_lv = list(level_scr[nlev:])
    pad_rows = (slice(0, HALO), slice(lo + tc, lo + tc + HALO))

    @pl.when(j == 0)
    def _():
        carry_scr[...] = h0_ref[0]
        for d in range(2):
            for ref in a_lv:
                ref[d, :, pad_rows[d], :] = jnp.ones((N_SLAB, HALO, V7X_LANES), _F32)
            for ref in b_lv:
                ref[d, :, pad_rows[d], :] = jnp.zeros((N_SLAB, HALO, V7X_LANES), _F32)

    cw = cw_ref[...]
    cb = cb_ref[...]

    def coeffs(d, xm, xp, xn, has_prev, has_next):
        xs_scr[d, :, 0:HALO, :] = jnp.where(has_prev, xp[...], 0.0)
        xs_scr[d, :, lo:lo + tc, :] = xm[...]
        xs_scr[d, :, lo + tc:lo + tc + HALO, :] = jnp.where(has_next, xn[...], 0.0)
        xcs = []
        for c in range(N_SLAB):
            lanes = slice(c * V7X_LANES, (c + 1) * V7X_LANES)
            acc = cb[:, lanes]
            for k in range(4):
                acc = acc + cw[k:k + 1, lanes] * xs_scr[d, c, lo - 2 + k:lo - 2 + k + tc, :]
            xcs.append(acc)
        lam = lam_ref[d]
        nlam = -lam
        softplus = jnp.maximum(nlam, 0.0) + jnp.log(1.0 + jnp.exp(-jnp.abs(nlam)))
        hrate = (-0.5 * LRU_C * LOG2E) * softplus
        half = V7X_MXU_DIM
        per_half = half // V7X_LANES
        for hh in range(D_LRU // half):
            xch = jnp.concatenate(xcs[hh * per_half:(hh + 1) * per_half], axis=1)
            z = _dot(xch.astype(_BF16), wg_ref[d, hh]) + bg_ref[d, hh]
            for cc in range(per_half):
                c = hh * per_half + cc
                lanes = slice(c * V7X_LANES, (c + 1) * V7X_LANES)
                tr = jnp.tanh(z[:, cc * V7X_LANES:(cc + 1) * V7X_LANES])
                ig = 0.5 * jnp.tanh(z[:, half + cc * V7X_LANES:half + (cc + 1) * V7X_LANES]) + 0.5
                a = jnp.exp2(tr * hrate[:, lanes] + hrate[:, lanes])
                one_m_a2 = jnp.maximum(1.0 - a * a, 1e-12)
                mult = one_m_a2 * lax.rsqrt(one_m_a2)
                a_lv[0][d, c, lo:lo + tc, :] = a
                b_lv[0][d, c, lo:lo + tc, :] = mult * ig * xcs[c]

    coeffs(0, xm_f, xp_f, xn_f, j > 0, j < nc - 1)
    coeffs(1, xm_b, xp_b, xn_b, j < nc - 1, j > 0)

    def step(g, carry):
        out = []
        for d, h_ref in ((0, hf_ref), (1, hb_ref)):
            grp = g if d == 0 else ngroups - 1 - g
            row0 = pl.multiple_of(lo + grp * V7X_SUBLANES, V7X_SUBLANES)
            cur = pl.ds(row0, V7X_SUBLANES)
            for c in range(N_SLAB):
                a = a_lv[0][d, c, cur, :]
                b = b_lv[0][d, c, cur, :]
                for lvl, shift in enumerate(SCAN_SHIFTS):
                    sh = pl.ds(row0 + (shift if d else -shift), V7X_SUBLANES)
                    b = a * b_lv[lvl][d, c, sh, :] + b
                    a = a * a_lv[lvl][d, c, sh, :]
                    if lvl + 1 < nlev:
                        a_lv[lvl + 1][d, c, cur, :] = a
                        b_lv[lvl + 1][d, c, cur, :] = b
                h = a * carry[d * N_SLAB + c] + b
                out_rows = pl.ds(pl.multiple_of(grp * V7X_SUBLANES, V7X_SUBLANES), V7X_SUBLANES)
                h_ref[out_rows, c * V7X_LANES:(c + 1) * V7X_LANES] = h
                out.append(h)
        return tuple(out)

    init = tuple(carry_scr[d, :, c * V7X_LANES:(c + 1) * V7X_LANES]
                 for d in range(2) for c in range(N_SLAB))
    carry = lax.fori_loop(0, ngroups, step, init, unroll=4)

    edge_row = (V7X_SUBLANES - 1, 0)
    data_edge = (slice(tc, tc + HALO), slice(lo, lo + HALO))
    for d in range(2):
        for c in range(N_SLAB):
            lanes = slice(c * V7X_LANES, (c + 1) * V7X_LANES)
            h = carry[d * N_SLAB + c]
            carry_scr[d, :, lanes] = h
            hl_ref[0, d, :, lanes] = jnp.broadcast_to(h[edge_row[d]:edge_row[d] + 1, :], h.shape)
        for ref in a_lv + b_lv:
            ref[d, :, pad_rows[d], :] = ref[d, :, data_edge[d], :]


def _scan(xa, h0, conv_w, conv_b, w_gate, b_gate, lam, batch, seq, tc):
    n = xa.shape[1]
    nc = seq // tc
    tb = tc // HALO
    sb = seq // HALO
    last = n // HALO - 1

    def fwd(b, j):
        return j

    def bwd(b, j):
        return nc - 1 - j

    def main(cf):
        return pl.BlockSpec((N_SLAB, tc, V7X_LANES), lambda b, j: (0, b * nc + cf(b, j), 0))

    def prev(cf):
        return pl.BlockSpec((N_SLAB, HALO, V7X_LANES),
                            lambda b, j: (0, jnp.maximum(b * sb + cf(b, j) * tb - 1, 0), 0))

    def nxt(cf):
        return pl.BlockSpec((N_SLAB, HALO, V7X_LANES),
                            lambda b, j: (0, jnp.minimum(b * sb + (cf(b, j) + 1) * tb, last), 0))

    kernel = functools.partial(_scan_kernel, nc=nc, tc=tc)
    nlev = len(SCAN_SHIFTS)
    return pl.pallas_call(
        kernel,
        grid=(batch, nc),
        in_specs=[
            main(fwd), prev(fwd), nxt(fwd), main(bwd), prev(bwd), nxt(bwd),
            _const_spec((4, D_LRU)),
            _const_spec((1, D_LRU)),
            _const_spec(w_gate.shape),
            _const_spec(b_gate.shape),
            _const_spec((2, 1, D_LRU)),
            pl.BlockSpec((1, 2, V7X_SUBLANES, D_LRU), lambda b, j: (b, 0, 0, 0)),
        ],
        out_specs=[
            pl.BlockSpec((tc, D_LRU), lambda b, j: (b * nc + j, 0)),
            pl.BlockSpec((tc, D_LRU), lambda b, j: (b * nc + nc - 1 - j, 0)),
            pl.BlockSpec((1, 2, V7X_SUBLANES, D_LRU), lambda b, j: (b, 0, 0, 0)),
        ],
        out_shape=[
            jax.ShapeDtypeStruct((n, D_LRU), _F32),
            jax.ShapeDtypeStruct((n, D_LRU), _F32),
            jax.ShapeDtypeStruct((batch, 2, V7X_SUBLANES, D_LRU), _F32),
        ],
        scratch_shapes=[
            pltpu.VMEM((2, N_SLAB, tc + 2 * HALO, V7X_LANES), _F32),
            pltpu.VMEM((2, V7X_SUBLANES, D_LRU), _F32),
        ] + [pltpu.VMEM((2, N_SLAB, tc + 2 * HALO, V7X_LANES), _F32) for _ in range(2 * nlev)],
        compiler_params=_params(("arbitrary", "arbitrary")),
        name="rglru_scan",
    )(xa, xa, xa, xa, xa, xa, conv_w, conv_b, w_gate, b_gate, lam, h0)


def _ffn_norm(x, y, mod_ref, gffn_ref):
    x1 = x + mod_ref[0, 2:3, :] * y
    h = _rms_mod(x1, gffn_ref[...], mod_ref[0, 3:4, :], mod_ref[0, 4:5, :]).astype(_BF16)
    return x1, h


def _ffn_apply(x1, h, mod_ref, w1_ref, w2_ref, gfin_ref):
    acc = None
    for c0, cw in FFN_CHUNKS:
        g = _dot(h, w1_ref[0, :, c0:c0 + cw])
        u = _dot(h, w1_ref[0, :, D_FF + c0:D_FF + c0 + cw])
        act = (g * _sigmoid(g) * u).astype(_BF16)
        part = _dot(act, w2_ref[0, c0:c0 + cw, :])
        acc = part if acc is None else acc + part
    x2 = x1 + mod_ref[0, 5:6, :] * acc
    if gfin_ref is not None:
        ms = jnp.mean(x2 * x2, axis=-1, keepdims=True)
        x2 = x2 * lax.rsqrt(ms + EPS) * gfin_ref[...]
    return x2


def _ffn_weight_specs(w1, w2, layer):
    return [_const_spec((1,) + w1.shape[1:], (layer, 0, 0)), _const_spec((1,) + w2.shape[1:], (layer, 0, 0))]


def _rc_out_kernel(x_ref, hf_ref, hb_ref, gg_ref, bg_ref, um_ref, up_ref, un_ref,
                   cbw_ref, wo_ref, mod_ref, gffn_ref, w1_ref, w2_ref, o_ref, *, nt, tm):
    i = pl.program_id(0)
    has_prev = (i % nt) > 0
    has_next = (i % nt) < nt - 1
    prev = jnp.where(has_prev, up_ref[...].astype(_F32), 0.0)
    nxt = jnp.where(has_next, un_ref[...].astype(_F32), 0.0)
    uext = jnp.concatenate([prev, um_ref[...].astype(_F32), nxt], axis=0)
    cbw = cbw_ref[...]
    sub = min(FFN_SUB_ROWS, tm)
    normed = []
    for r0 in range(0, tm, sub):
        rows = slice(r0, r0 + sub)
        ya = ((hf_ref[rows, :] + hb_ref[rows, :]) * gg_ref[rows, :].astype(_F32)).astype(_BF16)
        conv = None
        for k in range(3):
            first = HALO_BF16 - 1 + k + r0
            term = cbw[k:k + 1, :] * uext[first:first + sub, :]
            conv = term if conv is None else conv + term
        yb = (bg_ref[rows, :].astype(_F32) * conv).astype(_BF16)
        y = _dot(ya, wo_ref[0:D_LRU, :]) + _dot(yb, wo_ref[D_LRU:D_LRU + D_SC, :])
        normed.append(_ffn_norm(x_ref[rows, :], y, mod_ref, gffn_ref))
    for k, r0 in enumerate(range(0, tm, sub)):
        x1, h = normed[k]
        o_ref[r0:r0 + sub, :] = _ffn_apply(x1, h, mod_ref, w1_ref, w2_ref, None)


def _rc_out(x2d, hf, hb, p, conv_b_w, w_out, mod, gffn, w1, w2, layer, tm, seq, row_of_tile):
    n, d = x2d.shape
    nt = seq // tm
    tb = tm // HALO_BF16
    last = n // HALO_BF16 - 1

    def col(c):
        return pl.BlockSpec((tm, D_SC), lambda i: (i, c))

    kernel = functools.partial(_rc_out_kernel, nt=nt, tm=tm)
    return pl.pallas_call(
        kernel,
        grid=(n // tm,),
        in_specs=[
            pl.BlockSpec((tm, d), lambda i: (i, 0)),
            pl.BlockSpec((tm, D_LRU), lambda i: (i, 0)),
            pl.BlockSpec((tm, D_LRU), lambda i: (i, 0)),
            col(0), col(1), col(2),
            pl.BlockSpec((HALO_BF16, D_SC), lambda i: (jnp.maximum(i * tb - 1, 0), 2)),
            pl.BlockSpec((HALO_BF16, D_SC), lambda i: (jnp.minimum((i + 1) * tb, last), 2)),
            _const_spec((3, D_SC)),
            _const_spec(w_out.shape),
            pl.BlockSpec((1, N_MOD, d), lambda i: (row_of_tile(i), 0, 0)),
            _const_spec((1, d)),
        ] + _ffn_weight_specs(w1, w2, layer),
        out_specs=pl.BlockSpec((tm, d), lambda i: (i, 0)),
        out_shape=jax.ShapeDtypeStruct((n, d), _F32),
        compiler_params=_params(("parallel",)),
        name="rc_out_ffn",
    )(x2d, hf, hb, p, p, p, p, p, conv_b_w, w_out, mod, gffn, w1, w2)


def _attn_out_kernel(x_ref, o_in_ref, wo_ref, mod_ref, gffn_ref, gfin_ref, w1_ref, w2_ref, o_ref):
    tm = x_ref.shape[0]
    sub = min(FFN_SUB_ROWS, tm)
    normed = []
    for r0 in range(0, tm, sub):
        rows = slice(r0, r0 + sub)
        o_in = jnp.concatenate([o_in_ref[c, rows, :] for c in range(o_in_ref.shape[0])], axis=1)
        normed.append(_ffn_norm(x_ref[rows, :], _dot(o_in, wo_ref[...]), mod_ref, gffn_ref))
    for k, r0 in enumerate(range(0, tm, sub)):
        x1, h = normed[k]
        o_ref[r0:r0 + sub, :] = _ffn_apply(x1, h, mod_ref, w1_ref, w2_ref, gfin_ref)


def _attn_out(x2d, o, w_out, mod, gffn, w1, w2, layer, gfin, tm, row_of_tile):
    n, d = x2d.shape
    return pl.pallas_call(
        _attn_out_kernel,
        grid=(n // tm,),
        in_specs=[
            pl.BlockSpec((tm, d), lambda i: (i, 0)),
            pl.BlockSpec((o.shape[0], tm, V7X_LANES), lambda i: (0, i, 0)),
            _const_spec(w_out.shape),
            pl.BlockSpec((1, N_MOD, d), lambda i: (row_of_tile(i), 0, 0)),
            _const_spec((1, d)),
            _const_spec((1, d)),
        ] + _ffn_weight_specs(w1, w2, layer),
        out_specs=pl.BlockSpec((tm, d), lambda i: (i, 0)),
        out_shape=jax.ShapeDtypeStruct((n, d), _F32),
        compiler_params=_params(("parallel",)),
        name="attn_out_ffn",
    )(x2d, o, w_out, mod, gffn, gfin, w1, w2)


def _rope(x, cos, sin_signed, lo_half):
    partner = jnp.where(lo_half, pltpu.roll(x, V7X_LANES - ROPE_FREQS, 1), pltpu.roll(x, ROPE_FREQS, 1))
    return x * cos + partner * sin_signed


def _rope_lane_tables(cs):
    lane = lax.broadcasted_iota(jnp.int32, cs.shape, 1)
    in_head = lane % HEAD_DIM
    f = ROPE_FREQS
    back = lambda k: pltpu.roll(cs, k, 1)
    fwd = lambda k: pltpu.roll(cs, V7X_LANES - k, 1)
    cos_h = jnp.where(in_head < f, cs, jnp.where(in_head < 3 * f, back(f), back(2 * f)))
    sin_h = jnp.where(in_head < f, fwd(2 * f), jnp.where(in_head < 3 * f, fwd(f), cs))
    first = lane < HEAD_DIM
    cos = jnp.where(first, cos_h, pltpu.roll(cos_h, HEAD_DIM, 1))
    sin = jnp.where(first, sin_h, pltpu.roll(sin_h, HEAD_DIM, 1))
    lo_half = (lane % (2 * f)) < f
    return cos, jnp.where(lo_half, -sin, sin), lo_half


def _heads_to_slabs(x, ref, c):
    first = lax.broadcasted_iota(jnp.int32, x.shape, 1) < HEAD_DIM
    swapped = pltpu.roll(x, HEAD_DIM, 1)
    ref[2 * c] = jnp.where(first, x, swapped).astype(_BF16)
    ref[2 * c + 1] = jnp.where(first, swapped, x).astype(_BF16)


def _qkv_kernel(x_ref, mod_ref, g_ref, w_ref, tab_ref, q_ref, k_ref, v_ref, *, nt, tm):
    h = _rms_mod(x_ref[...], g_ref[...], mod_ref[0, 0:1, :], mod_ref[0, 1:2, :]).astype(_BF16)
    col_terms = tab_ref[0:GRID_W, :]
    first_grid_row = GRID_W + (pl.program_id(0) % nt) * (tm // GRID_W)
    cs = jnp.concatenate([col_terms + tab_ref[pl.ds(first_grid_row + g, 1), :] for g in range(tm // GRID_W)],
                         axis=0)
    cos, sin, lo_half = _rope_lane_tables(cs)
    dq = N_Q_HEADS * HEAD_DIM
    dkv = N_KV_HEADS * HEAD_DIM
    scale = HEAD_DIM ** -0.5 * LOG2E
    q = _dot(h, w_ref[:, 0:dq])
    for c in range(dq // V7X_LANES):
        cols = slice(c * V7X_LANES, (c + 1) * V7X_LANES)
        q_ref[c] = (_rope(q[:, cols], cos, sin, lo_half) * scale).astype(_BF16)
    k = _dot(h, w_ref[:, dq:dq + dkv])
    v = _dot(h, w_ref[:, dq + dkv:dq + 2 * dkv])
    for c in range(dkv // V7X_LANES):
        cols = slice(c * V7X_LANES, (c + 1) * V7X_LANES)
        _heads_to_slabs(_rope(k[:, cols], cos, sin, lo_half), k_ref, c)
        _heads_to_slabs(v[:, cols], v_ref, c)


def _qkv(x2d, mod, gain, w, cs_t, tm, seq, row_of_tile):
    n, d = x2d.shape
    nt = seq // tm
    nq_slab = N_Q_HEADS * HEAD_DIM // V7X_LANES
    assert tm % GRID_W == 0
    return pl.pallas_call(
        functools.partial(_qkv_kernel, nt=nt, tm=tm),
        grid=(n // tm,),
        in_specs=[
            pl.BlockSpec((tm, d), lambda i: (i, 0)),
            pl.BlockSpec((1, N_MOD, d), lambda i: (row_of_tile(i), 0, 0)),
            _const_spec((1, d)),
            _const_spec(w.shape),
            _const_spec(cs_t.shape),
        ],
        out_specs=[
            pl.BlockSpec((nq_slab, tm, V7X_LANES), lambda i: (0, i, 0)),
            pl.BlockSpec((N_KV_HEADS, tm, V7X_LANES), lambda i: (0, i, 0)),
            pl.BlockSpec((N_KV_HEADS, tm, V7X_LANES), lambda i: (0, i, 0)),
        ],
        out_shape=[
            jax.ShapeDtypeStruct((nq_slab, n, V7X_LANES), _BF16),
            jax.ShapeDtypeStruct((N_KV_HEADS, n, V7X_LANES), _BF16),
            jax.ShapeDtypeStruct((N_KV_HEADS, n, V7X_LANES), _BF16),
        ],
        compiler_params=_params(("parallel",)),
        name="qkv_rope",
    )(x2d, mod, gain, w, cs_t)


def _kv_kernel(x_ref, mod_ref, g_ref, w_ref, k_ref, v_ref):
    h = _rms_mod(x_ref[...], g_ref[...], mod_ref[0, 0:1, :], mod_ref[0, 1:2, :]).astype(_BF16)
    dq = N_Q_HEADS * HEAD_DIM
    dkv = N_KV_HEADS * HEAD_DIM
    k = _dot(h, w_ref[:, dq:dq + dkv])
    v = _dot(h, w_ref[:, dq + dkv:dq + 2 * dkv])
    for c in range(dkv // V7X_LANES):
        cols = slice(c * V7X_LANES, (c + 1) * V7X_LANES)
        _heads_to_slabs(k[:, cols], k_ref, c)
        _heads_to_slabs(v[:, cols], v_ref, c)


def _ctx_kv(x2d, mod, gain, w):
    n, d = x2d.shape
    slab = (N_KV_HEADS, n, V7X_LANES)
    return pl.pallas_call(
        _kv_kernel,
        grid=(1,),
        in_specs=[
            pl.BlockSpec((n, d), lambda i: (0, 0)),
            pl.BlockSpec((1, N_MOD, d), lambda i: (0, 0, 0)),
            _const_spec((1, d)),
            _const_spec(w.shape),
        ],
        out_specs=[pl.BlockSpec(slab, lambda i: (0, 0, 0)), pl.BlockSpec(slab, lambda i: (0, 0, 0))],
        out_shape=[jax.ShapeDtypeStruct(slab, _BF16), jax.ShapeDtypeStruct(slab, _BF16)],
        compiler_params=_params(("arbitrary",)),
        name="ctx_kv",
    )(x2d, mod, gain, w)


def _attn_kernel(sink_ref, q_ref, kp_ref, kc_ref, kn_ref, vp_ref, vc_ref, vn_ref, kx_ref, vx_ref, o_ref,
                 kext, vext, s_even, s_odd, *, tq, nq):
    n = pl.program_id(1)
    blk = WINDOW
    nsub = tq // blk
    nblocks = nsub * N_KV_HEADS
    ctx_len = kx_ref.shape[1]
    kext[:, 0:blk, :] = kp_ref[...]
    kext[:, blk:blk + tq, :] = kc_ref[...]
    kext[:, blk + tq:2 * blk + tq, :] = kn_ref[...]
    vext[:, 0:blk, :] = vp_ref[...]
    vext[:, blk:blk + tq, :] = vc_ref[...]
    vext[:, blk + tq:2 * blk + tq, :] = vn_ref[...]
    qi = lax.broadcasted_iota(jnp.int32, (blk, blk), 0)
    kj = lax.broadcasted_iota(jnp.int32, (blk, blk), 1)
    lane = lax.broadcasted_iota(jnp.int32, (blk, V7X_LANES), 1)
    lo = lane < HEAD_DIM
    ones_win = jnp.ones((3 * blk, V7X_LANES), _BF16)
    ones_ctx = jnp.ones((ctx_len, V7X_LANES), _BF16)

    def locate(i):
        j = i // N_KV_HEADS
        return j, i % N_KV_HEADS, j * blk

    def scores(i, s_ref):
        _, hkv, row0 = locate(i)
        parts = []
        for c in range(2):
            qc = q_ref[2 * hkv + c, pl.ds(row0, blk), :]
            parts.append(jnp.where(lo, qc, jnp.zeros_like(qc)))
            parts.append(jnp.where(lo, jnp.zeros_like(qc), qc))
        lhs = jnp.concatenate(parts, axis=0)
        s_ref[:, 0:3 * blk] = _dot_t(lhs, kext[hkv, pl.ds(row0, 3 * blk), :])
        s_ref[:, 3 * blk:3 * blk + ctx_len] = _dot_t(lhs, kx_ref[hkv])

    def softmax_pv(i, s_ref):
        j, hkv, row0 = locate(i)
        keep_prev = kj >= qi
        keep_next = kj <= qi
        if j == 0:
            keep_prev = keep_prev & (n > 0)
        if j == nsub - 1:
            keep_next = keep_next & (n < nq - 1)
        vwin = jnp.concatenate([vext[hkv, pl.ds(row0, 3 * blk), :], ones_win], axis=1)
        vx = jnp.concatenate([vx_ref[hkv], ones_ctx], axis=1)
        p_win, p_ctx, esink = [], [], []
        for g in range(GQA_GROUP):
            sink = sink_ref[hkv * GQA_GROUP + g] * LOG2E
            grow = slice(g * blk, (g + 1) * blk)
            s0 = jnp.where(keep_prev, s_ref[grow, 0:blk], NEG)
            s1 = s_ref[grow, blk:2 * blk]
            s2 = jnp.where(keep_next, s_ref[grow, 2 * blk:3 * blk], NEG)
            sc = [s_ref[grow, 3 * blk + t * V7X_LANES:3 * blk + (t + 1) * V7X_LANES]
                  for t in range(ctx_len // V7X_LANES)]
            mm = jnp.maximum(jnp.maximum(s0, s1), s2)
            for t in sc:
                mm = jnp.maximum(mm, t)
            m = jnp.maximum(jnp.max(mm, axis=-1, keepdims=True), sink)
            p_win.append(jnp.concatenate([jnp.exp2(t - m).astype(_BF16) for t in (s0, s1, s2)], axis=1))
            p_ctx.append(jnp.concatenate([jnp.exp2(t - m).astype(_BF16) for t in sc], axis=1))
            esink.append(jnp.exp2(sink - m))
        o = _dot(jnp.concatenate(p_win, axis=0), vwin) + _dot(jnp.concatenate(p_ctx, axis=0), vx)
        og = []
        for g in range(GQA_GROUP):
            grow = slice(g * blk, (g + 1) * blk)
            den = o[grow, V7X_LANES:2 * V7X_LANES] + esink[g]
            og.append(o[grow, 0:V7X_LANES] * (1.0 / den))
        for c in range(2):
            ocol = jnp.where(lo, og[2 * c], og[2 * c + 1])
            o_ref[2 * hkv + c, pl.ds(row0, blk), :] = ocol.astype(_BF16)

    s_bufs = (s_even, s_odd)
    scores(0, s_bufs[0])
    for i in range(nblocks):
        if i + 1 < nblocks:
            scores(i + 1, s_bufs[(i + 1) % 2])
        softmax_pv(i, s_bufs[i % 2])


def _attention(sink, q, kd, vd, kx, vx, batch, seq, ctx_len, tq):
    nq_slab, n, _ = q.shape
    nq = seq // tq
    hb = tq // WINDOW
    sb = seq // WINDOW
    last = n // WINDOW - 1
    kv = N_KV_HEADS

    def cur():
        return pl.BlockSpec((kv, tq, V7X_LANES), lambda b, i: (0, b * nq + i, 0))

    def prev():
        return pl.BlockSpec((kv, WINDOW, V7X_LANES), lambda b, i: (0, jnp.maximum(b * sb + i * hb - 1, 0), 0))

    def nxt():
        return pl.BlockSpec((kv, WINDOW, V7X_LANES), lambda b, i: (0, jnp.minimum(b * sb + (i + 1) * hb, last), 0))

    def ctx():
        return pl.BlockSpec((kv, ctx_len, V7X_LANES), lambda b, i: (0, b, 0))

    kernel = functools.partial(_attn_kernel, tq=tq, nq=nq)
    return pl.pallas_call(
        kernel,
        grid=(batch, nq),
        in_specs=[
            pl.BlockSpec(memory_space=pltpu.SMEM),
            pl.BlockSpec((nq_slab, tq, V7X_LANES), lambda b, i: (0, b * nq + i, 0)),
            prev(), cur(), nxt(), prev(), cur(), nxt(), ctx(), ctx(),
        ],
        out_specs=pl.BlockSpec((nq_slab, tq, V7X_LANES), lambda b, i: (0, b * nq + i, 0)),
        out_shape=jax.ShapeDtypeStruct(q.shape, _BF16),
        scratch_shapes=[
            pltpu.VMEM((kv, tq + 2 * WINDOW, V7X_LANES), _BF16),
            pltpu.VMEM((kv, tq + 2 * WINDOW, V7X_LANES), _BF16),
            pltpu.VMEM((GQA_GROUP * WINDOW, 3 * WINDOW + ctx_len), _F32),
            pltpu.VMEM((GQA_GROUP * WINDOW, 3 * WINDOW + ctx_len), _F32),
        ],
        compiler_params=_params(("parallel", "parallel")),
        name="band_attn",
    )(sink, q, kd, kd, kd, vd, vd, vd, kx, vx)


def _gate_weights(r_w, r_b, i_w, i_b):
    heads_per_half = V7X_MXU_DIM // LRU_HEAD_DIM
    eye = jnp.eye(heads_per_half, dtype=_F32)

    def halves(w):
        w = w.reshape(2, D_LRU // V7X_MXU_DIM, heads_per_half, LRU_HEAD_DIM, LRU_HEAD_DIM)
        bd = jnp.einsum('dxhij,hk->dxhikj', w, eye)
        return bd.reshape(2, D_LRU // V7X_MXU_DIM, V7X_MXU_DIM, V7X_MXU_DIM)

    w = (0.5 * jnp.concatenate([halves(r_w), halves(i_w)], axis=-1)).astype(_BF16)
    rb = r_b.reshape(2, D_LRU // V7X_MXU_DIM, 1, V7X_MXU_DIM)
    ib = i_b.reshape(2, D_LRU // V7X_MXU_DIM, 1, V7X_MXU_DIM)
    return w, 0.5 * jnp.concatenate([rb, ib], axis=-1)


def _rope_table(seq):
    grid_rows = seq // GRID_W
    inv_freq = ROPE_BASE ** (-jnp.arange(ROPE_FREQS, dtype=_F32) / ROPE_FREQS)
    pos = jnp.concatenate([jnp.arange(GRID_W), jnp.arange(grid_rows)]).astype(_F32)
    ang = pos[:, None] * inv_freq
    cos, sin = jnp.cos(ang), jnp.sin(ang)
    is_col = (jnp.arange(GRID_W + grid_rows) < GRID_W)[:, None]
    zero = jnp.zeros_like(cos)
    table = jnp.concatenate([jnp.where(is_col, zero, cos), jnp.where(is_col, cos, zero),
                             jnp.where(is_col, zero, sin), jnp.where(is_col, sin, zero)], axis=1)
    return jnp.pad(table, ((0, 0), (0, V7X_LANES - 4 * ROPE_FREQS)))


def kernel(x, c, ctx, c_ctx, ada_w, ada_b, norm_mix_g, norm_ffn_g, norm_final_g, ffn_w_in, ffn_w_out,
           rc_w_in, rc_conv_a_w, rc_conv_a_b, rc_gate_r_w, rc_gate_r_b, rc_gate_i_w, rc_gate_i_b,
           rc_lambda, rc_conv_b_w, rc_w_out, at_w_qkv, at_sink, at_w_out):
    batch, seq, d = x.shape
    ctx_len = ctx.shape[1]
    tm = 512
    tm_ffn = 1024
    tq = 512
    tc = 512
    assert seq % tm_ffn == 0 and seq % tm == 0 and seq % tq == 0 and seq % tc == 0
    assert ctx_len % V7X_LANES == 0

    xl = x.reshape(batch * seq, d)
    xc = ctx.reshape(batch * ctx_len, d)

    cond = jnp.concatenate([c_ctx[None], c, jnp.zeros((8 - 1 - batch, d), _F32)], axis=0)
    mod = _modulation(cond, ada_w, ada_b).reshape(ada_w.shape[0], 8, N_MOD, d)

    def lat_row(tile):
        return lambda i: 1 + i // (seq // tile)

    ctx_row = lambda i: 0

    w1 = ffn_w_in.astype(_BF16)
    w2 = ffn_w_out.astype(_BF16)

    w_in = rc_w_in[0].astype(_BF16)
    w_gate, b_gate = _gate_weights(rc_gate_r_w[0], rc_gate_r_b[0], rc_gate_i_w[0], rc_gate_i_b[0])
    lam = rc_lambda[0].reshape(2, 1, D_LRU)
    conv_a_b = rc_conv_a_b[0].reshape(1, D_LRU)
    w_out0 = rc_w_out[0].astype(_BF16)
    gmix0 = norm_mix_g[0].reshape(1, d)
    gffn0 = norm_ffn_g[0].reshape(1, d)

    xa_c, p_c = _project(xc, mod[0], gmix0, w_in, batch * ctx_len, ctx_row)
    xa_l, p_l = _project(xl, mod[0], gmix0, w_in, tm, lat_row(tm))

    h0 = jnp.zeros((batch, 2, V7X_SUBLANES, D_LRU), _F32)
    hf_c, hb_c, h_ctx = _scan(xa_c, h0, rc_conv_a_w[0], conv_a_b, w_gate, b_gate, lam, batch, ctx_len, ctx_len)
    hf_l, hb_l, _ = _scan(xa_l, h_ctx, rc_conv_a_w[0], conv_a_b, w_gate, b_gate, lam, batch, seq, tc)

    xc = _rc_out(xc, hf_c, hb_c, p_c, rc_conv_b_w[0], w_out0, mod[0], gffn0, w1, w2, 0,
                 ctx_len, ctx_len, ctx_row)
    xl = _rc_out(xl, hf_l, hb_l, p_l, rc_conv_b_w[0], w_out0, mod[0], gffn0, w1, w2, 0,
                 tm_ffn, seq, lat_row(tm_ffn))

    w_qkv = at_w_qkv[0].astype(_BF16)
    gmix1 = norm_mix_g[1].reshape(1, d)
    gffn1 = norm_ffn_g[1].reshape(1, d)

    q, kd, vd = _qkv(xl, mod[1], gmix1, w_qkv, _rope_table(seq), tm, seq, lat_row(tm))
    kx, vx = _ctx_kv(xc, mod[1], gmix1, w_qkv)
    o = _attention(at_sink[0], q, kd, vd, kx, vx, batch, seq, ctx_len, tq)

    out = _attn_out(xl, o, at_w_out[0].astype(_BF16), mod[1], gffn1, w1, w2, 1,
                    norm_final_g.reshape(1, d), tm_ffn, lat_row(tm_ffn))
    return out.reshape(batch, seq, d)
```

```python
import functools

import jax
import jax.numpy as jnp
from jax import lax
from jax.experimental import pallas as pl
from jax.experimental.pallas import tpu as pltpu

D_MODEL = 1024
N_MOD = 6
EPS = 1e-6
NEG = -1e30
D_LRU = 512
D_SC = 512
LRU_HEADS = 8
LRU_HEAD_DIM = 64
LRU_C = 8.0
RC_IN_WIDTH = 2 * D_LRU + 3 * D_SC
HEAD_DIM = 64
N_Q_HEADS = 16
N_KV_HEADS = 4
GQA_GROUP = 4
WINDOW = 128
GRID_W = 64
ROPE_BASE = 10000.0
ROPE_FREQS = 16
D_FF = 2816
LOG2E = 1.4426950408889634

V7X_LANES = 128
V7X_SUBLANES = 8
V7X_MXU_DIM = 256
V7X_VMEM_LIMIT = 60 * 1024 * 1024

HALO = V7X_SUBLANES
HALO_BF16 = 2 * V7X_SUBLANES
N_SLAB = D_LRU // V7X_LANES
SCAN_SHIFTS = (1, 2, 4)
FFN_CHUNKS = ((0, 1024), (1024, 1024), (2048, 768))
FFN_SUB_ROWS = 512

_BF16 = jnp.bfloat16
_F32 = jnp.float32


def _dot(a, b):
    return jnp.dot(a, b, preferred_element_type=_F32)


def _dot_t(a, b):
    return lax.dot_general(a, b, (((1,), (1,)), ((), ())), preferred_element_type=_F32)


def _sigmoid(x):
    return 0.5 * jnp.tanh(0.5 * x) + 0.5


def _gelu_tanh(x):
    return 0.5 * x * (1.0 + jnp.tanh(0.7978845608028654 * (x + 0.044715 * (x * x * x))))


def _rms_mod(x, gain, shift, scale):
    ms = jnp.mean(x * x, axis=-1, keepdims=True)
    return (x * lax.rsqrt(ms + EPS) * gain) * (1.0 + scale) + shift


def _const_spec(shape, index=None):
    idx = (0,) * len(shape) if index is None else index
    return pl.BlockSpec(shape, lambda *_: idx, pipeline_mode=pl.Buffered(1))


def _params(sem):
    return pltpu.CompilerParams(dimension_semantics=sem, vmem_limit_bytes=V7X_VMEM_LIMIT)


def _mod_kernel(c_ref, w_ref, b_ref, o_ref):
    c = c_ref[...]
    s = (c * _sigmoid(c)).astype(_BF16)
    o_ref[0] = _dot(s, w_ref[0].astype(_BF16)) + b_ref[0]


def _modulation(cond, ada_w, ada_b):
    depth, d, n = ada_w.shape
    bn = 1536
    return pl.pallas_call(
        _mod_kernel,
        grid=(depth, n // bn),
        in_specs=[
            pl.BlockSpec((8, d), lambda l, j: (0, 0)),
            pl.BlockSpec((1, d, bn), lambda l, j: (l, 0, j)),
            pl.BlockSpec((1, 1, bn), lambda l, j: (l, 0, j)),
        ],
        out_specs=pl.BlockSpec((1, 8, bn), lambda l, j: (l, 0, j)),
        out_shape=jax.ShapeDtypeStruct((depth, 8, n), _F32),
        compiler_params=_params(("arbitrary", "arbitrary")),
        name="adaln_mod",
    )(cond, ada_w, ada_b.reshape(depth, 1, n))


def _proj_kernel(x_ref, mod_ref, g_ref, w_ref, xa_ref, rest_ref):
    h = _rms_mod(x_ref[...], g_ref[...], mod_ref[0, 0:1, :], mod_ref[0, 1:2, :]).astype(_BF16)
    xa = _dot(h, w_ref[:, 0:D_LRU])
    for c in range(N_SLAB):
        xa_ref[c] = xa[:, c * V7X_LANES:(c + 1) * V7X_LANES]
    ga = _dot(h, w_ref[:, D_LRU:2 * D_LRU])
    rest_ref[:, 0:D_LRU] = _gelu_tanh(ga).astype(_BF16)
    rest_ref[:, D_LRU:D_LRU + D_SC] = _dot(h, w_ref[:, 2 * D_LRU:2 * D_LRU + D_SC]).astype(_BF16)
    cv = _dot(h, w_ref[:, 2 * D_LRU + D_SC:2 * D_LRU + 3 * D_SC])
    rest_ref[:, D_LRU + D_SC:D_LRU + 2 * D_SC] = (cv[:, 0:D_SC] * cv[:, D_SC:2 * D_SC]).astype(_BF16)


def _project(x2d, mod, gain, w, tm, row_of_tile):
    n, d = x2d.shape
    nrest = D_LRU + 2 * D_SC
    return pl.pallas_call(
        _proj_kernel,
        grid=(n // tm,),
        in_specs=[
            pl.BlockSpec((tm, d), lambda i: (i, 0)),
            pl.BlockSpec((1, N_MOD, d), lambda i: (row_of_tile(i), 0, 0)),
            _const_spec((1, d)),
            _const_spec(w.shape),
        ],
        out_specs=[
            pl.BlockSpec((N_SLAB, tm, V7X_LANES), lambda i: (0, i, 0)),
            pl.BlockSpec((tm, nrest), lambda i: (i, 0)),
        ],
        out_shape=[
            jax.ShapeDtypeStruct((N_SLAB, n, V7X_LANES), _F32),
            jax.ShapeDtypeStruct((n, nrest), _BF16),
        ],
        compiler_params=_params(("parallel",)),
        name="rc_in_proj",
    )(x2d, mod, gain, w)


def _scan_kernel(xm_f, xp_f, xn_f, xm_b, xp_b, xn_b, cw_ref, cb_ref, wg_ref, bg_ref, lam_ref, h0_ref,
                 hf_ref, hb_ref, hl_ref, xs_scr, carry_scr, *level_scr, nc, tc):
    j = pl.program_id(1)
    lo = HALO
    ngroups = tc // V7X_SUBLANES
    nlev = len(SCAN_SHIFTS)
    a_lv = list(level_scr[:nlev])
    b_lv = list(level_scr[nlev:])
    pad_rows = (slice(0, HALO), slice(lo + tc, lo + tc + HALO))

    @pl.when(j == 0)
    def _():
        carry_scr[...] = h0_ref[0]
        for d in range(2):
            for ref in a_lv:
                ref[d, :, pad_rows[d], :] = jnp.ones((N_SLAB, HALO, V7X_LANES), _F32)
            for ref in b_lv:
                ref[d, :, pad_rows[d], :] = jnp.zeros((N_SLAB, HALO, V7X_LANES), _F32)

    cw = cw_ref[...]
    cb = cb_ref[...]

    def coeffs(d, xm, xp, xn, has_prev, has_next):
        xs_scr[d, :, 0:HALO, :] = jnp.where(has_prev, xp[...], 0.0)
        xs_scr[d, :, lo:lo + tc, :] = xm[...]
        xs_scr[d, :, lo + tc:lo + tc + HALO, :] = jnp.where(has_next, xn[...], 0.0)
        xcs = []
        for c in range(N_SLAB):
            lanes = slice(c * V7X_LANES, (c + 1) * V7X_LANES)
            acc = cb[:, lanes]
            for k in range(4):
                acc = acc + cw[k:k + 1, lanes] * xs_scr[d, c, lo - 2 + k:lo - 2 + k + tc, :]
            xcs.append(acc)
        lam = lam_ref[d]
        nlam = -lam
        softplus = jnp.maximum(nlam, 0.0) + jnp.log(1.0 + jnp.exp(-jnp.abs(nlam)))
        hrate = (-0.5 * LRU_C * LOG2E) * softplus
        half = V7X_MXU_DIM
        per_half = half // V7X_LANES
        for hh in range(D_LRU // half):
            xch = jnp.concatenate(xcs[hh * per_half:(hh + 1) * per_half], axis=1)
            z = _dot(xch.astype(_BF16), wg_ref[d, hh]) + bg_ref[d, hh]
            for cc in range(per_half):
                c = hh * per_half + cc
                lanes = slice(c * V7X_LANES, (c + 1) * V7X_LANES)
                tr = jnp.tanh(z[:, cc * V7X_LANES:(cc + 1) * V7X_LANES])
                ig = 0.5 * jnp.tanh(z[:, half + cc * V7X_LANES:half + (cc + 1) * V7X_LANES]) + 0.5
                a = jnp.exp2(tr * hrate[:, lanes] + hrate[:, lanes])
                one_m_a2 = jnp.maximum(1.0 - a * a, 1e-12)
                mult = one_m_a2 * lax.rsqrt(one_m_a2)
                a_lv[0][d, c, lo:lo + tc, :] = a
                b_lv[0][d, c, lo:lo + tc, :] = mult * ig * xcs[c]

    coeffs(0, xm_f, xp_f, xn_f, j > 0, j < nc - 1)
    coeffs(1, xm_b, xp_b, xn_b, j < nc - 1, j > 0)

    def step(g, carry):
        out = []
        for d, h_ref in ((0, hf_ref), (1, hb_ref)):
            grp = g if d == 0 else ngroups - 1 - g
            row0 = pl.multiple_of(lo + grp * V7X_SUBLANES, V7X_SUBLANES)
            cur = pl.ds(row0, V7X_SUBLANES)
            for c in range(N_SLAB):
                a = a_lv[0][d, c, cur, :]
                b = b_lv[0][d, c, cur, :]
                for lvl, shift in enumerate(SCAN_SHIFTS):
                    sh = pl.ds(row0 + (shift if d else -shift), V7X_SUBLANES)
                    b = a * b_lv[lvl][d, c, sh, :] + b
                    a = a * a_lv[lvl][d, c, sh, :]
                    if lvl + 1 < nlev:
                        a_lv[lvl + 1][d, c, cur, :] = a
                        b_lv[lvl + 1][d, c, cur, :] = b
                h = a * carry[d * N_SLAB + c] + b
                out_rows = pl.ds(pl.multiple_of(grp * V7X_SUBLANES, V7X_SUBLANES), V7X_SUBLANES)
                h_ref[out_rows, c * V7X_LANES:(c + 1) * V7X_LANES] = h
                out.append(h)
        return tuple(out)

    init = tuple(carry_scr[d, :, c * V7X_LANES:(c + 1) * V7X_LANES]
                 for d in range(2) for c in range(N_SLAB))
    carry = lax.fori_loop(0, ngroups, step, init, unroll=4)

    edge_row = (V7X_SUBLANES - 1, 0)
    data_edge = (slice(tc, tc + HALO), slice(lo, lo + HALO))
    for d in range(2):
        for c in range(N_SLAB):
            lanes = slice(c * V7X_LANES, (c + 1) * V7X_LANES)
            h = carry[d * N_SLAB + c]
            carry_scr[d, :, lanes] = h
            hl_ref[0, d, :, lanes] = jnp.broadcast_to(h[edge_row[d]:edge_row[d] + 1, :], h.shape)
        for ref in a_lv + b_lv:
            ref[d, :, pad_rows[d], :] = ref[d, :, data_edge[d], :]


def _scan(xa, h0, conv_w, conv_b, w_gate, b_gate, lam, batch, seq, tc):
    n = xa.shape[1]
    nc = seq // tc
    tb = tc // HALO
    sb = seq // HALO
    last = n // HALO - 1

    def fwd(b, j):
        return j

    def bwd(b, j):
        return nc - 1 - j

    def main(cf):
        return pl.BlockSpec((N_SLAB, tc, V7X_LANES), lambda b, j: (0, b * nc + cf(b, j), 0))

    def prev(cf):
        return pl.BlockSpec((N_SLAB, HALO, V7X_LANES),
                            lambda b, j: (0, jnp.maximum(b * sb + cf(b, j) * tb - 1, 0), 0))

    def nxt(cf):
        return pl.BlockSpec((N_SLAB, HALO, V7X_LANES),
                            lambda b, j: (0, jnp.minimum(b * sb + (cf(b, j) + 1) * tb, last), 0))

    kernel = functools.partial(_scan_kernel, nc=nc, tc=tc)
    nlev = len(SCAN_SHIFTS)
    return pl.pallas_call(
        kernel,
        grid=(batch, nc),
        in_specs=[
            main(fwd), prev(fwd), nxt(fwd), main(bwd), prev(bwd), nxt(bwd),
            _const_spec((4, D_LRU)),
            _const_spec((1, D_LRU)),
            _const_spec(w_gate.shape),
            _const_spec(b_gate.shape),
            _const_spec((2, 1, D_LRU)),
            pl.BlockSpec((1, 2, V7X_SUBLANES, D_LRU), lambda b, j: (b, 0, 0, 0)),
        ],
        out_specs=[
            pl.BlockSpec((tc, D_LRU), lambda b, j: (b * nc + j, 0)),
            pl.BlockSpec((tc, D_LRU), lambda b, j: (b * nc + nc - 1 - j, 0)),
            pl.BlockSpec((1, 2, V7X_SUBLANES, D_LRU), lambda b, j: (b, 0, 0, 0)),
        ],
        out_shape=[
            jax.ShapeDtypeStruct((n, D_LRU), _F32),
            jax.ShapeDtypeStruct((n, D_LRU), _F32),
            jax.ShapeDtypeStruct((batch, 2, V7X_SUBLANES, D_LRU), _F32),
        ],
        scratch_shapes=[
            pltpu.VMEM((2, N_SLAB, tc + 2 * HALO, V7X_LANES), _F32),
            pltpu.VMEM((2, V7X_SUBLANES, D_LRU), _F32),
        ] + [pltpu.VMEM((2, N_SLAB, tc + 2 * HALO, V7X_LANES), _F32) for _ in range(2 * nlev)],
        compiler_params=_params(("arbitrary", "arbitrary")),
        name="rglru_scan",
    )(xa, xa, xa, xa, xa, xa, conv_w, conv_b, w_gate, b_gate, lam, h0)


def _ffn_norm(x, y, mod_ref, gffn_ref):
    x1 = x + mod_ref[0, 2:3, :] * y
    h = _rms_mod(x1, gffn_ref[...], mod_ref[0, 3:4, :], mod_ref[0, 4:5, :]).astype(_BF16)
    return x1, h


def _ffn_apply(x1, h, mod_ref, w1_ref, w2_ref, gfin_ref):
    acc = None
    for c0, cw in FFN_CHUNKS:
        g = _dot(h, w1_ref[0, :, c0:c0 + cw])
        u = _dot(h, w1_ref[0, :, D_FF + c0:D_FF + c0 + cw])
        act = (g * _sigmoid(g) * u).astype(_BF16)
        part = _dot(act, w2_ref[0, c0:c0 + cw, :])
        acc = part if acc is None else acc + part
    x2 = x1 + mod_ref[0, 5:6, :] * acc
    if gfin_ref is not None:
        ms = jnp.mean(x2 * x2, axis=-1, keepdims=True)
        x2 = x2 * lax.rsqrt(ms + EPS) * gfin_ref[...]
    return x2


def _ffn_weight_specs(w1, w2, layer):
    return [_const_spec((1,) + w1.shape[1:], (layer, 0, 0)), _const_spec((1,) + w2.shape[1:], (layer, 0, 0))]


def _rc_out_kernel(x_ref, hf_ref, hb_ref, gg_ref, bg_ref, um_ref, up_ref, un_ref,
                   cbw_ref, wo_ref, mod_ref, gffn_ref, w1_ref, w2_ref, o_ref, *, nt, tm):
    i = pl.program_id(0)
    has_prev = (i % nt) > 0
    has_next = (i % nt) < nt - 1
    prev = jnp.where(has_prev, up_ref[...].astype(_F32), 0.0)
    nxt = jnp.where(has_next, un_ref[...].astype(_F32), 0.0)
    uext = jnp.concatenate([prev, um_ref[...].astype(_F32), nxt], axis=0)
    cbw = cbw_ref[...]
    sub = min(FFN_SUB_ROWS, tm)
    normed = []
    for r0 in range(0, tm, sub):
        rows = slice(r0, r0 + sub)
        ya = ((hf_ref[rows, :] + hb_ref[rows, :]) * gg_ref[rows, :].astype(_F32)).astype(_BF16)
        conv = None
        for k in range(3):
            first = HALO_BF16 - 1 + k + r0
            term = cbw[k:k + 1, :] * uext[first:first + sub, :]
            conv = term if conv is None else conv + term
        yb = (bg_ref[rows, :].astype(_F32) * conv).astype(_BF16)
        y = _dot(ya, wo_ref[0:D_LRU, :]) + _dot(yb, wo_ref[D_LRU:D_LRU + D_SC, :])
        normed.append(_ffn_norm(x_ref[rows, :], y, mod_ref, gffn_ref))
    for k, r0 in enumerate(range(0, tm, sub)):
        x1, h = normed[k]
        o_ref[r0:r0 + sub, :] = _ffn_apply(x1, h, mod_ref, w1_ref, w2_ref, None)


def _rc_out(x2d, hf, hb, p, conv_b_w, w_out, mod, gffn, w1, w2, layer, tm, seq, row_of_tile):
    n, d = x2d.shape
    nt = seq // tm
    tb = tm // HALO_BF16
    last = n // HALO_BF16 - 1

    def col(c):
        return pl.BlockSpec((tm, D_SC), lambda i: (i, c))

    kernel = functools.partial(_rc_out_kernel, nt=nt, tm=tm)
    return pl.pallas_call(
        kernel,
        grid=(n // tm,),
        in_specs=[
            pl.BlockSpec((tm, d), lambda i: (i, 0)),
            pl.BlockSpec((tm, D_LRU), lambda i: (i, 0)),
            pl.BlockSpec((tm, D_LRU), lambda i: (i, 0)),
            col(0), col(1), col(2),
            pl.BlockSpec((HALO_BF16, D_SC), lambda i: (jnp.maximum(i * tb - 1, 0), 2)),
            pl.BlockSpec((HALO_BF16, D_SC), lambda i: (jnp.minimum((i + 1) * tb, last), 2)),
            _const_spec((3, D_SC)),
            _const_spec(w_out.shape),
            pl.BlockSpec((1, N_MOD, d), lambda i: (row_of_tile(i), 0, 0)),
            _const_spec((1, d)),
        ] + _ffn_weight_specs(w1, w2, layer),
        out_specs=pl.BlockSpec((tm, d), lambda i: (i, 0)),
        out_shape=jax.ShapeDtypeStruct((n, d), _F32),
        compiler_params=_params(("parallel",)),
        name="rc_out_ffn",
    )(x2d, hf, hb, p, p, p, p, p, conv_b_w, w_out, mod, gffn, w1, w2)


def _attn_out_kernel(x_ref, o_in_ref, wo_ref, mod_ref, gffn_ref, gfin_ref, w1_ref, w2_ref, o_ref):
    tm = x_ref.shape[0]
    sub = min(FFN_SUB_ROWS, tm)
    normed = []
    for r0 in range(0, tm, sub):
        rows = slice(r0, r0 + sub)
        o_in = jnp.concatenate([o_in_ref[c, rows, :] for c in range(o_in_ref.shape[0])], axis=1)
        normed.append(_ffn_norm(x_ref[rows, :], _dot(o_in, wo_ref[...]), mod_ref, gffn_ref))
    for k, r0 in enumerate(range(0, tm, sub)):
        x1, h = normed[k]
        o_ref[r0:r0 + sub, :] = _ffn_apply(x1, h, mod_ref, w1_ref, w2_ref, gfin_ref)


def _attn_out(x2d, o, w_out, mod, gffn, w1, w2, layer, gfin, tm, row_of_tile):
    n, d = x2d.shape
    return pl.pallas_call(
        _attn_out_kernel,
        grid=(n // tm,),
        in_specs=[
            pl.BlockSpec((tm, d), lambda i: (i, 0)),
            pl.BlockSpec((o.shape[0], tm, V7X_LANES), lambda i: (0, i, 0)),
            _const_spec(w_out.shape),
            pl.BlockSpec((1, N_MOD, d), lambda i: (row_of_tile(i), 0, 0)),
            _const_spec((1, d)),
            _const_spec((1, d)),
        ] + _ffn_weight_specs(w1, w2, layer),
        out_specs=pl.BlockSpec((tm, d), lambda i: (i, 0)),
        out_shape=jax.ShapeDtypeStruct((n, d), _F32),
        compiler_params=_params(("parallel",)),
        name="attn_out_ffn",
    )(x2d, o, w_out, mod, gffn, gfin, w1, w2)


def _rope(x, cos, sin_signed, lo_half):
    partner = jnp.where(lo_half, pltpu.roll(x, V7X_LANES - ROPE_FREQS, 1), pltpu.roll(x, ROPE_FREQS, 1))
    return x * cos + partner * sin_signed


def _rope_lane_tables(cs):
    lane = lax.broadcasted_iota(jnp.int32, cs.shape, 1)
    in_head = lane % HEAD_DIM
    f = ROPE_FREQS
    back = lambda k: pltpu.roll(cs, k, 1)
    fwd = lambda k: pltpu.roll(cs, V7X_LANES - k, 1)
    cos_h = jnp.where(in_head < f, cs, jnp.where(in_head < 3 * f, back(f), back(2 * f)))
    sin_h = jnp.where(in_head < f, fwd(2 * f), jnp.where(in_head < 3 * f, fwd(f), cs))
    first = lane < HEAD_DIM
    cos = jnp.where(first, cos_h, pltpu.roll(cos_h, HEAD_DIM, 1))
    sin = jnp.where(first, sin_h, pltpu.roll(sin_h, HEAD_DIM, 1))
    lo_half = (lane % (2 * f)) < f
    return cos, jnp.where(lo_half, -sin, sin), lo_half


def _heads_to_slabs(x, ref, c):
    first = lax.broadcasted_iota(jnp.int32, x.shape, 1) < HEAD_DIM
    swapped = pltpu.roll(x, HEAD_DIM, 1)
    ref[2 * c] = jnp.where(first, x, swapped).astype(_BF16)
    ref[2 * c + 1] = jnp.where(first, swapped, x).astype(_BF16)


def _qkv_kernel(x_ref, mod_ref, g_ref, w_ref, tab_ref, q_ref, k_ref, v_ref, *, nt, tm):
    h = _rms_mod(x_ref[...], g_ref[...], mod_ref[0, 0:1, :], mod_ref[0, 1:2, :]).astype(_BF16)
    col_terms = tab_ref[0:GRID_W, :]
    first_grid_row = GRID_W + (pl.program_id(0) % nt) * (tm // GRID_W)
    cs = jnp.concatenate([col_terms + tab_ref[pl.ds(first_grid_row + g, 1), :] for g in range(tm // GRID_W)],
                         axis=0)
    cos, sin, lo_half = _rope_lane_tables(cs)
    dq = N_Q_HEADS * HEAD_DIM
    dkv = N_KV_HEADS * HEAD_DIM
    scale = HEAD_DIM ** -0.5 * LOG2E
    q = _dot(h, w_ref[:, 0:dq])
    for c in range(dq // V7X_LANES):
        cols = slice(c * V7X_LANES, (c + 1) * V7X_LANES)
        q_ref[c] = (_rope(q[:, cols], cos, sin, lo_half) * scale).astype(_BF16)
    k = _dot(h, w_ref[:, dq:dq + dkv])
    v = _dot(h, w_ref[:, dq + dkv:dq + 2 * dkv])
    for c in range(dkv // V7X_LANES):
        cols = slice(c * V7X_LANES, (c + 1) * V7X_LANES)
        _heads_to_slabs(_rope(k[:, cols], cos, sin, lo_half), k_ref, c)
        _heads_to_slabs(v[:, cols], v_ref, c)


def _qkv(x2d, mod, gain, w, cs_t, tm, seq, row_of_tile):
    n, d = x2d.shape
    nt = seq // tm
    nq_slab = N_Q_HEADS * HEAD_DIM // V7X_LANES
    assert tm % GRID_W == 0
    return pl.pallas_call(
        functools.partial(_qkv_kernel, nt=nt, tm=tm),
        grid=(n // tm,),
        in_specs=[
            pl.BlockSpec((tm, d), lambda i: (i, 0)),
            pl.BlockSpec((1, N_MOD, d), lambda i: (row_of_tile(i), 0, 0)),
            _const_spec((1, d)),
            _const_spec(w.shape),
            _const_spec(cs_t.shape),
        ],
        out_specs=[
            pl.BlockSpec((nq_slab, tm, V7X_LANES), lambda i: (0, i, 0)),
            pl.BlockSpec((N_KV_HEADS, tm, V7X_LANES), lambda i: (0, i, 0)),
            pl.BlockSpec((N_KV_HEADS, tm, V7X_LANES), lambda i: (0, i, 0)),
        ],
        out_shape=[
            jax.ShapeDtypeStruct((nq_slab, n, V7X_LANES), _BF16),
            jax.ShapeDtypeStruct((N_KV_HEADS, n, V7X_LANES), _BF16),
            jax.ShapeDtypeStruct((N_KV_HEADS, n, V7X_LANES), _BF16),
        ],
        compiler_params=_params(("parallel",)),
        name="qkv_rope",
    )(x2d, mod, gain, w, cs_t)


def _kv_kernel(x_ref, mod_ref, g_ref, w_ref, k_ref, v_ref):
    h = _rms_mod(x_ref[...], g_ref[...], mod_ref[0, 0:1, :], mod_ref[0, 1:2, :]).astype(_BF16)
    dq = N_Q_HEADS * HEAD_DIM
    dkv = N_KV_HEADS * HEAD_DIM
    k = _dot(h, w_ref[:, dq:dq + dkv])
    v = _dot(h, w_ref[:, dq + dkv:dq + 2 * dkv])
    for c in range(dkv // V7X_LANES):
        cols = slice(c * V7X_LANES, (c + 1) * V7X_LANES)
        _heads_to_slabs(k[:, cols], k_ref, c)
        _heads_to_slabs(v[:, cols], v_ref, c)


def _ctx_kv(x2d, mod, gain, w):
    n, d = x2d.shape
    slab = (N_KV_HEADS, n, V7X_LANES)
    return pl.pallas_call(
        _kv_kernel,
        grid=(1,),
        in_specs=[
            pl.BlockSpec((n, d), lambda i: (0, 0)),
            pl.BlockSpec((1, N_MOD, d), lambda i: (0, 0, 0)),
            _const_spec((1, d)),
            _const_spec(w.shape),
        ],
        out_specs=[pl.BlockSpec(slab, lambda i: (0, 0, 0)), pl.BlockSpec(slab, lambda i: (0, 0, 0))],
        out_shape=[jax.ShapeDtypeStruct(slab, _BF16), jax.ShapeDtypeStruct(slab, _BF16)],
        compiler_params=_params(("arbitrary",)),
        name="ctx_kv",
    )(x2d, mod, gain, w)


def _attn_kernel(sink_ref, q_ref, kp_ref, kc_ref, kn_ref, vp_ref, vc_ref, vn_ref, kx_ref, vx_ref, o_ref,
                 kext, vext, s_even, s_odd, *, tq, nq):
    n = pl.program_id(1)
    blk = WINDOW
    nsub = tq // blk
    nblocks = nsub * N_KV_HEADS
    ctx_len = kx_ref.shape[1]
    kext[:, 0:blk, :] = kp_ref[...]
    kext[:, blk:blk + tq, :] = kc_ref[...]
    kext[:, blk + tq:2 * blk + tq, :] = kn_ref[...]
    vext[:, 0:blk, :] = vp_ref[...]
    vext[:, blk:blk + tq, :] = vc_ref[...]
    vext[:, blk + tq:2 * blk + tq, :] = vn_ref[...]
    qi = lax.broadcasted_iota(jnp.int32, (blk, blk), 0)
    kj = lax.broadcasted_iota(jnp.int32, (blk, blk), 1)
    lane = lax.broadcasted_iota(jnp.int32, (blk, V7X_LANES), 1)
    lo = lane < HEAD_DIM
    ones_win = jnp.ones((3 * blk, V7X_LANES), _BF16)
    ones_ctx = jnp.ones((ctx_len, V7X_LANES), _BF16)

    def locate(i):
        j = i // N_KV_HEADS
        return j, i % N_KV_HEADS, j * blk

    def scores(i, s_ref):
        _, hkv, row0 = locate(i)
        parts = []
        for c in range(2):
            qc = q_ref[2 * hkv + c, pl.ds(row0, blk), :]
            parts.append(jnp.where(lo, qc, jnp.zeros_like(qc)))
            parts.append(jnp.where(lo, jnp.zeros_like(qc), qc))
        lhs = jnp.concatenate(parts, axis=0)
        s_ref[:, 0:3 * blk] = _dot_t(lhs, kext[hkv, pl.ds(row0, 3 * blk), :])
        s_ref[:, 3 * blk:3 * blk + ctx_len] = _dot_t(lhs, kx_ref[hkv])

    def softmax_pv(i, s_ref):
        j, hkv, row0 = locate(i)
        keep_prev = kj >= qi
        keep_next = kj <= qi
        if j == 0:
            keep_prev = keep_prev & (n > 0)
        if j == nsub - 1:
            keep_next = keep_next & (n < nq - 1)
        vwin = jnp.concatenate([vext[hkv, pl.ds(row0, 3 * blk), :], ones_win], axis=1)
        vx = jnp.concatenate([vx_ref[hkv], ones_ctx], axis=1)
        p_win, p_ctx, esink = [], [], []
        for g in range(GQA_GROUP):
            sink = sink_ref[hkv * GQA_GROUP + g] * LOG2E
            grow = slice(g * blk, (g + 1) * blk)
            s0 = jnp.where(keep_prev, s_ref[grow, 0:blk], NEG)
            s1 = s_ref[grow, blk:2 * blk]
            s2 = jnp.where(keep_next, s_ref[grow, 2 * blk:3 * blk], NEG)
            sc = [s_ref[grow, 3 * blk + t * V7X_LANES:3 * blk + (t + 1) * V7X_LANES]
                  for t in range(ctx_len // V7X_LANES)]
            mm = jnp.maximum(jnp.maximum(s0, s1), s2)
            for t in sc:
                mm = jnp.maximum(mm, t)
            m = jnp.maximum(jnp.max(mm, axis=-1, keepdims=True), sink)
            p_win.append(jnp.concatenate([jnp.exp2(t - m).astype(_BF16) for t in (s0, s1, s2)], axis=1))
            p_ctx.append(jnp.concatenate([jnp.exp2(t - m).astype(_BF16) for t in sc], axis=1))
            esink.append(jnp.exp2(sink - m))
        o = _dot(jnp.concatenate(p_win, axis=0), vwin) + _dot(jnp.concatenate(p_ctx, axis=0), vx)
        og = []
        for g in range(GQA_GROUP):
            grow = slice(g * blk, (g + 1) * blk)
            den = o[grow, V7X_LANES:2 * V7X_LANES] + esink[g]
            og.append(o[grow, 0:V7X_LANES] * (1.0 / den))
        for c in range(2):
            ocol = jnp.where(lo, og[2 * c], og[2 * c + 1])
            o_ref[2 * hkv + c, pl.ds(row0, blk), :] = ocol.astype(_BF16)

    s_bufs = (s_even, s_odd)
    scores(0, s_bufs[0])
    for i in range(nblocks):
        if i + 1 < nblocks:
            scores(i + 1, s_bufs[(i + 1) % 2])
        softmax_pv(i, s_bufs[i % 2])


def _attention(sink, q, kd, vd, kx, vx, batch, seq, ctx_len, tq):
    nq_slab, n, _ = q.shape
    nq = seq // tq
    hb = tq // WINDOW
    sb = seq // WINDOW
    last = n // WINDOW - 1
    kv = N_KV_HEADS

    def cur():
        return pl.BlockSpec((kv, tq, V7X_LANES), lambda b, i: (0, b * nq + i, 0))

    def prev():
        return pl.BlockSpec((kv, WINDOW, V7X_LANES), lambda b, i: (0, jnp.maximum(b * sb + i * hb - 1, 0), 0))

    def nxt():
        return pl.BlockSpec((kv, WINDOW, V7X_LANES), lambda b, i: (0, jnp.minimum(b * sb + (i + 1) * hb, last), 0))

    def ctx():
        return pl.BlockSpec((kv, ctx_len, V7X_LANES), lambda b, i: (0, b, 0))

    kernel = functools.partial(_attn_kernel, tq=tq, nq=nq)
    return pl.pallas_call(
        kernel,
        grid=(batch, nq),
        in_specs=[
            pl.BlockSpec(memory_space=pltpu.SMEM),
            pl.BlockSpec((nq_slab, tq, V7X_LANES), lambda b, i: (0, b * nq + i, 0)),
            prev(), cur(), nxt(), prev(), cur(), nxt(), ctx(), ctx(),
        ],
        out_specs=pl.BlockSpec((nq_slab, tq, V7X_LANES), lambda b, i: (0, b * nq + i, 0)),
        out_shape=jax.ShapeDtypeStruct(q.shape, _BF16),
        scratch_shapes=[
            pltpu.VMEM((kv, tq + 2 * WINDOW, V7X_LANES), _BF16),
            pltpu.VMEM((kv, tq + 2 * WINDOW, V7X_LANES), _BF16),
            pltpu.VMEM((GQA_GROUP * WINDOW, 3 * WINDOW + ctx_len), _F32),
            pltpu.VMEM((GQA_GROUP * WINDOW, 3 * WINDOW + ctx_len), _F32),
        ],
        compiler_params=_params(("parallel", "parallel")),
        name="band_attn",
    )(sink, q, kd, kd, kd, vd, vd, vd, kx, vx)


def _gate_weights(r_w, r_b, i_w, i_b):
    heads_per_half = V7X_MXU_DIM // LRU_HEAD_DIM
    eye = jnp.eye(heads_per_half, dtype=_F32)

    def halves(w):
        w = w.reshape(2, D_LRU // V7X_MXU_DIM, heads_per_half, LRU_HEAD_DIM, LRU_HEAD_DIM)
        bd = jnp.einsum('dxhij,hk->dxhikj', w, eye)
        return bd.reshape(2, D_LRU // V7X_MXU_DIM, V7X_MXU_DIM, V7X_MXU_DIM)

    w = (0.5 * jnp.concatenate([halves(r_w), halves(i_w)], axis=-1)).astype(_BF16)
    rb = r_b.reshape(2, D_LRU // V7X_MXU_DIM, 1, V7X_MXU_DIM)
    ib = i_b.reshape(2, D_LRU // V7X_MXU_DIM, 1, V7X_MXU_DIM)
    return w, 0.5 * jnp.concatenate([rb, ib], axis=-1)


def _rope_table(seq):
    grid_rows = seq // GRID_W
    inv_freq = ROPE_BASE ** (-jnp.arange(ROPE_FREQS, dtype=_F32) / ROPE_FREQS)
    pos = jnp.concatenate([jnp.arange(GRID_W), jnp.arange(grid_rows)]).astype(_F32)
    ang = pos[:, None] * inv_freq
    cos, sin = jnp.cos(ang), jnp.sin(ang)
    is_col = (jnp.arange(GRID_W + grid_rows) < GRID_W)[:, None]
    zero = jnp.zeros_like(cos)
    table = jnp.concatenate([jnp.where(is_col, zero, cos), jnp.where(is_col, cos, zero),
                             jnp.where(is_col, zero, sin), jnp.where(is_col, sin, zero)], axis=1)
    return jnp.pad(table, ((0, 0), (0, V7X_LANES - 4 * ROPE_FREQS)))


def kernel(x, c, ctx, c_ctx, ada_w, ada_b, norm_mix_g, norm_ffn_g, norm_final_g, ffn_w_in, ffn_w_out,
           rc_w_in, rc_conv_a_w, rc_conv_a_b, rc_gate_r_w, rc_gate_r_b, rc_gate_i_w, rc_gate_i_b,
           rc_lambda, rc_conv_b_w, rc_w_out, at_w_qkv, at_sink, at_w_out):
    batch, seq, d = x.shape
    ctx_len = ctx.shape[1]
    tm = 1024
    tm_rc = 512
    tm_ffn = 1024
    tq = 1024
    tc = 1024
    assert seq % tm_ffn == 0 and seq % tm_rc == 0 and seq % tm == 0 and seq % tq == 0 and seq % tc == 0
    assert ctx_len % V7X_LANES == 0

    xl = x.reshape(batch * seq, d)
    xc = ctx.reshape(batch * ctx_len, d)

    cond = jnp.concatenate([c_ctx[None], c, jnp.zeros((8 - 1 - batch, d), _F32)], axis=0)
    mod = _modulation(cond, ada_w, ada_b).reshape(ada_w.shape[0], 8, N_MOD, d)

    def lat_row(tile):
        return lambda i: 1 + i // (seq // tile)

    ctx_row = lambda i: 0

    w1 = ffn_w_in.astype(_BF16)
    w2 = ffn_w_out.astype(_BF16)

    w_in = rc_w_in[0].astype(_BF16)
    w_gate, b_gate = _gate_weights(rc_gate_r_w[0], rc_gate_r_b[0], rc_gate_i_w[0], rc_gate_i_b[0])
    lam = rc_lambda[0].reshape(2, 1, D_LRU)
    conv_a_b = rc_conv_a_b[0].reshape(1, D_LRU)
    w_out0 = rc_w_out[0].astype(_BF16)
    gmix0 = norm_mix_g[0].reshape(1, d)
    gffn0 = norm_ffn_g[0].reshape(1, d)

    xa_c, p_c = _project(xc, mod[0], gmix0, w_in, batch * ctx_len, ctx_row)
    xa_l, p_l = _project(xl, mod[0], gmix0, w_in, tm, lat_row(tm))

    h0 = jnp.zeros((batch, 2, V7X_SUBLANES, D_LRU), _F32)
    hf_c, hb_c, h_ctx = _scan(xa_c, h0, rc_conv_a_w[0], conv_a_b, w_gate, b_gate, lam, batch, ctx_len, ctx_len)
    hf_l, hb_l, _ = _scan(xa_l, h_ctx, rc_conv_a_w[0], conv_a_b, w_gate, b_gate, lam, batch, seq, tc)

    xc = _rc_out(xc, hf_c, hb_c, p_c, rc_conv_b_w[0], w_out0, mod[0], gffn0, w1, w2, 0,
                 ctx_len, ctx_len, ctx_row)
    xl = _rc_out(xl, hf_l, hb_l, p_l, rc_conv_b_w[0], w_out0, mod[0], gffn0, w1, w2, 0,
                 tm_rc, seq, lat_row(tm_rc))

    w_qkv = at_w_qkv[0].astype(_BF16)
    gmix1 = norm_mix_g[1].reshape(1, d)
    gffn1 = norm_ffn_g[1].reshape(1, d)

    q, kd, vd = _qkv(xl, mod[1], gmix1, w_qkv, _rope_table(seq), tm, seq, lat_row(tm))
    kx, vx = _ctx_kv(xc, mod[1], gmix1, w_qkv)
    o = _attention(at_sink[0], q, kd, vd, kx, vx, batch, seq, ctx_len, tq)

    out = _attn_out(xl, o, at_w_out[0].astype(_BF16), mod[1], gffn1, w1, w2, 1,
                    norm_final_g.reshape(1, d), tm_ffn, lat_row(tm_ffn))
    return out.reshape(batch, seq, d)
```

```python
import functools

import jax
import jax.numpy as jnp
from jax import lax
from jax.experimental import pallas as pl
from jax.experimental.pallas import tpu as pltpu

D_MODEL = 1024
N_MOD = 6
EPS = 1e-6
NEG = -1e30
D_LRU = 512
D_SC = 512
LRU_HEADS = 8
LRU_HEAD_DIM = 64
LRU_C = 8.0
RC_IN_WIDTH = 2 * D_LRU + 3 * D_SC
HEAD_DIM = 64
N_Q_HEADS = 16
N_KV_HEADS = 4
GQA_GROUP = 4
WINDOW = 128
GRID_W = 64
ROPE_BASE = 10000.0
ROPE_FREQS = 16
D_FF = 2816
LOG2E = 1.4426950408889634

V7X_LANES = 128
V7X_SUBLANES = 8
V7X_MXU_DIM = 256
V7X_VMEM_LIMIT = 60 * 1024 * 1024

HALO = V7X_SUBLANES
HALO_BF16 = 2 * V7X_SUBLANES
N_SLAB = D_LRU // V7X_LANES
SCAN_SHIFTS = (1, 2, 4)
FFN_CHUNKS = ((0, 1024), (1024, 1024), (2048, 768))
FFN_SUB_ROWS = 512

_BF16 = jnp.bfloat16
_F32 = jnp.float32


def _dot(a, b):
    return jnp.dot(a, b, preferred_element_type=_F32)


def _dot_t(a, b):
    return lax.dot_general(a, b, (((1,), (1,)), ((), ())), preferred_element_type=_F32)


def _sigmoid(x):
    return 0.5 * jnp.tanh(0.5 * x) + 0.5


def _gelu_tanh(x):
    return 0.5 * x * (1.0 + jnp.tanh(0.7978845608028654 * (x + 0.044715 * (x * x * x))))


def _rms_mod(x, gain, shift, scale):
    ms = jnp.mean(x * x, axis=-1, keepdims=True)
    return (x * lax.rsqrt(ms + EPS) * gain) * (1.0 + scale) + shift


def _const_spec(shape, index=None):
    idx = (0,) * len(shape) if index is None else index
    return pl.BlockSpec(shape, lambda *_: idx, pipeline_mode=pl.Buffered(1))


def _params(sem):
    return pltpu.CompilerParams(dimension_semantics=sem, vmem_limit_bytes=V7X_VMEM_LIMIT)


def _mod_kernel(c_ref, w_ref, b_ref, o_ref):
    c = c_ref[...]
    s = (c * _sigmoid(c)).astype(_BF16)
    o_ref[0] = _dot(s, w_ref[0].astype(_BF16)) + b_ref[0]


def _modulation(cond, ada_w, ada_b):
    depth, d, n = ada_w.shape
    bn = 1536
    return pl.pallas_call(
        _mod_kernel,
        grid=(depth, n // bn),
        in_specs=[
            pl.BlockSpec((8, d), lambda l, j: (0, 0)),
            pl.BlockSpec((1, d, bn), lambda l, j: (l, 0, j)),
            pl.BlockSpec((1, 1, bn), lambda l, j: (l, 0, j)),
        ],
        out_specs=pl.BlockSpec((1, 8, bn), lambda l, j: (l, 0, j)),
        out_shape=jax.ShapeDtypeStruct((depth, 8, n), _F32),
        compiler_params=_params(("arbitrary", "arbitrary")),
        name="adaln_mod",
    )(cond, ada_w, ada_b.reshape(depth, 1, n))


def _proj_kernel(x_ref, mod_ref, g_ref, w_ref, xa_ref, rest_ref):
    h = _rms_mod(x_ref[...], g_ref[...], mod_ref[0, 0:1, :], mod_ref[0, 1:2, :]).astype(_BF16)
    xa = _dot(h, w_ref[:, 0:D_LRU])
    for c in range(N_SLAB):
        xa_ref[c] = xa[:, c * V7X_LANES:(c + 1) * V7X_LANES]
    ga = _dot(h, w_ref[:, D_LRU:2 * D_LRU])
    rest_ref[:, 0:D_LRU] = _gelu_tanh(ga).astype(_BF16)
    rest_ref[:, D_LRU:D_LRU + D_SC] = _dot(h, w_ref[:, 2 * D_LRU:2 * D_LRU + D_SC]).astype(_BF16)
    cv = _dot(h, w_ref[:, 2 * D_LRU + D_SC:2 * D_LRU + 3 * D_SC])
    rest_ref[:, D_LRU + D_SC:D_LRU + 2 * D_SC] = (cv[:, 0:D_SC] * cv[:, D_SC:2 * D_SC]).astype(_BF16)


def _project(x2d, mod, gain, w, tm, row_of_tile):
    n, d = x2d.shape
    nrest = D_LRU + 2 * D_SC
    return pl.pallas_call(
        _proj_kernel,
        grid=(n // tm,),
        in_specs=[
            pl.BlockSpec((tm, d), lambda i: (i, 0)),
            pl.BlockSpec((1, N_MOD, d), lambda i: (row_of_tile(i), 0, 0)),
            _const_spec((1, d)),
            _const_spec(w.shape),
        ],
        out_specs=[
            pl.BlockSpec((N_SLAB, tm, V7X_LANES), lambda i: (0, i, 0)),
            pl.BlockSpec((tm, nrest), lambda i: (i, 0)),
        ],
        out_shape=[
            jax.ShapeDtypeStruct((N_SLAB, n, V7X_LANES), _F32),
            jax.ShapeDtypeStruct((n, nrest), _BF16),
        ],
        compiler_params=_params(("parallel",)),
        name="rc_in_proj",
    )(x2d, mod, gain, w)


def _scan_kernel(*refs, nc, tc, ncast):
    (xm_f, xp_f, xn_f, xm_b, xp_b, xn_b, cw_ref, cb_ref, wg_ref, bg_ref, lam_ref, h0_ref) = refs[:12]
    cast_in = refs[12:12 + ncast]
    hf_ref, hb_ref, hl_ref = refs[12 + ncast:15 + ncast]
    cast_out = refs[15 + ncast:15 + 2 * ncast]
    xs_scr, carry_scr = refs[15 + 2 * ncast:17 + 2 * ncast]
    level_scr = refs[17 + 2 * ncast:]
    for src, dst in zip(cast_in, cast_out):
        dst[...] = src[...].astype(_BF16)
    j = pl.program_id(1)
    lo = HALO
    ngroups = tc // V7X_SUBLANES
    nlev = len(SCAN_SHIFTS)
    a_lv = list(level_scr[:nlev])
    b_lv = list(level_scr[nlev:])
    pad_rows = (slice(0, HALO), slice(lo + tc, lo + tc + HALO))

    @pl.when(j == 0)
    def _():
        carry_scr[...] = h0_ref[0]
        for d in range(2):
            for ref in a_lv:
                ref[d, :, pad_rows[d], :] = jnp.ones((N_SLAB, HALO, V7X_LANES), _F32)
            for ref in b_lv:
                ref[d, :, pad_rows[d], :] = jnp.zeros((N_SLAB, HALO, V7X_LANES), _F32)

    cw = cw_ref[...]
    cb = cb_ref[...]

    def coeffs(d, xm, xp, xn, has_prev, has_next):
        xs_scr[d, :, 0:HALO, :] = jnp.where(has_prev, xp[...], 0.0)
        xs_scr[d, :, lo:lo + tc, :] = xm[...]
        xs_scr[d, :, lo + tc:lo + tc + HALO, :] = jnp.where(has_next, xn[...], 0.0)
        xcs = []
        for c in range(N_SLAB):
            lanes = slice(c * V7X_LANES, (c + 1) * V7X_LANES)
            acc = cb[:, lanes]
            for k in range(4):
                acc = acc + cw[k:k + 1, lanes] * xs_scr[d, c, lo - 2 + k:lo - 2 + k + tc, :]
            xcs.append(acc)
        lam = lam_ref[d]
        nlam = -lam
        softplus = jnp.maximum(nlam, 0.0) + jnp.log(1.0 + jnp.exp(-jnp.abs(nlam)))
        hrate = (-0.5 * LRU_C * LOG2E) * softplus
        half = V7X_MXU_DIM
        per_half = half // V7X_LANES
        for hh in range(D_LRU // half):
            xch = jnp.concatenate(xcs[hh * per_half:(hh + 1) * per_half], axis=1)
            z = _dot(xch.astype(_BF16), wg_ref[d, hh]) + bg_ref[d, hh]
            for cc in range(per_half):
                c = hh * per_half + cc
                lanes = slice(c * V7X_LANES, (c + 1) * V7X_LANES)
                tr = jnp.tanh(z[:, cc * V7X_LANES:(cc + 1) * V7X_LANES])
                ig = 0.5 * jnp.tanh(z[:, half + cc * V7X_LANES:half + (cc + 1) * V7X_LANES]) + 0.5
                a = jnp.exp2(tr * hrate[:, lanes] + hrate[:, lanes])
                one_m_a2 = jnp.maximum(1.0 - a * a, 1e-12)
                mult = one_m_a2 * lax.rsqrt(one_m_a2)
                a_lv[0][d, c, lo:lo + tc, :] = a
                b_lv[0][d, c, lo:lo + tc, :] = mult * ig * xcs[c]

    coeffs(0, xm_f, xp_f, xn_f, j > 0, j < nc - 1)
    coeffs(1, xm_b, xp_b, xn_b, j < nc - 1, j > 0)

    def step(g, carry):
        out = []
        for d, h_ref in ((0, hf_ref), (1, hb_ref)):
            grp = g if d == 0 else ngroups - 1 - g
            row0 = pl.multiple_of(lo + grp * V7X_SUBLANES, V7X_SUBLANES)
            cur = pl.ds(row0, V7X_SUBLANES)
            for c in range(N_SLAB):
                a = a_lv[0][d, c, cur, :]
                b = b_lv[0][d, c, cur, :]
                for lvl, shift in enumerate(SCAN_SHIFTS):
                    sh = pl.ds(row0 + (shift if d else -shift), V7X_SUBLANES)
                    b = a * b_lv[lvl][d, c, sh, :] + b
                    a = a * a_lv[lvl][d, c, sh, :]
                    if lvl + 1 < nlev:
                        a_lv[lvl + 1][d, c, cur, :] = a
                        b_lv[lvl + 1][d, c, cur, :] = b
                h = a * carry[d * N_SLAB + c] + b
                out_rows = pl.ds(pl.multiple_of(grp * V7X_SUBLANES, V7X_SUBLANES), V7X_SUBLANES)
                h_ref[out_rows, c * V7X_LANES:(c + 1) * V7X_LANES] = h
                out.append(h)
        return tuple(out)

    init = tuple(carry_scr[d, :, c * V7X_LANES:(c + 1) * V7X_LANES]
                 for d in range(2) for c in range(N_SLAB))
    carry = lax.fori_loop(0, ngroups, step, init, unroll=4)

    edge_row = (V7X_SUBLANES - 1, 0)
    data_edge = (slice(tc, tc + HALO), slice(lo, lo + HALO))
    for d in range(2):
        for c in range(N_SLAB):
            lanes = slice(c * V7X_LANES, (c + 1) * V7X_LANES)
            h = carry[d * N_SLAB + c]
            carry_scr[d, :, lanes] = h
            hl_ref[0, d, :, lanes] = jnp.broadcast_to(h[edge_row[d]:edge_row[d] + 1, :], h.shape)
        for ref in a_lv + b_lv:
            ref[d, :, pad_rows[d], :] = ref[d, :, data_edge[d], :]


def _scan(xa, h0, conv_w, conv_b, w_gate, b_gate, lam, batch, seq, tc, casts=()):
    n = xa.shape[1]
    nc = seq // tc
    tb = tc // HALO
    sb = seq // HALO
    last = n // HALO - 1

    def fwd(b, j):
        return j

    def bwd(b, j):
        return nc - 1 - j

    def main(cf):
        return pl.BlockSpec((N_SLAB, tc, V7X_LANES), lambda b, j: (0, b * nc + cf(b, j), 0))

    def prev(cf):
        return pl.BlockSpec((N_SLAB, HALO, V7X_LANES),
                            lambda b, j: (0, jnp.maximum(b * sb + cf(b, j) * tb - 1, 0), 0))

    def nxt(cf):
        return pl.BlockSpec((N_SLAB, HALO, V7X_LANES),
                            lambda b, j: (0, jnp.minimum(b * sb + (cf(b, j) + 1) * tb, last), 0))

    steps = batch * nc
    for w in casts:
        assert w.shape[0] % (steps * HALO_BF16) == 0, w.shape
    cast_specs = [pl.BlockSpec((w.shape[0] // steps, w.shape[1]), lambda b, j: (b * nc + j, 0)) for w in casts]
    kernel = functools.partial(_scan_kernel, nc=nc, tc=tc, ncast=len(casts))
    nlev = len(SCAN_SHIFTS)
    return pl.pallas_call(
        kernel,
        grid=(batch, nc),
        in_specs=[
            main(fwd), prev(fwd), nxt(fwd), main(bwd), prev(bwd), nxt(bwd),
            _const_spec((4, D_LRU)),
            _const_spec((1, D_LRU)),
            _const_spec(w_gate.shape),
            _const_spec(b_gate.shape),
            _const_spec((2, 1, D_LRU)),
            pl.BlockSpec((1, 2, V7X_SUBLANES, D_LRU), lambda b, j: (b, 0, 0, 0)),
        ] + cast_specs,
        out_specs=[
            pl.BlockSpec((tc, D_LRU), lambda b, j: (b * nc + j, 0)),
            pl.BlockSpec((tc, D_LRU), lambda b, j: (b * nc + nc - 1 - j, 0)),
            pl.BlockSpec((1, 2, V7X_SUBLANES, D_LRU), lambda b, j: (b, 0, 0, 0)),
        ] + cast_specs,
        out_shape=[
            jax.ShapeDtypeStruct((n, D_LRU), _F32),
            jax.ShapeDtypeStruct((n, D_LRU), _F32),
            jax.ShapeDtypeStruct((batch, 2, V7X_SUBLANES, D_LRU), _F32),
        ] + [jax.ShapeDtypeStruct(w.shape, _BF16) for w in casts],
        scratch_shapes=[
            pltpu.VMEM((2, N_SLAB, tc + 2 * HALO, V7X_LANES), _F32),
            pltpu.VMEM((2, V7X_SUBLANES, D_LRU), _F32),
        ] + [pltpu.VMEM((2, N_SLAB, tc + 2 * HALO, V7X_LANES), _F32) for _ in range(2 * nlev)],
        compiler_params=_params(("arbitrary", "arbitrary")),
        name="rglru_scan",
    )(xa, xa, xa, xa, xa, xa, conv_w, conv_b, w_gate, b_gate, lam, h0, *casts)


def _ffn_norm(x, y, mod_ref, gffn_ref):
    x1 = x + mod_ref[0, 2:3, :] * y
    h = _rms_mod(x1, gffn_ref[...], mod_ref[0, 3:4, :], mod_ref[0, 4:5, :]).astype(_BF16)
    return x1, h


def _ffn_apply(x1, h, mod_ref, w1_ref, w2_ref, gfin_ref):
    acc = None
    for c0, cw in FFN_CHUNKS:
        g = _dot(h, w1_ref[0, :, c0:c0 + cw])
        u = _dot(h, w1_ref[0, :, D_FF + c0:D_FF + c0 + cw])
        act = (g * _sigmoid(g) * u).astype(_BF16)
        part = _dot(act, w2_ref[0, c0:c0 + cw, :])
        acc = part if acc is None else acc + part
    x2 = x1 + mod_ref[0, 5:6, :] * acc
    if gfin_ref is not None:
        ms = jnp.mean(x2 * x2, axis=-1, keepdims=True)
        x2 = x2 * lax.rsqrt(ms + EPS) * gfin_ref[...]
    return x2


def _ffn_weight_specs(w1, w2, layer):
    return [_const_spec((1,) + w1.shape[1:], (layer, 0, 0)), _const_spec((1,) + w2.shape[1:], (layer, 0, 0))]


def _rc_out_kernel(x_ref, hf_ref, hb_ref, gg_ref, bg_ref, um_ref, up_ref, un_ref,
                   cbw_ref, wo_ref, mod_ref, gffn_ref, w1_ref, w2_ref, o_ref, *, nt, tm):
    i = pl.program_id(0)
    has_prev = (i % nt) > 0
    has_next = (i % nt) < nt - 1
    prev = jnp.where(has_prev, up_ref[...].astype(_F32), 0.0)
    nxt = jnp.where(has_next, un_ref[...].astype(_F32), 0.0)
    uext = jnp.concatenate([prev, um_ref[...].astype(_F32), nxt], axis=0)
    cbw = cbw_ref[...]
    sub = min(FFN_SUB_ROWS, tm)
    normed = []
    for r0 in range(0, tm, sub):
        rows = slice(r0, r0 + sub)
        ya = ((hf_ref[rows, :] + hb_ref[rows, :]) * gg_ref[rows, :].astype(_F32)).astype(_BF16)
        conv = None
        for k in range(3):
            first = HALO_BF16 - 1 + k + r0
            term = cbw[k:k + 1, :] * uext[first:first + sub, :]
            conv = term if conv is None else conv + term
        yb = (bg_ref[rows, :].astype(_F32) * conv).astype(_BF16)
        y = _dot(ya, wo_ref[0:D_LRU, :]) + _dot(yb, wo_ref[D_LRU:D_LRU + D_SC, :])
        normed.append(_ffn_norm(x_ref[rows, :], y, mod_ref, gffn_ref))
    for k, r0 in enumerate(range(0, tm, sub)):
        x1, h = normed[k]
        o_ref[r0:r0 + sub, :] = _ffn_apply(x1, h, mod_ref, w1_ref, w2_ref, None)


def _rc_out(x2d, hf, hb, p, conv_b_w, w_out, mod, gffn, w1, w2, layer, tm, seq, row_of_tile):
    n, d = x2d.shape
    nt = seq // tm
    tb = tm // HALO_BF16
    last = n // HALO_BF16 - 1

    def col(c):
        return pl.BlockSpec((tm, D_SC), lambda i: (i, c))

    kernel = functools.partial(_rc_out_kernel, nt=nt, tm=tm)
    return pl.pallas_call(
        kernel,
        grid=(n // tm,),
        in_specs=[
            pl.BlockSpec((tm, d), lambda i: (i, 0)),
            pl.BlockSpec((tm, D_LRU), lambda i: (i, 0)),
            pl.BlockSpec((tm, D_LRU), lambda i: (i, 0)),
            col(0), col(1), col(2),
            pl.BlockSpec((HALO_BF16, D_SC), lambda i: (jnp.maximum(i * tb - 1, 0), 2)),
            pl.BlockSpec((HALO_BF16, D_SC), lambda i: (jnp.minimum((i + 1) * tb, last), 2)),
            _const_spec((3, D_SC)),
            _const_spec(w_out.shape),
            pl.BlockSpec((1, N_MOD, d), lambda i: (row_of_tile(i), 0, 0)),
            _const_spec((1, d)),
        ] + _ffn_weight_specs(w1, w2, layer),
        out_specs=pl.BlockSpec((tm, d), lambda i: (i, 0)),
        out_shape=jax.ShapeDtypeStruct((n, d), _F32),
        compiler_params=_params(("parallel",)),
        name="rc_out_ffn",
    )(x2d, hf, hb, p, p, p, p, p, conv_b_w, w_out, mod, gffn, w1, w2)


def _attn_out_kernel(x_ref, o_in_ref, wo_ref, mod_ref, gffn_ref, gfin_ref, w1_ref, w2_ref, o_ref):
    tm = x_ref.shape[0]
    sub = min(FFN_SUB_ROWS, tm)
    normed = []
    for r0 in range(0, tm, sub):
        rows = slice(r0, r0 + sub)
        o_in = jnp.concatenate([o_in_ref[c, rows, :] for c in range(o_in_ref.shape[0])], axis=1)
        normed.append(_ffn_norm(x_ref[rows, :], _dot(o_in, wo_ref[...]), mod_ref, gffn_ref))
    for k, r0 in enumerate(range(0, tm, sub)):
        x1, h = normed[k]
        o_ref[r0:r0 + sub, :] = _ffn_apply(x1, h, mod_ref, w1_ref, w2_ref, gfin_ref)


def _attn_out(x2d, o, w_out, mod, gffn, w1, w2, layer, gfin, tm, row_of_tile):
    n, d = x2d.shape
    return pl.pallas_call(
        _attn_out_kernel,
        grid=(n // tm,),
        in_specs=[
            pl.BlockSpec((tm, d), lambda i: (i, 0)),
            pl.BlockSpec((o.shape[0], tm, V7X_LANES), lambda i: (0, i, 0)),
            _const_spec(w_out.shape),
            pl.BlockSpec((1, N_MOD, d), lambda i: (row_of_tile(i), 0, 0)),
            _const_spec((1, d)),
            _const_spec((1, d)),
        ] + _ffn_weight_specs(w1, w2, layer),
        out_specs=pl.BlockSpec((tm, d), lambda i: (i, 0)),
        out_shape=jax.ShapeDtypeStruct((n, d), _F32),
        compiler_params=_params(("parallel",)),
        name="attn_out_ffn",
    )(x2d, o, w_out, mod, gffn, gfin, w1, w2)


def _rope(x, cos, sin_signed, lo_half):
    partner = jnp.where(lo_half, pltpu.roll(x, V7X_LANES - ROPE_FREQS, 1), pltpu.roll(x, ROPE_FREQS, 1))
    return x * cos + partner * sin_signed


def _rope_lane_tables(cs):
    lane = lax.broadcasted_iota(jnp.int32, cs.shape, 1)
    in_head = lane % HEAD_DIM
    f = ROPE_FREQS
    back = lambda k: pltpu.roll(cs, k, 1)
    fwd = lambda k: pltpu.roll(cs, V7X_LANES - k, 1)
    cos_h = jnp.where(in_head < f, cs, jnp.where(in_head < 3 * f, back(f), back(2 * f)))
    sin_h = jnp.where(in_head < f, fwd(2 * f), jnp.where(in_head < 3 * f, fwd(f), cs))
    first = lane < HEAD_DIM
    cos = jnp.where(first, cos_h, pltpu.roll(cos_h, HEAD_DIM, 1))
    sin = jnp.where(first, sin_h, pltpu.roll(sin_h, HEAD_DIM, 1))
    lo_half = (lane % (2 * f)) < f
    return cos, jnp.where(lo_half, -sin, sin), lo_half


def _heads_to_slabs(x, ref, c):
    first = lax.broadcasted_iota(jnp.int32, x.shape, 1) < HEAD_DIM
    swapped = pltpu.roll(x, HEAD_DIM, 1)
    ref[2 * c] = jnp.where(first, x, swapped).astype(_BF16)
    ref[2 * c + 1] = jnp.where(first, swapped, x).astype(_BF16)


def _qkv_kernel(x_ref, mod_ref, g_ref, w_ref, tab_ref, q_ref, k_ref, v_ref, *, nt, tm):
    h = _rms_mod(x_ref[...], g_ref[...], mod_ref[0, 0:1, :], mod_ref[0, 1:2, :]).astype(_BF16)
    col_terms = tab_ref[0:GRID_W, :]
    first_grid_row = GRID_W + (pl.program_id(0) % nt) * (tm // GRID_W)
    cs = jnp.concatenate([col_terms + tab_ref[pl.ds(first_grid_row + g, 1), :] for g in range(tm // GRID_W)],
                         axis=0)
    cos, sin, lo_half = _rope_lane_tables(cs)
    dq = N_Q_HEADS * HEAD_DIM
    dkv = N_KV_HEADS * HEAD_DIM
    scale = HEAD_DIM ** -0.5 * LOG2E
    q = _dot(h, w_ref[:, 0:dq])
    for c in range(dq // V7X_LANES):
        cols = slice(c * V7X_LANES, (c + 1) * V7X_LANES)
        q_ref[c] = (_rope(q[:, cols], cos, sin, lo_half) * scale).astype(_BF16)
    k = _dot(h, w_ref[:, dq:dq + dkv])
    v = _dot(h, w_ref[:, dq + dkv:dq + 2 * dkv])
    for c in range(dkv // V7X_LANES):
        cols = slice(c * V7X_LANES, (c + 1) * V7X_LANES)
        _heads_to_slabs(_rope(k[:, cols], cos, sin, lo_half), k_ref, c)
        _heads_to_slabs(v[:, cols], v_ref, c)


def _qkv(x2d, mod, gain, w, cs_t, tm, seq, row_of_tile):
    n, d = x2d.shape
    nt = seq // tm
    nq_slab = N_Q_HEADS * HEAD_DIM // V7X_LANES
    assert tm % GRID_W == 0
    return pl.pallas_call(
        functools.partial(_qkv_kernel, nt=nt, tm=tm),
        grid=(n // tm,),
        in_specs=[
            pl.BlockSpec((tm, d), lambda i: (i, 0)),
            pl.BlockSpec((1, N_MOD, d), lambda i: (row_of_tile(i), 0, 0)),
            _const_spec((1, d)),
            _const_spec(w.shape),
            _const_spec(cs_t.shape),
        ],
        out_specs=[
            pl.BlockSpec((nq_slab, tm, V7X_LANES), lambda i: (0, i, 0)),
            pl.BlockSpec((N_KV_HEADS, tm, V7X_LANES), lambda i: (0, i, 0)),
            pl.BlockSpec((N_KV_HEADS, tm, V7X_LANES), lambda i: (0, i, 0)),
        ],
        out_shape=[
            jax.ShapeDtypeStruct((nq_slab, n, V7X_LANES), _BF16),
            jax.ShapeDtypeStruct((N_KV_HEADS, n, V7X_LANES), _BF16),
            jax.ShapeDtypeStruct((N_KV_HEADS, n, V7X_LANES), _BF16),
        ],
        compiler_params=_params(("parallel",)),
        name="qkv_rope",
    )(x2d, mod, gain, w, cs_t)


def _kv_kernel(x_ref, mod_ref, g_ref, w_ref, k_ref, v_ref):
    h = _rms_mod(x_ref[...], g_ref[...], mod_ref[0, 0:1, :], mod_ref[0, 1:2, :]).astype(_BF16)
    dq = N_Q_HEADS * HEAD_DIM
    dkv = N_KV_HEADS * HEAD_DIM
    k = _dot(h, w_ref[:, dq:dq + dkv])
    v = _dot(h, w_ref[:, dq + dkv:dq + 2 * dkv])
    for c in range(dkv // V7X_LANES):
        cols = slice(c * V7X_LANES, (c + 1) * V7X_LANES)
        _heads_to_slabs(k[:, cols], k_ref, c)
        _heads_to_slabs(v[:, cols], v_ref, c)


def _ctx_kv(x2d, mod, gain, w):
    n, d = x2d.shape
    slab = (N_KV_HEADS, n, V7X_LANES)
    return pl.pallas_call(
        _kv_kernel,
        grid=(1,),
        in_specs=[
            pl.BlockSpec((n, d), lambda i: (0, 0)),
            pl.BlockSpec((1, N_MOD, d), lambda i: (0, 0, 0)),
            _const_spec((1, d)),
            _const_spec(w.shape),
        ],
        out_specs=[pl.BlockSpec(slab, lambda i: (0, 0, 0)), pl.BlockSpec(slab, lambda i: (0, 0, 0))],
        out_shape=[jax.ShapeDtypeStruct(slab, _BF16), jax.ShapeDtypeStruct(slab, _BF16)],
        compiler_params=_params(("arbitrary",)),
        name="ctx_kv",
    )(x2d, mod, gain, w)


def _attn_kernel(sink_ref, q_ref, kp_ref, kc_ref, kn_ref, vp_ref, vc_ref, vn_ref, kx_ref, vx_ref, o_ref,
                 kext, vext, s_even, s_odd, *, tq, nq):
    n = pl.program_id(1)
    blk = WINDOW
    nsub = tq // blk
    nblocks = nsub * N_KV_HEADS
    ctx_len = kx_ref.shape[1]
    kext[:, 0:blk, :] = kp_ref[...]
    kext[:, blk:blk + tq, :] = kc_ref[...]
    kext[:, blk + tq:2 * blk + tq, :] = kn_ref[...]
    vext[:, 0:blk, :] = vp_ref[...]
    vext[:, blk:blk + tq, :] = vc_ref[...]
    vext[:, blk + tq:2 * blk + tq, :] = vn_ref[...]
    qi = lax.broadcasted_iota(jnp.int32, (blk, blk), 0)
    kj = lax.broadcasted_iota(jnp.int32, (blk, blk), 1)
    lane = lax.broadcasted_iota(jnp.int32, (blk, V7X_LANES), 1)
    lo = lane < HEAD_DIM
    ones_win = jnp.ones((3 * blk, V7X_LANES), _BF16)
    ones_ctx = jnp.ones((ctx_len, V7X_LANES), _BF16)

    def locate(i):
        j = i // N_KV_HEADS
        return j, i % N_KV_HEADS, j * blk

    def scores(i, s_ref):
        _, hkv, row0 = locate(i)
        parts = []
        for c in range(2):
            qc = q_ref[2 * hkv + c, pl.ds(row0, blk), :]
            parts.append(jnp.where(lo, qc, jnp.zeros_like(qc)))
            parts.append(jnp.where(lo, jnp.zeros_like(qc), qc))
        lhs = jnp.concatenate(parts, axis=0)
        s_ref[:, 0:3 * blk] = _dot_t(lhs, kext[hkv, pl.ds(row0, 3 * blk), :])
        s_ref[:, 3 * blk:3 * blk + ctx_len] = _dot_t(lhs, kx_ref[hkv])

    def softmax_pv(i, s_ref):
        j, hkv, row0 = locate(i)
        keep_prev = kj >= qi
        keep_next = kj <= qi
        if j == 0:
            keep_prev = keep_prev & (n > 0)
        if j == nsub - 1:
            keep_next = keep_next & (n < nq - 1)
        vwin = jnp.concatenate([vext[hkv, pl.ds(row0, 3 * blk), :], ones_win], axis=1)
        vx = jnp.concatenate([vx_ref[hkv], ones_ctx], axis=1)
        p_win, p_ctx, esink = [], [], []
        for g in range(GQA_GROUP):
            sink = sink_ref[hkv * GQA_GROUP + g] * LOG2E
            grow = slice(g * blk, (g + 1) * blk)
            s0 = jnp.where(keep_prev, s_ref[grow, 0:blk], NEG)
            s1 = s_ref[grow, blk:2 * blk]
            s2 = jnp.where(keep_next, s_ref[grow, 2 * blk:3 * blk], NEG)
            sc = [s_ref[grow, 3 * blk + t * V7X_LANES:3 * blk + (t + 1) * V7X_LANES]
                  for t in range(ctx_len // V7X_LANES)]
            mm = jnp.maximum(jnp.maximum(s0, s1), s2)
            for t in sc:
                mm = jnp.maximum(mm, t)
            m = jnp.maximum(jnp.max(mm, axis=-1, keepdims=True), sink)
            p_win.append(jnp.concatenate([jnp.exp2(t - m).astype(_BF16) for t in (s0, s1, s2)], axis=1))
            p_ctx.append(jnp.concatenate([jnp.exp2(t - m).astype(_BF16) for t in sc], axis=1))
            esink.append(jnp.exp2(sink - m))
        o = _dot(jnp.concatenate(p_win, axis=0), vwin) + _dot(jnp.concatenate(p_ctx, axis=0), vx)
        og = []
        for g in range(GQA_GROUP):
            grow = slice(g * blk, (g + 1) * blk)
            den = o[grow, V7X_LANES:2 * V7X_LANES] + esink[g]
            og.append(o[grow, 0:V7X_LANES] * (1.0 / den))
        for c in range(2):
            ocol = jnp.where(lo, og[2 * c], og[2 * c + 1])
            o_ref[2 * hkv + c, pl.ds(row0, blk), :] = ocol.astype(_BF16)

    s_bufs = (s_even, s_odd)
    scores(0, s_bufs[0])
    for i in range(nblocks):
        if i + 1 < nblocks:
            scores(i + 1, s_bufs[(i + 1) % 2])
        softmax_pv(i, s_bufs[i % 2])


def _attention(sink, q, kd, vd, kx, vx, batch, seq, ctx_len, tq):
    nq_slab, n, _ = q.shape
    nq = seq // tq
    hb = tq // WINDOW
    sb = seq // WINDOW
    last = n // WINDOW - 1
    kv = N_KV_HEADS

    def cur():
        return pl.BlockSpec((kv, tq, V7X_LANES), lambda b, i: (0, b * nq + i, 0))

    def prev():
        return pl.BlockSpec((kv, WINDOW, V7X_LANES), lambda b, i: (0, jnp.maximum(b * sb + i * hb - 1, 0), 0))

    def nxt():
        return pl.BlockSpec((kv, WINDOW, V7X_LANES), lambda b, i: (0, jnp.minimum(b * sb + (i + 1) * hb, last), 0))

    def ctx():
        return pl.BlockSpec((kv, ctx_len, V7X_LANES), lambda b, i: (0, b, 0))

    kernel = functools.partial(_attn_kernel, tq=tq, nq=nq)
    return pl.pallas_call(
        kernel,
        grid=(batch, nq),
        in_specs=[
            pl.BlockSpec(memory_space=pltpu.SMEM),
            pl.BlockSpec((nq_slab, tq, V7X_LANES), lambda b, i: (0, b * nq + i, 0)),
            prev(), cur(), nxt(), prev(), cur(), nxt(), ctx(), ctx(),
        ],
        out_specs=pl.BlockSpec((nq_slab, tq, V7X_LANES), lambda b, i: (0, b * nq + i, 0)),
        out_shape=jax.ShapeDtypeStruct(q.shape, _BF16),
        scratch_shapes=[
            pltpu.VMEM((kv, tq + 2 * WINDOW, V7X_LANES), _BF16),
            pltpu.VMEM((kv, tq + 2 * WINDOW, V7X_LANES), _BF16),
            pltpu.VMEM((GQA_GROUP * WINDOW, 3 * WINDOW + ctx_len), _F32),
            pltpu.VMEM((GQA_GROUP * WINDOW, 3 * WINDOW + ctx_len), _F32),
        ],
        compiler_params=_params(("parallel", "parallel")),
        name="band_attn",
    )(sink, q, kd, kd, kd, vd, vd, vd, kx, vx)


def _gate_weights(r_w, r_b, i_w, i_b):
    heads_per_half = V7X_MXU_DIM // LRU_HEAD_DIM
    eye = jnp.eye(heads_per_half, dtype=_F32)

    def halves(w):
        w = w.reshape(2, D_LRU // V7X_MXU_DIM, heads_per_half, LRU_HEAD_DIM, LRU_HEAD_DIM)
        bd = jnp.einsum('dxhij,hk->dxhikj', w, eye)
        return bd.reshape(2, D_LRU // V7X_MXU_DIM, V7X_MXU_DIM, V7X_MXU_DIM)

    w = (0.5 * jnp.concatenate([halves(r_w), halves(i_w)], axis=-1)).astype(_BF16)
    rb = r_b.reshape(2, D_LRU // V7X_MXU_DIM, 1, V7X_MXU_DIM)
    ib = i_b.reshape(2, D_LRU // V7X_MXU_DIM, 1, V7X_MXU_DIM)
    return w, 0.5 * jnp.concatenate([rb, ib], axis=-1)


def _rope_table(seq):
    grid_rows = seq // GRID_W
    inv_freq = ROPE_BASE ** (-jnp.arange(ROPE_FREQS, dtype=_F32) / ROPE_FREQS)
    pos = jnp.concatenate([jnp.arange(GRID_W), jnp.arange(grid_rows)]).astype(_F32)
    ang = pos[:, None] * inv_freq
    cos, sin = jnp.cos(ang), jnp.sin(ang)
    is_col = (jnp.arange(GRID_W + grid_rows) < GRID_W)[:, None]
    zero = jnp.zeros_like(cos)
    table = jnp.concatenate([jnp.where(is_col, zero, cos), jnp.where(is_col, cos, zero),
                             jnp.where(is_col, zero, sin), jnp.where(is_col, sin, zero)], axis=1)
    return jnp.pad(table, ((0, 0), (0, V7X_LANES - 4 * ROPE_FREQS)))


def kernel(x, c, ctx, c_ctx, ada_w, ada_b, norm_mix_g, norm_ffn_g, norm_final_g, ffn_w_in, ffn_w_out,
           rc_w_in, rc_conv_a_w, rc_conv_a_b, rc_gate_r_w, rc_gate_r_b, rc_gate_i_w, rc_gate_i_b,
           rc_lambda, rc_conv_b_w, rc_w_out, at_w_qkv, at_sink, at_w_out):
    batch, seq, d = x.shape
    ctx_len = ctx.shape[1]
    tm = 1024
    tm_rc = 512
    tm_ffn = 1024
    tq = 1024
    tc = 512
    assert seq % tm_ffn == 0 and seq % tm_rc == 0 and seq % tm == 0 and seq % tq == 0 and seq % tc == 0
    assert ctx_len % V7X_LANES == 0

    xl = x.reshape(batch * seq, d)
    xc = ctx.reshape(batch * ctx_len, d)

    cond = jnp.concatenate([c_ctx[None], c, jnp.zeros((8 - 1 - batch, d), _F32)], axis=0)
    mod = _modulation(cond, ada_w, ada_b).reshape(ada_w.shape[0], 8, N_MOD, d)

    def lat_row(tile):
        return lambda i: 1 + i // (seq // tile)

    ctx_row = lambda i: 0

    w_in = rc_w_in[0].astype(_BF16)
    w_gate, b_gate = _gate_weights(rc_gate_r_w[0], rc_gate_r_b[0], rc_gate_i_w[0], rc_gate_i_b[0])
    lam = rc_lambda[0].reshape(2, 1, D_LRU)
    conv_a_b = rc_conv_a_b[0].reshape(1, D_LRU)
    gmix0 = norm_mix_g[0].reshape(1, d)
    gffn0 = norm_ffn_g[0].reshape(1, d)

    xa_c, p_c = _project(xc, mod[0], gmix0, w_in, batch * ctx_len, ctx_row)
    xa_l, p_l = _project(xl, mod[0], gmix0, w_in, tm, lat_row(tm))

    h0 = jnp.zeros((batch, 2, V7X_SUBLANES, D_LRU), _F32)
    hf_c, hb_c, h_ctx = _scan(xa_c, h0, rc_conv_a_w[0], conv_a_b, w_gate, b_gate, lam, batch, ctx_len, ctx_len)
    later_weights = [ffn_w_in.reshape(-1, ffn_w_in.shape[-1]), ffn_w_out.reshape(-1, d),
                     rc_w_out[0], at_w_qkv[0], at_w_out[0]]
    hf_l, hb_l, _, w1, w2, w_out0, w_qkv, w_out1 = _scan(
        xa_l, h_ctx, rc_conv_a_w[0], conv_a_b, w_gate, b_gate, lam, batch, seq, tc, casts=later_weights)
    w1 = w1.reshape(ffn_w_in.shape)
    w2 = w2.reshape(ffn_w_out.shape)

    xc = _rc_out(xc, hf_c, hb_c, p_c, rc_conv_b_w[0], w_out0, mod[0], gffn0, w1, w2, 0,
                 ctx_len, ctx_len, ctx_row)
    xl = _rc_out(xl, hf_l, hb_l, p_l, rc_conv_b_w[0], w_out0, mod[0], gffn0, w1, w2, 0,
                 tm_rc, seq, lat_row(tm_rc))

    gmix1 = norm_mix_g[1].reshape(1, d)
    gffn1 = norm_ffn_g[1].reshape(1, d)

    q, kd, vd = _qkv(xl, mod[1], gmix1, w_qkv, _rope_table(seq), tm, seq, lat_row(tm))
    kx, vx = _ctx_kv(xc, mod[1], gmix1, w_qkv)
    o = _attention(at_sink[0], q, kd, vd, kx, vx, batch, seq, ctx_len, tq)

    out = _attn_out(xl, o, w_out1, mod[1], gffn1, w1, w2, 1,
                    norm_final_g.reshape(1, d), tm_ffn, lat_row(tm_ffn))
    return out.reshape(batch, seq, d)
```

```python
import functools

import jax
import jax.numpy as jnp
from jax import lax
from jax.experimental import pallas as pl
from jax.experimental.pallas import tpu as pltpu

D_MODEL = 1024
N_MOD = 6
EPS = 1e-6
NEG = -1e30
D_LRU = 512
D_SC = 512
LRU_HEADS = 8
LRU_HEAD_DIM = 64
LRU_C = 8.0
RC_IN_WIDTH = 2 * D_LRU + 3 * D_SC
HEAD_DIM = 64
N_Q_HEADS = 16
N_KV_HEADS = 4
GQA_GROUP = 4
WINDOW = 128
GRID_W = 64
ROPE_BASE = 10000.0
ROPE_FREQS = 16
D_FF = 2816
LOG2E = 1.4426950408889634

V7X_LANES = 128
V7X_SUBLANES = 8
V7X_MXU_DIM = 256
V7X_VMEM_LIMIT = 60 * 1024 * 1024

HALO = V7X_SUBLANES
HALO_BF16 = 2 * V7X_SUBLANES
N_SLAB = D_LRU // V7X_LANES
SCAN_SHIFTS = (1, 2, 4)
FFN_CHUNKS = ((0, 1024), (1024, 1024), (2048, 768))
FFN_SUB_ROWS = 512
PROJ_SUB_ROWS = 512

_BF16 = jnp.bfloat16
_F32 = jnp.float32


def _dot(a, b):
    return jnp.dot(a, b, preferred_element_type=_F32)


def _dot_t(a, b):
    return lax.dot_general(a, b, (((1,), (1,)), ((), ())), preferred_element_type=_F32)


def _sigmoid(x):
    return 0.5 * jnp.tanh(0.5 * x) + 0.5


def _gelu_tanh(x):
    return 0.5 * x * (1.0 + jnp.tanh(0.7978845608028654 * (x + 0.044715 * (x * x * x))))


def _rms_mod(x, gain, shift, scale):
    ms = jnp.mean(x * x, axis=-1, keepdims=True)
    return (x * lax.rsqrt(ms + EPS) * gain) * (1.0 + scale) + shift


def _const_spec(shape, index=None):
    idx = (0,) * len(shape) if index is None else index
    return pl.BlockSpec(shape, lambda *_: idx, pipeline_mode=pl.Buffered(1))


def _params(sem):
    return pltpu.CompilerParams(dimension_semantics=sem, vmem_limit_bytes=V7X_VMEM_LIMIT)


def _cast_specs(jobs, nsteps):
    ins, outs, shapes = [], [], []
    for w, layer, _ in jobs:
        _, r, c = w.shape
        assert r % (nsteps * HALO_BF16) == 0, (w.shape, nsteps)
        chunk = r // nsteps
        ins.append(pl.BlockSpec((1, chunk, c), lambda i, layer=layer: (layer, i, 0)))
        outs.append(pl.BlockSpec((chunk, c), lambda i: (i, 0)))
        shapes.append(jax.ShapeDtypeStruct((r, c), _BF16))
    return ins, outs, shapes


def _cast_chunks(rope_cols_per_job, cast_in, cast_out):
    for rope_cols, src, dst in zip(rope_cols_per_job, cast_in, cast_out):
        moves = _rope_lane_moves((src.shape[1], V7X_LANES)) if rope_cols else None
        for c0 in range(0, rope_cols, V7X_LANES):
            dst[:, c0:c0 + V7X_LANES] = _to_rope_lanes(src[0, :, c0:c0 + V7X_LANES], moves).astype(_BF16)
        dst[:, rope_cols:] = src[0, :, rope_cols:].astype(_BF16)


def _mod_kernel(c_ref, w_ref, b_ref, o_ref):
    c = c_ref[...]
    s = (c * _sigmoid(c)).astype(_BF16)
    o_ref[0] = _dot(s, w_ref[0].astype(_BF16)) + b_ref[0]


def _modulation(cond, ada_w, ada_b):
    depth, d, n = ada_w.shape
    bn = 1536
    return pl.pallas_call(
        _mod_kernel,
        grid=(depth, n // bn),
        in_specs=[
            pl.BlockSpec((8, d), lambda l, j: (0, 0)),
            pl.BlockSpec((1, d, bn), lambda l, j: (l, 0, j)),
            pl.BlockSpec((1, 1, bn), lambda l, j: (l, 0, j)),
        ],
        out_specs=pl.BlockSpec((1, 8, bn), lambda l, j: (l, 0, j)),
        out_shape=jax.ShapeDtypeStruct((depth, 8, n), _F32),
        compiler_params=_params(("arbitrary", "arbitrary")),
        name="adaln_mod",
    )(cond, ada_w, ada_b.reshape(depth, 1, n))


def _proj_kernel(*refs, cast_rope, own_weight):
    ncast = len(cast_rope)
    nin = 4 if own_weight else 3
    x_ref, mod_ref, g_ref = refs[:3]
    xa_ref, rest_ref = refs[nin + ncast:nin + 2 + ncast]
    cast_out = refs[nin + 2 + ncast:]
    _cast_chunks(cast_rope, refs[nin:nin + ncast], cast_out)
    w_ref = refs[3] if own_weight else cast_out[0]
    h_all = _rms_mod(x_ref[...], g_ref[...], mod_ref[0, 0:1, :], mod_ref[0, 1:2, :]).astype(_BF16)
    tm = h_all.shape[0]
    sub = min(PROJ_SUB_ROWS, tm)
    for r0 in range(0, tm, sub):
        rows = slice(r0, r0 + sub)
        h = h_all[rows, :]
        xa = _dot(h, w_ref[:, 0:D_LRU])
        for c in range(N_SLAB):
            xa_ref[c, rows, :] = xa[:, c * V7X_LANES:(c + 1) * V7X_LANES]
        ga = _dot(h, w_ref[:, D_LRU:2 * D_LRU])
        rest_ref[rows, 0:D_LRU] = _gelu_tanh(ga).astype(_BF16)
        rest_ref[rows, D_LRU:D_LRU + D_SC] = _dot(h, w_ref[:, 2 * D_LRU:2 * D_LRU + D_SC]).astype(_BF16)
        cv = _dot(h, w_ref[:, 2 * D_LRU + D_SC:2 * D_LRU + 3 * D_SC])
        rest_ref[rows, D_LRU + D_SC:D_LRU + 2 * D_SC] = (cv[:, 0:D_SC] * cv[:, D_SC:2 * D_SC]).astype(_BF16)


def _project(x2d, mod, gain, w, tm, row_of_tile, cast_jobs=()):
    n, d = x2d.shape
    nrest = D_LRU + 2 * D_SC
    assert w is not None or n == tm
    cast_in, cast_out, cast_shapes = _cast_specs(cast_jobs, n // tm)
    weight = [] if w is None else [w]
    return pl.pallas_call(
        functools.partial(_proj_kernel, cast_rope=tuple(job[2] for job in cast_jobs), own_weight=w is not None),
        grid=(n // tm,),
        in_specs=[
            pl.BlockSpec((tm, d), lambda i: (i, 0)),
            pl.BlockSpec((1, N_MOD, d), lambda i: (row_of_tile(i), 0, 0)),
            _const_spec((1, d)),
        ] + [_const_spec(a.shape) for a in weight] + cast_in,
        out_specs=[
            pl.BlockSpec((N_SLAB, tm, V7X_LANES), lambda i: (0, i, 0)),
            pl.BlockSpec((tm, nrest), lambda i: (i, 0)),
        ] + cast_out,
        out_shape=[
            jax.ShapeDtypeStruct((N_SLAB, n, V7X_LANES), _F32),
            jax.ShapeDtypeStruct((n, nrest), _BF16),
        ] + cast_shapes,
        compiler_params=_params(("arbitrary",)),
        name="rc_in_proj",
    )(x2d, mod, gain, *weight, *[job[0] for job in cast_jobs])


def _scan_kernel(xm_f, xp_f, xn_f, xm_b, xp_b, xn_b, cw_ref, cb_ref, wg_ref, bg_ref, lam_ref, h0_ref,
                 hf_ref, hb_ref, hl_ref, xs_scr, carry_scr, *level_scr, nc, tc):
    j = pl.program_id(1)
    lo = HALO
    ngroups = tc // V7X_SUBLANES
    nlev = len(SCAN_SHIFTS)
    a_lv = list(level_scr[:nlev])
    b_lv = list(level_scr[nlev:])
    pad_rows = (slice(0, HALO), slice(lo + tc, lo + tc + HALO))

    @pl.when(j == 0)
    def _():
        carry_scr[...] = h0_ref[0]
        for d in range(2):
            for ref in a_lv:
                ref[d, :, pad_rows[d], :] = jnp.ones((N_SLAB, HALO, V7X_LANES), _F32)
            for ref in b_lv:
                ref[d, :, pad_rows[d], :] = jnp.zeros((N_SLAB, HALO, V7X_LANES), _F32)

    cw = cw_ref[...]
    cb = cb_ref[...]

    def coeffs(d, xm, xp, xn, has_prev, has_next):
        xs_scr[d, :, 0:HALO, :] = jnp.where(has_prev, xp[...], 0.0)
        xs_scr[d, :, lo:lo + tc, :] = xm[...]
        xs_scr[d, :, lo + tc:lo + tc + HALO, :] = jnp.where(has_next, xn[...], 0.0)
        xcs = []
        for c in range(N_SLAB):
            lanes = slice(c * V7X_LANES, (c + 1) * V7X_LANES)
            acc = cb[:, lanes]
            for k in range(4):
                acc = acc + cw[k:k + 1, lanes] * xs_scr[d, c, lo - 2 + k:lo - 2 + k + tc, :]
            xcs.append(acc)
        lam = lam_ref[d]
        nlam = -lam
        softplus = jnp.maximum(nlam, 0.0) + jnp.log(1.0 + jnp.exp(-jnp.abs(nlam)))
        hrate = (-0.5 * LRU_C * LOG2E) * softplus
        half = V7X_MXU_DIM
        per_half = half // V7X_LANES
        for hh in range(D_LRU // half):
            xch = jnp.concatenate(xcs[hh * per_half:(hh + 1) * per_half], axis=1)
            z = _dot(xch.astype(_BF16), wg_ref[d, hh]) + bg_ref[d, hh]
            for cc in range(per_half):
                c = hh * per_half + cc
                lanes = slice(c * V7X_LANES, (c + 1) * V7X_LANES)
                tr = jnp.tanh(z[:, cc * V7X_LANES:(cc + 1) * V7X_LANES])
                ig = 0.5 * jnp.tanh(z[:, half + cc * V7X_LANES:half + (cc + 1) * V7X_LANES]) + 0.5
                a = jnp.exp2(tr * hrate[:, lanes] + hrate[:, lanes])
                one_m_a2 = jnp.maximum(1.0 - a * a, 1e-12)
                mult = one_m_a2 * lax.rsqrt(one_m_a2)
                a_lv[0][d, c, lo:lo + tc, :] = a
                b_lv[0][d, c, lo:lo + tc, :] = mult * ig * xcs[c]

    coeffs(0, xm_f, xp_f, xn_f, j > 0, j < nc - 1)
    coeffs(1, xm_b, xp_b, xn_b, j < nc - 1, j > 0)

    def step(g, carry):
        out = []
        for d, h_ref in ((0, hf_ref), (1, hb_ref)):
            grp = g if d == 0 else ngroups - 1 - g
            row0 = pl.multiple_of(lo + grp * V7X_SUBLANES, V7X_SUBLANES)
            cur = pl.ds(row0, V7X_SUBLANES)
            for c in range(N_SLAB):
                a = a_lv[0][d, c, cur, :]
                b = b_lv[0][d, c, cur, :]
                for lvl, shift in enumerate(SCAN_SHIFTS):
                    sh = pl.ds(row0 + (shift if d else -shift), V7X_SUBLANES)
                    b = a * b_lv[lvl][d, c, sh, :] + b
                    a = a * a_lv[lvl][d, c, sh, :]
                    if lvl + 1 < nlev:
                        a_lv[lvl + 1][d, c, cur, :] = a
                        b_lv[lvl + 1][d, c, cur, :] = b
                h = a * carry[d * N_SLAB + c] + b
                out_rows = pl.ds(pl.multiple_of(grp * V7X_SUBLANES, V7X_SUBLANES), V7X_SUBLANES)
                h_ref[out_rows, c * V7X_LANES:(c + 1) * V7X_LANES] = h
                out.append(h)
        return tuple(out)

    init = tuple(carry_scr[d, :, c * V7X_LANES:(c + 1) * V7X_LANES]
                 for d in range(2) for c in range(N_SLAB))
    carry = lax.fori_loop(0, ngroups, step, init, unroll=4)

    edge_row = (V7X_SUBLANES - 1, 0)
    data_edge = (slice(tc, tc + HALO), slice(lo, lo + HALO))
    for d in range(2):
        for c in range(N_SLAB):
            lanes = slice(c * V7X_LANES, (c + 1) * V7X_LANES)
            h = carry[d * N_SLAB + c]
            carry_scr[d, :, lanes] = h
            hl_ref[0, d, :, lanes] = jnp.broadcast_to(h[edge_row[d]:edge_row[d] + 1, :], h.shape)
        for ref in a_lv + b_lv:
            ref[d, :, pad_rows[d], :] = ref[d, :, data_edge[d], :]


def _scan(xa, h0, conv_w, conv_b, w_gate, b_gate, lam, batch, seq, tc):
    n = xa.shape[1]
    nc = seq // tc
    tb = tc // HALO
    sb = seq // HALO
    last = n // HALO - 1

    def fwd(b, j):
        return j

    def bwd(b, j):
        return nc - 1 - j

    def main(cf):
        return pl.BlockSpec((N_SLAB, tc, V7X_LANES), lambda b, j: (0, b * nc + cf(b, j), 0))

    def prev(cf):
        return pl.BlockSpec((N_SLAB, HALO, V7X_LANES),
                            lambda b, j: (0, jnp.maximum(b * sb + cf(b, j) * tb - 1, 0), 0))

    def nxt(cf):
        return pl.BlockSpec((N_SLAB, HALO, V7X_LANES),
                            lambda b, j: (0, jnp.minimum(b * sb + (cf(b, j) + 1) * tb, last), 0))

    kernel = functools.partial(_scan_kernel, nc=nc, tc=tc)
    nlev = len(SCAN_SHIFTS)
    return pl.pallas_call(
        kernel,
        grid=(batch, nc),
        in_specs=[
            main(fwd), prev(fwd), nxt(fwd), main(bwd), prev(bwd), nxt(bwd),
            _const_spec((4, D_LRU)),
            _const_spec((1, D_LRU)),
            _const_spec(w_gate.shape),
            _const_spec(b_gate.shape),
            _const_spec((2, 1, D_LRU)),
            pl.BlockSpec((1, 2, V7X_SUBLANES, D_LRU), lambda b, j: (b, 0, 0, 0)),
        ],
        out_specs=[
            pl.BlockSpec((tc, D_LRU), lambda b, j: (b * nc + j, 0)),
            pl.BlockSpec((tc, D_LRU), lambda b, j: (b * nc + nc - 1 - j, 0)),
            pl.BlockSpec((1, 2, V7X_SUBLANES, D_LRU), lambda b, j: (b, 0, 0, 0)),
        ],
        out_shape=[
            jax.ShapeDtypeStruct((n, D_LRU), _F32),
            jax.ShapeDtypeStruct((n, D_LRU), _F32),
            jax.ShapeDtypeStruct((batch, 2, V7X_SUBLANES, D_LRU), _F32),
        ],
        scratch_shapes=[
            pltpu.VMEM((2, N_SLAB, tc + 2 * HALO, V7X_LANES), _F32),
            pltpu.VMEM((2, V7X_SUBLANES, D_LRU), _F32),
        ] + [pltpu.VMEM((2, N_SLAB, tc + 2 * HALO, V7X_LANES), _F32) for _ in range(2 * nlev)],
        compiler_params=_params(("arbitrary", "arbitrary")),
        name="rglru_scan",
    )(xa, xa, xa, xa, xa, xa, conv_w, conv_b, w_gate, b_gate, lam, h0)


def _ffn_norm(x, y, mod_ref, gffn_ref):
    x1 = x + mod_ref[0, 2:3, :] * y
    h = _rms_mod(x1, gffn_ref[...], mod_ref[0, 3:4, :], mod_ref[0, 4:5, :]).astype(_BF16)
    return x1, h


def _ffn_apply(x1, h, mod_ref, w1_ref, w2_ref, gfin_ref):
    acc = None
    for c0, cw in FFN_CHUNKS:
        g = _dot(h, w1_ref[:, c0:c0 + cw])
        u = _dot(h, w1_ref[:, D_FF + c0:D_FF + c0 + cw])
        act = (g * _sigmoid(g) * u).astype(_BF16)
        part = _dot(act, w2_ref[c0:c0 + cw, :])
        acc = part if acc is None else acc + part
    x2 = x1 + mod_ref[0, 5:6, :] * acc
    if gfin_ref is not None:
        ms = jnp.mean(x2 * x2, axis=-1, keepdims=True)
        x2 = x2 * lax.rsqrt(ms + EPS) * gfin_ref[...]
    return x2


def _ffn_weight_specs(w1, w2):
    return [_const_spec(w1.shape), _const_spec(w2.shape)]


def _rc_out_kernel(*refs, nt, tm, cast_rope):
    ncast = len(cast_rope)
    (x_ref, hf_ref, hb_ref, gg_ref, bg_ref, um_ref, up_ref, un_ref,
     cbw_ref, wo_ref, mod_ref, gffn_ref, w1_ref, w2_ref) = refs[:14]
    o_ref = refs[14 + ncast]
    _cast_chunks(cast_rope, refs[14:14 + ncast], refs[15 + ncast:])
    i = pl.program_id(0)
    has_prev = (i % nt) > 0
    has_next = (i % nt) < nt - 1
    prev = jnp.where(has_prev, up_ref[...].astype(_F32), 0.0)
    nxt = jnp.where(has_next, un_ref[...].astype(_F32), 0.0)
    uext = jnp.concatenate([prev, um_ref[...].astype(_F32), nxt], axis=0)
    cbw = cbw_ref[...]
    sub = min(FFN_SUB_ROWS, tm // 2)
    normed = []
    for r0 in range(0, tm, sub):
        rows = slice(r0, r0 + sub)
        ya = ((hf_ref[rows, :] + hb_ref[rows, :]) * gg_ref[rows, :].astype(_F32)).astype(_BF16)
        conv = None
        for k in range(3):
            first = HALO_BF16 - 1 + k + r0
            term = cbw[k:k + 1, :] * uext[first:first + sub, :]
            conv = term if conv is None else conv + term
        yb = (bg_ref[rows, :].astype(_F32) * conv).astype(_BF16)
        y = _dot(ya, wo_ref[0:D_LRU, :]) + _dot(yb, wo_ref[D_LRU:D_LRU + D_SC, :])
        normed.append(_ffn_norm(x_ref[rows, :], y, mod_ref, gffn_ref))
    for k, r0 in enumerate(range(0, tm, sub)):
        x1, h = normed[k]
        o_ref[r0:r0 + sub, :] = _ffn_apply(x1, h, mod_ref, w1_ref, w2_ref, None)


def _rc_out(x2d, hf, hb, p, conv_b_w, w_out, mod, gffn, w1, w2, tm, seq, row_of_tile, cast_jobs=()):
    n, d = x2d.shape
    nt = seq // tm
    tb = tm // HALO_BF16
    last = n // HALO_BF16 - 1
    cast_in, cast_out, cast_shapes = _cast_specs(cast_jobs, n // tm)

    def col(c):
        return pl.BlockSpec((tm, D_SC), lambda i: (i, c))

    kernel = functools.partial(_rc_out_kernel, nt=nt, tm=tm, cast_rope=tuple(job[2] for job in cast_jobs))
    return pl.pallas_call(
        kernel,
        grid=(n // tm,),
        in_specs=[
            pl.BlockSpec((tm, d), lambda i: (i, 0)),
            pl.BlockSpec((tm, D_LRU), lambda i: (i, 0)),
            pl.BlockSpec((tm, D_LRU), lambda i: (i, 0)),
            col(0), col(1), col(2),
            pl.BlockSpec((HALO_BF16, D_SC), lambda i: (jnp.maximum(i * tb - 1, 0), 2)),
            pl.BlockSpec((HALO_BF16, D_SC), lambda i: (jnp.minimum((i + 1) * tb, last), 2)),
            _const_spec((3, D_SC)),
            _const_spec(w_out.shape),
            pl.BlockSpec((1, N_MOD, d), lambda i: (row_of_tile(i), 0, 0)),
            _const_spec((1, d)),
        ] + _ffn_weight_specs(w1, w2) + cast_in,
        out_specs=[pl.BlockSpec((tm, d), lambda i: (i, 0))] + cast_out,
        out_shape=[jax.ShapeDtypeStruct((n, d), _F32)] + cast_shapes,
        compiler_params=_params(("arbitrary",)),
        name="rc_out_ffn",
    )(x2d, hf, hb, p, p, p, p, p, conv_b_w, w_out, mod, gffn, w1, w2, *[job[0] for job in cast_jobs])


def _attn_out_kernel(x_ref, o_in_ref, wo_ref, mod_ref, gffn_ref, gfin_ref, w1_ref, w2_ref, o_ref):
    tm = x_ref.shape[0]
    sub = min(FFN_SUB_ROWS, tm)
    normed = []
    for r0 in range(0, tm, sub):
        rows = slice(r0, r0 + sub)
        o_in = jnp.concatenate([o_in_ref[c, rows, :] for c in range(o_in_ref.shape[0])], axis=1)
        normed.append(_ffn_norm(x_ref[rows, :], _dot(o_in, wo_ref[...]), mod_ref, gffn_ref))
    for k, r0 in enumerate(range(0, tm, sub)):
        x1, h = normed[k]
        o_ref[r0:r0 + sub, :] = _ffn_apply(x1, h, mod_ref, w1_ref, w2_ref, gfin_ref)


def _attn_out(x2d, o, w_out, mod, gffn, w1, w2, gfin, tm, row_of_tile):
    n, d = x2d.shape
    return pl.pallas_call(
        _attn_out_kernel,
        grid=(n // tm,),
        in_specs=[
            pl.BlockSpec((tm, d), lambda i: (i, 0)),
            pl.BlockSpec((o.shape[0], tm, V7X_LANES), lambda i: (0, i, 0)),
            _const_spec(w_out.shape),
            pl.BlockSpec((1, N_MOD, d), lambda i: (row_of_tile(i), 0, 0)),
            _const_spec((1, d)),
            _const_spec((1, d)),
        ] + _ffn_weight_specs(w1, w2),
        out_specs=pl.BlockSpec((tm, d), lambda i: (i, 0)),
        out_shape=jax.ShapeDtypeStruct((n, d), _F32),
        compiler_params=_params(("parallel",)),
        name="attn_out_ffn",
    )(x2d, o, w_out, mod, gffn, gfin, w1, w2)


ROPE_HALF = V7X_LANES // 2
ROPE_HEAD_SPAN = ROPE_HALF // 2


def _rope_lane_moves(shape):
    lane = lax.broadcasted_iota(jnp.int32, shape, 1)
    dst = lane // ROPE_FREQS
    src = ((dst % 4) // 2) * 4 + (dst % 2) * 2 + dst // 4
    return {delta: (src - dst) == delta for delta in range(-3, 4) if delta != 0}


def _to_rope_lanes(x, moves):
    out = x
    for delta, mask in moves.items():
        shifted = pltpu.roll(x, (-delta * ROPE_FREQS) % V7X_LANES, 1)
        out = jnp.where(mask, shifted, out)
    return out


def _rope(x, cos, sin_signed):
    return x * cos + pltpu.roll(x, ROPE_HALF, 1) * sin_signed


def _rope_lane_tables(cs):
    lane = lax.broadcasted_iota(jnp.int32, cs.shape, 1)
    span = ROPE_HEAD_SPAN
    cos = jnp.where(lane < span, cs, pltpu.roll(cs, span, 1))
    sin = jnp.where(lane < span, pltpu.roll(cs, V7X_LANES - span, 1), cs)
    first = lane < ROPE_HALF
    cos = jnp.where(first, cos, pltpu.roll(cos, ROPE_HALF, 1))
    sin = jnp.where(first, sin, pltpu.roll(sin, ROPE_HALF, 1))
    return cos, jnp.where(first, -sin, sin)


def _key_heads_to_slabs(x, ref, c, rows):
    lane = lax.broadcasted_iota(jnp.int32, x.shape, 1)
    is_first = (lane % ROPE_HALF) < ROPE_HEAD_SPAN
    ref[2 * c, rows, :] = jnp.where(is_first, x, pltpu.roll(x, ROPE_HEAD_SPAN, 1)).astype(_BF16)
    ref[2 * c + 1, rows, :] = jnp.where(is_first, pltpu.roll(x, V7X_LANES - ROPE_HEAD_SPAN, 1), x).astype(_BF16)


def _heads_to_slabs(x, ref, c, rows):
    first = lax.broadcasted_iota(jnp.int32, x.shape, 1) < HEAD_DIM
    swapped = pltpu.roll(x, HEAD_DIM, 1)
    ref[2 * c, rows, :] = jnp.where(first, x, swapped).astype(_BF16)
    ref[2 * c + 1, rows, :] = jnp.where(first, swapped, x).astype(_BF16)


def _qkv_kernel(*refs, nt, tm, cast_rope):
    ncast = len(cast_rope)
    x_ref, mod_ref, g_ref, w_ref, tab_ref = refs[:5]
    q_ref, k_ref, v_ref = refs[5 + ncast:8 + ncast]
    _cast_chunks(cast_rope, refs[5:5 + ncast], refs[8 + ncast:])
    h = _rms_mod(x_ref[...], g_ref[...], mod_ref[0, 0:1, :], mod_ref[0, 1:2, :]).astype(_BF16)
    col_terms = tab_ref[0:GRID_W, :]
    first_grid_row = GRID_W + (pl.program_id(0) % nt) * (tm // GRID_W)
    cs = jnp.concatenate([col_terms + tab_ref[pl.ds(first_grid_row + g, 1), :] for g in range(tm // GRID_W)],
                         axis=0)
    cos_t, sin_t = _rope_lane_tables(cs)
    dq = N_Q_HEADS * HEAD_DIM
    dkv = N_KV_HEADS * HEAD_DIM
    scale = HEAD_DIM ** -0.5 * LOG2E
    sub = min(PROJ_SUB_ROWS, tm)
    for r0 in range(0, tm, sub):
        rows = slice(r0, r0 + sub)
        hs, cos, sin = h[rows, :], cos_t[rows, :], sin_t[rows, :]
        q = _dot(hs, w_ref[:, 0:dq])
        for c in range(dq // V7X_LANES):
            cols = slice(c * V7X_LANES, (c + 1) * V7X_LANES)
            q_ref[c, rows, :] = (_rope(q[:, cols], cos, sin) * scale).astype(_BF16)
        k = _dot(hs, w_ref[:, dq:dq + dkv])
        v = _dot(hs, w_ref[:, dq + dkv:dq + 2 * dkv])
        for c in range(dkv // V7X_LANES):
            cols = slice(c * V7X_LANES, (c + 1) * V7X_LANES)
            _key_heads_to_slabs(_rope(k[:, cols], cos, sin), k_ref, c, rows)
            _heads_to_slabs(v[:, cols], v_ref, c, rows)


def _qkv(x2d, mod, gain, w, cs_t, tm, seq, row_of_tile, cast_jobs=()):
    n, d = x2d.shape
    nt = seq // tm
    nq_slab = N_Q_HEADS * HEAD_DIM // V7X_LANES
    assert tm % GRID_W == 0
    cast_in, cast_out, cast_shapes = _cast_specs(cast_jobs, n // tm)
    return pl.pallas_call(
        functools.partial(_qkv_kernel, nt=nt, tm=tm, cast_rope=tuple(job[2] for job in cast_jobs)),
        grid=(n // tm,),
        in_specs=[
            pl.BlockSpec((tm, d), lambda i: (i, 0)),
            pl.BlockSpec((1, N_MOD, d), lambda i: (row_of_tile(i), 0, 0)),
            _const_spec((1, d)),
            _const_spec(w.shape),
            _const_spec(cs_t.shape),
        ] + cast_in,
        out_specs=[
            pl.BlockSpec((nq_slab, tm, V7X_LANES), lambda i: (0, i, 0)),
            pl.BlockSpec((N_KV_HEADS, tm, V7X_LANES), lambda i: (0, i, 0)),
            pl.BlockSpec((N_KV_HEADS, tm, V7X_LANES), lambda i: (0, i, 0)),
        ] + cast_out,
        out_shape=[
            jax.ShapeDtypeStruct((nq_slab, n, V7X_LANES), _BF16),
            jax.ShapeDtypeStruct((N_KV_HEADS, n, V7X_LANES), _BF16),
            jax.ShapeDtypeStruct((N_KV_HEADS, n, V7X_LANES), _BF16),
        ] + cast_shapes,
        compiler_params=_params(("parallel",)),
        name="qkv_rope",
    )(x2d, mod, gain, w, cs_t, *[job[0] for job in cast_jobs])


def _kv_kernel(x_ref, mod_ref, g_ref, w_ref, k_ref, v_ref):
    h = _rms_mod(x_ref[...], g_ref[...], mod_ref[0, 0:1, :], mod_ref[0, 1:2, :]).astype(_BF16)
    dq = N_Q_HEADS * HEAD_DIM
    dkv = N_KV_HEADS * HEAD_DIM
    k = _dot(h, w_ref[:, dq:dq + dkv])
    v = _dot(h, w_ref[:, dq + dkv:dq + 2 * dkv])
    for c in range(dkv // V7X_LANES):
        cols = slice(c * V7X_LANES, (c + 1) * V7X_LANES)
        _key_heads_to_slabs(k[:, cols], k_ref, c, slice(None))
        _heads_to_slabs(v[:, cols], v_ref, c, slice(None))


def _ctx_kv(x2d, mod, gain, w):
    n, d = x2d.shape
    slab = (N_KV_HEADS, n, V7X_LANES)
    return pl.pallas_call(
        _kv_kernel,
        grid=(1,),
        in_specs=[
            pl.BlockSpec((n, d), lambda i: (0, 0)),
            pl.BlockSpec((1, N_MOD, d), lambda i: (0, 0, 0)),
            _const_spec((1, d)),
            _const_spec(w.shape),
        ],
        out_specs=[pl.BlockSpec(slab, lambda i: (0, 0, 0)), pl.BlockSpec(slab, lambda i: (0, 0, 0))],
        out_shape=[jax.ShapeDtypeStruct(slab, _BF16), jax.ShapeDtypeStruct(slab, _BF16)],
        compiler_params=_params(("arbitrary",)),
        name="ctx_kv",
    )(x2d, mod, gain, w)


def _attn_kernel(sink_ref, q_ref, kp_ref, kc_ref, kn_ref, vp_ref, vc_ref, vn_ref, kx_ref, vx_ref, o_ref,
                 kext, vext, s_even, s_odd, *, tq, nq):
    n = pl.program_id(1)
    blk = WINDOW
    nsub = tq // blk
    nblocks = nsub * N_KV_HEADS
    ctx_len = kx_ref.shape[1]
    kext[:, 0:blk, :] = kp_ref[...]
    kext[:, blk:blk + tq, :] = kc_ref[...]
    kext[:, blk + tq:2 * blk + tq, :] = kn_ref[...]
    vext[:, 0:blk, :] = vp_ref[...]
    vext[:, blk:blk + tq, :] = vc_ref[...]
    vext[:, blk + tq:2 * blk + tq, :] = vn_ref[...]
    qi = lax.broadcasted_iota(jnp.int32, (blk, blk), 0)
    kj = lax.broadcasted_iota(jnp.int32, (blk, blk), 1)
    lane = lax.broadcasted_iota(jnp.int32, (blk, V7X_LANES), 1)
    lo = lane < HEAD_DIM
    q_first = (lane % ROPE_HALF) < ROPE_HEAD_SPAN
    ones_win = jnp.ones((3 * blk, V7X_LANES), _BF16)
    ones_ctx = jnp.ones((ctx_len, V7X_LANES), _BF16)

    def locate(i):
        j = i // N_KV_HEADS
        return j, i % N_KV_HEADS, j * blk

    def scores(i, s_ref):
        _, hkv, row0 = locate(i)
        parts = []
        for c in range(2):
            qc = q_ref[2 * hkv + c, pl.ds(row0, blk), :]
            parts.append(jnp.where(q_first, qc, jnp.zeros_like(qc)))
            parts.append(jnp.where(q_first, jnp.zeros_like(qc), qc))
        lhs = jnp.concatenate(parts, axis=0)
        s_ref[:, 0:3 * blk] = _dot_t(lhs, kext[hkv, pl.ds(row0, 3 * blk), :])
        s_ref[:, 3 * blk:3 * blk + ctx_len] = _dot_t(lhs, kx_ref[hkv])

    def softmax_pv(i, s_ref):
        j, hkv, row0 = locate(i)
        keep_prev = kj >= qi
        keep_next = kj <= qi
        if j == 0:
            keep_prev = keep_prev & (n > 0)
        if j == nsub - 1:
            keep_next = keep_next & (n < nq - 1)
        vwin = jnp.concatenate([vext[hkv, pl.ds(row0, 3 * blk), :], ones_win], axis=1)
        vx = jnp.concatenate([vx_ref[hkv], ones_ctx], axis=1)
        p_win, p_ctx, esink = [], [], []
        for g in range(GQA_GROUP):
            sink = sink_ref[hkv * GQA_GROUP + g] * LOG2E
            grow = slice(g * blk, (g + 1) * blk)
            s0 = jnp.where(keep_prev, s_ref[grow, 0:blk], NEG)
            s1 = s_ref[grow, blk:2 * blk]
            s2 = jnp.where(keep_next, s_ref[grow, 2 * blk:3 * blk], NEG)
            sc = [s_ref[grow, 3 * blk + t * V7X_LANES:3 * blk + (t + 1) * V7X_LANES]
                  for t in range(ctx_len // V7X_LANES)]
            mm = jnp.maximum(jnp.maximum(s0, s1), s2)
            for t in sc:
                mm = jnp.maximum(mm, t)
            m = jnp.maximum(jnp.max(mm, axis=-1, keepdims=True), sink)
            p_win.append(jnp.concatenate([jnp.exp2(t - m).astype(_BF16) for t in (s0, s1, s2)], axis=1))
            p_ctx.append(jnp.concatenate([jnp.exp2(t - m).astype(_BF16) for t in sc], axis=1))
            esink.append(jnp.exp2(sink - m))
        o = _dot(jnp.concatenate(p_win, axis=0), vwin) + _dot(jnp.concatenate(p_ctx, axis=0), vx)
        og = []
        for g in range(GQA_GROUP):
            grow = slice(g * blk, (g + 1) * blk)
            den = o[grow, V7X_LANES:2 * V7X_LANES] + esink[g]
            og.append(o[grow, 0:V7X_LANES] * (1.0 / den))
        for c in range(2):
            ocol = jnp.where(lo, og[2 * c], og[2 * c + 1])
            o_ref[2 * hkv + c, pl.ds(row0, blk), :] = ocol.astype(_BF16)

    s_bufs = (s_even, s_odd)
    scores(0, s_bufs[0])
    for i in range(nblocks):
        if i + 1 < nblocks:
            scores(i + 1, s_bufs[(i + 1) % 2])
        softmax_pv(i, s_bufs[i % 2])


def _attention(sink, q, kd, vd, kx, vx, batch, seq, ctx_len, tq):
    nq_slab, n, _ = q.shape
    nq = seq // tq
    hb = tq // WINDOW
    sb = seq // WINDOW
    last = n // WINDOW - 1
    kv = N_KV_HEADS

    def cur():
        return pl.BlockSpec((kv, tq, V7X_LANES), lambda b, i: (0, b * nq + i, 0))

    def prev():
        return pl.BlockSpec((kv, WINDOW, V7X_LANES), lambda b, i: (0, jnp.maximum(b * sb + i * hb - 1, 0), 0))

    def nxt():
        return pl.BlockSpec((kv, WINDOW, V7X_LANES), lambda b, i: (0, jnp.minimum(b * sb + (i + 1) * hb, last), 0))

    def ctx():
        return pl.BlockSpec((kv, ctx_len, V7X_LANES), lambda b, i: (0, b, 0))

    kernel = functools.partial(_attn_kernel, tq=tq, nq=nq)
    return pl.pallas_call(
        kernel,
        grid=(batch, nq),
        in_specs=[
            pl.BlockSpec(memory_space=pltpu.SMEM),
            pl.BlockSpec((nq_slab, tq, V7X_LANES), lambda b, i: (0, b * nq + i, 0)),
            prev(), cur(), nxt(), prev(), cur(), nxt(), ctx(), ctx(),
        ],
        out_specs=pl.BlockSpec((nq_slab, tq, V7X_LANES), lambda b, i: (0, b * nq + i, 0)),
        out_shape=jax.ShapeDtypeStruct(q.shape, _BF16),
        scratch_shapes=[
            pltpu.VMEM((kv, tq + 2 * WINDOW, V7X_LANES), _BF16),
            pltpu.VMEM((kv, tq + 2 * WINDOW, V7X_LANES), _BF16),
            pltpu.VMEM((GQA_GROUP * WINDOW, 3 * WINDOW + ctx_len), _F32),
            pltpu.VMEM((GQA_GROUP * WINDOW, 3 * WINDOW + ctx_len), _F32),
        ],
        compiler_params=_params(("parallel", "parallel")),
        name="band_attn",
    )(sink, q, kd, kd, kd, vd, vd, vd, kx, vx)


def _gate_weights(r_w, r_b, i_w, i_b):
    heads_per_half = V7X_MXU_DIM // LRU_HEAD_DIM
    eye = jnp.eye(heads_per_half, dtype=_F32)

    def halves(w):
        w = w.reshape(2, D_LRU // V7X_MXU_DIM, heads_per_half, LRU_HEAD_DIM, LRU_HEAD_DIM)
        bd = jnp.einsum('dxhij,hk->dxhikj', w, eye)
        return bd.reshape(2, D_LRU // V7X_MXU_DIM, V7X_MXU_DIM, V7X_MXU_DIM)

    w = (0.5 * jnp.concatenate([halves(r_w), halves(i_w)], axis=-1)).astype(_BF16)
    rb = r_b.reshape(2, D_LRU // V7X_MXU_DIM, 1, V7X_MXU_DIM)
    ib = i_b.reshape(2, D_LRU // V7X_MXU_DIM, 1, V7X_MXU_DIM)
    return w, 0.5 * jnp.concatenate([rb, ib], axis=-1)


def _rope_table(seq):
    grid_rows = seq // GRID_W
    inv_freq = ROPE_BASE ** (-jnp.arange(ROPE_FREQS, dtype=_F32) / ROPE_FREQS)
    pos = jnp.concatenate([jnp.arange(GRID_W), jnp.arange(grid_rows)]).astype(_F32)
    ang = pos[:, None] * inv_freq
    cos, sin = jnp.cos(ang), jnp.sin(ang)
    is_col = (jnp.arange(GRID_W + grid_rows) < GRID_W)[:, None]
    zero = jnp.zeros_like(cos)
    table = jnp.concatenate([jnp.where(is_col, zero, cos), jnp.where(is_col, cos, zero),
                             jnp.where(is_col, zero, sin), jnp.where(is_col, sin, zero)], axis=1)
    return jnp.pad(table, ((0, 0), (0, V7X_LANES - 4 * ROPE_FREQS)))


def kernel(x, c, ctx, c_ctx, ada_w, ada_b, norm_mix_g, norm_ffn_g, norm_final_g, ffn_w_in, ffn_w_out,
           rc_w_in, rc_conv_a_w, rc_conv_a_b, rc_gate_r_w, rc_gate_r_b, rc_gate_i_w, rc_gate_i_b,
           rc_lambda, rc_conv_b_w, rc_w_out, at_w_qkv, at_sink, at_w_out):
    batch, seq, d = x.shape
    ctx_len = ctx.shape[1]
    tm = 1024
    tm_rc = 512
    tm_ffn = 1024
    tq = 1024
    tc = 1024
    assert seq % tm_ffn == 0 and seq % tm_rc == 0 and seq % tm == 0 and seq % tq == 0 and seq % tc == 0
    assert ctx_len % V7X_LANES == 0

    xl = x.reshape(batch * seq, d)
    xc = ctx.reshape(batch * ctx_len, d)

    cond = jnp.concatenate([c_ctx[None], c, jnp.zeros((8 - 1 - batch, d), _F32)], axis=0)
    mod = _modulation(cond, ada_w, ada_b).reshape(ada_w.shape[0], 8, N_MOD, d)

    def lat_row(tile):
        return lambda i: 1 + i // (seq // tile)

    ctx_row = lambda i: 0

    w_gate, b_gate = _gate_weights(rc_gate_r_w[0], rc_gate_r_b[0], rc_gate_i_w[0], rc_gate_i_b[0])
    lam = rc_lambda[0].reshape(2, 1, D_LRU)
    conv_a_b = rc_conv_a_b[0].reshape(1, D_LRU)
    gmix0 = norm_mix_g[0].reshape(1, d)
    gffn0 = norm_ffn_g[0].reshape(1, d)

    xa_c, p_c, w_in = _project(xc, mod[0], gmix0, None, batch * ctx_len, ctx_row, cast_jobs=[(rc_w_in, 0, 0)])
    xa_l, p_l, w1_0, w2_0, w_out0 = _project(
        xl, mod[0], gmix0, w_in, tm, lat_row(tm),
        cast_jobs=[(ffn_w_in, 0, 0), (ffn_w_out, 0, 0), (rc_w_out, 0, 0)])

    h0 = jnp.zeros((batch, 2, V7X_SUBLANES, D_LRU), _F32)
    hf_c, hb_c, h_ctx = _scan(xa_c, h0, rc_conv_a_w[0], conv_a_b, w_gate, b_gate, lam, batch, ctx_len, ctx_len)
    hf_l, hb_l, _ = _scan(xa_l, h_ctx, rc_conv_a_w[0], conv_a_b, w_gate, b_gate, lam, batch, seq, tc)

    (xc,) = _rc_out(xc, hf_c, hb_c, p_c, rc_conv_b_w[0], w_out0, mod[0], gffn0, w1_0, w2_0,
                    ctx_len, ctx_len, ctx_row)
    xl, w1_1, w_qkv, w_out1 = _rc_out(
        xl, hf_l, hb_l, p_l, rc_conv_b_w[0], w_out0, mod[0], gffn0, w1_0, w2_0, tm_rc, seq, lat_row(tm_rc),
        cast_jobs=[(ffn_w_in, 1, 0), (at_w_qkv, 0, (N_Q_HEADS + N_KV_HEADS) * HEAD_DIM), (at_w_out, 0, 0)])

    gmix1 = norm_mix_g[1].reshape(1, d)
    gffn1 = norm_ffn_g[1].reshape(1, d)

    q, kd, vd, w2_1 = _qkv(xl, mod[1], gmix1, w_qkv, _rope_table(seq), tm, seq, lat_row(tm),
                           cast_jobs=[(ffn_w_out, 1, 0)])
    kx, vx = _ctx_kv(xc, mod[1], gmix1, w_qkv)
    o = _attention(at_sink[0], q, kd, vd, kx, vx, batch, seq, ctx_len, tq)

    out = _attn_out(xl, o, w_out1, mod[1], gffn1, w1_1, w2_1,
                    norm_final_g.reshape(1, d), tm_ffn, lat_row(tm_ffn))
    return out.reshape(batch, seq, d)
```

```python
import functools

import jax
import jax.numpy as jnp
from jax import lax
from jax.experimental import pallas as pl
from jax.experimental.pallas import tpu as pltpu

D_MODEL = 1024
N_MOD = 6
EPS = 1e-6
NEG = -1e30
D_LRU = 512
D_SC = 512
LRU_HEADS = 8
LRU_HEAD_DIM = 64
LRU_C = 8.0
RC_IN_WIDTH = 2 * D_LRU + 3 * D_SC
HEAD_DIM = 64
N_Q_HEADS = 16
N_KV_HEADS = 4
GQA_GROUP = 4
WINDOW = 128
GRID_W = 64
ROPE_BASE = 10000.0
ROPE_FREQS = 16
D_FF = 2816
LOG2E = 1.4426950408889634

V7X_LANES = 128
V7X_SUBLANES = 8
V7X_MXU_DIM = 256
V7X_VMEM_LIMIT = 60 * 1024 * 1024

HALO = V7X_SUBLANES
HALO_BF16 = 2 * V7X_SUBLANES
N_SLAB = D_LRU // V7X_LANES
SCAN_SHIFTS = (1, 2, 4)
FFN_CHUNKS = ((0, 1024), (1024, 1024), (2048, 768))
FFN_SUB_ROWS = 512
PROJ_SUB_ROWS = 512

_BF16 = jnp.bfloat16
_F32 = jnp.float32


def _dot(a, b):
    return jnp.dot(a, b, preferred_element_type=_F32)


def _dot_t(a, b):
    return lax.dot_general(a, b, (((1,), (1,)), ((), ())), preferred_element_type=_F32)


def _sigmoid(x):
    return 0.5 * jnp.tanh(0.5 * x) + 0.5


def _gelu_tanh(x):
    return 0.5 * x * (1.0 + jnp.tanh(0.7978845608028654 * (x + 0.044715 * (x * x * x))))


def _rms_mod(x, gain, shift, scale):
    ms = jnp.mean(x * x, axis=-1, keepdims=True)
    return (x * lax.rsqrt(ms + EPS) * gain) * (1.0 + scale) + shift


def _const_spec(shape, index=None):
    idx = (0,) * len(shape) if index is None else index
    return pl.BlockSpec(shape, lambda *_: idx, pipeline_mode=pl.Buffered(1))


def _params(sem):
    return pltpu.CompilerParams(dimension_semantics=sem, vmem_limit_bytes=V7X_VMEM_LIMIT)


def _cast_specs(jobs, nsteps, step_of=lambda i: i):
    ins, outs, shapes = [], [], []
    for w, layer, _ in jobs:
        _, r, c = w.shape
        assert r % (nsteps * HALO_BF16) == 0, (w.shape, nsteps)
        chunk = r // nsteps
        ins.append(pl.BlockSpec((1, chunk, c), lambda *idx, layer=layer: (layer, step_of(*idx), 0)))
        outs.append(pl.BlockSpec((chunk, c), lambda *idx: (step_of(*idx), 0)))
        shapes.append(jax.ShapeDtypeStruct((r, c), _BF16))
    return ins, outs, shapes


def _cast_chunks(rope_cols_per_job, cast_in, cast_out):
    for rope_cols, src, dst in zip(rope_cols_per_job, cast_in, cast_out):
        moves = _rope_lane_moves((src.shape[1], V7X_LANES)) if rope_cols else None
        for c0 in range(0, rope_cols, V7X_LANES):
            dst[:, c0:c0 + V7X_LANES] = _to_rope_lanes(src[0, :, c0:c0 + V7X_LANES], moves).astype(_BF16)
        dst[:, rope_cols:] = src[0, :, rope_cols:].astype(_BF16)


def _mod_kernel(c_ref, w_ref, b_ref, o_ref):
    c = c_ref[...]
    s = (c * _sigmoid(c)).astype(_BF16)
    o_ref[0] = _dot(s, w_ref[0].astype(_BF16)) + b_ref[0]


def _modulation(cond, ada_w, ada_b):
    depth, d, n = ada_w.shape
    bn = 1536
    return pl.pallas_call(
        _mod_kernel,
        grid=(depth, n // bn),
        in_specs=[
            pl.BlockSpec((8, d), lambda l, j: (0, 0)),
            pl.BlockSpec((1, d, bn), lambda l, j: (l, 0, j)),
            pl.BlockSpec((1, 1, bn), lambda l, j: (l, 0, j)),
        ],
        out_specs=pl.BlockSpec((1, 8, bn), lambda l, j: (l, 0, j)),
        out_shape=jax.ShapeDtypeStruct((depth, 8, n), _F32),
        compiler_params=_params(("arbitrary", "arbitrary")),
        name="adaln_mod",
    )(cond, ada_w, ada_b.reshape(depth, 1, n))


def _proj_kernel(*refs, cast_rope, own_weight):
    ncast = len(cast_rope)
    nin = 4 if own_weight else 3
    x_ref, mod_ref, g_ref = refs[:3]
    xa_ref, rest_ref = refs[nin + ncast:nin + 2 + ncast]
    cast_out = refs[nin + 2 + ncast:]
    _cast_chunks(cast_rope, refs[nin:nin + ncast], cast_out)
    w_ref = refs[3] if own_weight else cast_out[0]
    h_all = _rms_mod(x_ref[...], g_ref[...], mod_ref[0, 0:1, :], mod_ref[0, 1:2, :]).astype(_BF16)
    tm = h_all.shape[0]
    sub = min(PROJ_SUB_ROWS, tm)
    for r0 in range(0, tm, sub):
        rows = slice(r0, r0 + sub)
        h = h_all[rows, :]
        xa = _dot(h, w_ref[:, 0:D_LRU])
        for c in range(N_SLAB):
            xa_ref[c, rows, :] = xa[:, c * V7X_LANES:(c + 1) * V7X_LANES]
        ga = _dot(h, w_ref[:, D_LRU:2 * D_LRU])
        rest_ref[rows, 0:D_LRU] = _gelu_tanh(ga).astype(_BF16)
        rest_ref[rows, D_LRU:D_LRU + D_SC] = _dot(h, w_ref[:, 2 * D_LRU:2 * D_LRU + D_SC]).astype(_BF16)
        cv = _dot(h, w_ref[:, 2 * D_LRU + D_SC:2 * D_LRU + 3 * D_SC])
        rest_ref[rows, D_LRU + D_SC:D_LRU + 2 * D_SC] = (cv[:, 0:D_SC] * cv[:, D_SC:2 * D_SC]).astype(_BF16)


def _project(x2d, mod, gain, w, tm, row_of_tile, cast_jobs=()):
    n, d = x2d.shape
    nrest = D_LRU + 2 * D_SC
    assert w is not None or n == tm
    cast_in, cast_out, cast_shapes = _cast_specs(cast_jobs, n // tm)
    weight = [] if w is None else [w]
    return pl.pallas_call(
        functools.partial(_proj_kernel, cast_rope=tuple(job[2] for job in cast_jobs), own_weight=w is not None),
        grid=(n // tm,),
        in_specs=[
            pl.BlockSpec((tm, d), lambda i: (i, 0)),
            pl.BlockSpec((1, N_MOD, d), lambda i: (row_of_tile(i), 0, 0)),
            _const_spec((1, d)),
        ] + [_const_spec(a.shape) for a in weight] + cast_in,
        out_specs=[
            pl.BlockSpec((N_SLAB, tm, V7X_LANES), lambda i: (0, i, 0)),
            pl.BlockSpec((tm, nrest), lambda i: (i, 0)),
        ] + cast_out,
        out_shape=[
            jax.ShapeDtypeStruct((N_SLAB, n, V7X_LANES), _F32),
            jax.ShapeDtypeStruct((n, nrest), _BF16),
        ] + cast_shapes,
        compiler_params=_params(("arbitrary",)),
        name="rc_in_proj",
    )(x2d, mod, gain, *weight, *[job[0] for job in cast_jobs])


def _scan_kernel(xm_f, xp_f, xn_f, xm_b, xp_b, xn_b, cw_ref, cb_ref, wg_ref, bg_ref, lam_ref, h0_ref,
                 hf_ref, hb_ref, hl_ref, xs_scr, carry_scr, *level_scr, nc, tc):
    j = pl.program_id(1)
    lo = HALO
    ngroups = tc // V7X_SUBLANES
    nlev = len(SCAN_SHIFTS)
    a_lv = list(level_scr[:nlev])
    b_lv = list(level_scr[nlev:])
    pad_rows = (slice(0, HALO), slice(lo + tc, lo + tc + HALO))

    @pl.when(j == 0)
    def _():
        carry_scr[...] = h0_ref[0]
        for d in range(2):
            for ref in a_lv:
                ref[d, :, pad_rows[d], :] = jnp.ones((N_SLAB, HALO, V7X_LANES), _F32)
            for ref in b_lv:
                ref[d, :, pad_rows[d], :] = jnp.zeros((N_SLAB, HALO, V7X_LANES), _F32)

    cw = cw_ref[...]
    cb = cb_ref[...]

    def coeffs(d, xm, xp, xn, has_prev, has_next):
        xs_scr[d, :, 0:HALO, :] = jnp.where(has_prev, xp[...], 0.0)
        xs_scr[d, :, lo:lo + tc, :] = xm[...]
        xs_scr[d, :, lo + tc:lo + tc + HALO, :] = jnp.where(has_next, xn[...], 0.0)
        xcs = []
        for c in range(N_SLAB):
            lanes = slice(c * V7X_LANES, (c + 1) * V7X_LANES)
            acc = cb[:, lanes]
            for k in range(4):
                acc = acc + cw[k:k + 1, lanes] * xs_scr[d, c, lo - 2 + k:lo - 2 + k + tc, :]
            xcs.append(acc)
        lam = lam_ref[d]
        nlam = -lam
        softplus = jnp.maximum(nlam, 0.0) + jnp.log(1.0 + jnp.exp(-jnp.abs(nlam)))
        hrate = (-0.5 * LRU_C * LOG2E) * softplus
        half = V7X_MXU_DIM
        per_half = half // V7X_LANES
        for hh in range(D_LRU // half):
            xch = jnp.concatenate(xcs[hh * per_half:(hh + 1) * per_half], axis=1)
            z = _dot(xch.astype(_BF16), wg_ref[d, hh]) + bg_ref[d, hh]
            for cc in range(per_half):
                c = hh * per_half + cc
                lanes = slice(c * V7X_LANES, (c + 1) * V7X_LANES)
                tr = jnp.tanh(z[:, cc * V7X_LANES:(cc + 1) * V7X_LANES])
                ig = 0.5 * jnp.tanh(z[:, half + cc * V7X_LANES:half + (cc + 1) * V7X_LANES]) + 0.5
                a = jnp.exp2(tr * hrate[:, lanes] + hrate[:, lanes])
                one_m_a2 = jnp.maximum(1.0 - a * a, 1e-12)
                mult = one_m_a2 * lax.rsqrt(one_m_a2)
                a_lv[0][d, c, lo:lo + tc, :] = a
                b_lv[0][d, c, lo:lo + tc, :] = mult * ig * xcs[c]

    coeffs(0, xm_f, xp_f, xn_f, j > 0, j < nc - 1)
    coeffs(1, xm_b, xp_b, xn_b, j < nc - 1, j > 0)

    def step(g, carry):
        out = []
        for d, h_ref in ((0, hf_ref), (1, hb_ref)):
            grp = g if d == 0 else ngroups - 1 - g
            row0 = pl.multiple_of(lo + grp * V7X_SUBLANES, V7X_SUBLANES)
            cur = pl.ds(row0, V7X_SUBLANES)
            for c in range(N_SLAB):
                a = a_lv[0][d, c, cur, :]
                b = b_lv[0][d, c, cur, :]
                for lvl, shift in enumerate(SCAN_SHIFTS):
                    sh = pl.ds(row0 + (shift if d else -shift), V7X_SUBLANES)
                    b = a * b_lv[lvl][d, c, sh, :] + b
                    a = a * a_lv[lvl][d, c, sh, :]
                    if lvl + 1 < nlev:
                        a_lv[lvl + 1][d, c, cur, :] = a
                        b_lv[lvl + 1][d, c, cur, :] = b
                h = a * carry[d * N_SLAB + c] + b
                out_rows = pl.ds(pl.multiple_of(grp * V7X_SUBLANES, V7X_SUBLANES), V7X_SUBLANES)
                h_ref[out_rows, c * V7X_LANES:(c + 1) * V7X_LANES] = h
                out.append(h)
        return tuple(out)

    init = tuple(carry_scr[d, :, c * V7X_LANES:(c + 1) * V7X_LANES]
                 for d in range(2) for c in range(N_SLAB))
    carry = lax.fori_loop(0, ngroups, step, init, unroll=4)

    edge_row = (V7X_SUBLANES - 1, 0)
    data_edge = (slice(tc, tc + HALO), slice(lo, lo + HALO))
    for d in range(2):
        for c in range(N_SLAB):
            lanes = slice(c * V7X_LANES, (c + 1) * V7X_LANES)
            h = carry[d * N_SLAB + c]
            carry_scr[d, :, lanes] = h
            hl_ref[0, d, :, lanes] = jnp.broadcast_to(h[edge_row[d]:edge_row[d] + 1, :], h.shape)
        for ref in a_lv + b_lv:
            ref[d, :, pad_rows[d], :] = ref[d, :, data_edge[d], :]


def _scan(xa, h0, conv_w, conv_b, w_gate, b_gate, lam, batch, seq, tc):
    n = xa.shape[1]
    nc = seq // tc
    tb = tc // HALO
    sb = seq // HALO
    last = n // HALO - 1

    def fwd(b, j):
        return j

    def bwd(b, j):
        return nc - 1 - j

    def main(cf):
        return pl.BlockSpec((N_SLAB, tc, V7X_LANES), lambda b, j: (0, b * nc + cf(b, j), 0))

    def prev(cf):
        return pl.BlockSpec((N_SLAB, HALO, V7X_LANES),
                            lambda b, j: (0, jnp.maximum(b * sb + cf(b, j) * tb - 1, 0), 0))

    def nxt(cf):
        return pl.BlockSpec((N_SLAB, HALO, V7X_LANES),
                            lambda b, j: (0, jnp.minimum(b * sb + (cf(b, j) + 1) * tb, last), 0))

    kernel = functools.partial(_scan_kernel, nc=nc, tc=tc)
    nlev = len(SCAN_SHIFTS)
    return pl.pallas_call(
        kernel,
        grid=(batch, nc),
        in_specs=[
            main(fwd), prev(fwd), nxt(fwd), main(bwd), prev(bwd), nxt(bwd),
            _const_spec((4, D_LRU)),
            _const_spec((1, D_LRU)),
            _const_spec(w_gate.shape),
            _const_spec(b_gate.shape),
            _const_spec((2, 1, D_LRU)),
            pl.BlockSpec((1, 2, V7X_SUBLANES, D_LRU), lambda b, j: (b, 0, 0, 0)),
        ],
        out_specs=[
            pl.BlockSpec((tc, D_LRU), lambda b, j: (b * nc + j, 0)),
            pl.BlockSpec((tc, D_LRU), lambda b, j: (b * nc + nc - 1 - j, 0)),
            pl.BlockSpec((1, 2, V7X_SUBLANES, D_LRU), lambda b, j: (b, 0, 0, 0)),
        ],
        out_shape=[
            jax.ShapeDtypeStruct((n, D_LRU), _F32),
            jax.ShapeDtypeStruct((n, D_LRU), _F32),
            jax.ShapeDtypeStruct((batch, 2, V7X_SUBLANES, D_LRU), _F32),
        ],
        scratch_shapes=[
            pltpu.VMEM((2, N_SLAB, tc + 2 * HALO, V7X_LANES), _F32),
            pltpu.VMEM((2, V7X_SUBLANES, D_LRU), _F32),
        ] + [pltpu.VMEM((2, N_SLAB, tc + 2 * HALO, V7X_LANES), _F32) for _ in range(2 * nlev)],
        compiler_params=_params(("arbitrary", "arbitrary")),
        name="rglru_scan",
    )(xa, xa, xa, xa, xa, xa, conv_w, conv_b, w_gate, b_gate, lam, h0)


def _ffn_norm(x, y, mod_ref, gffn_ref):
    x1 = x + mod_ref[0, 2:3, :] * y
    h = _rms_mod(x1, gffn_ref[...], mod_ref[0, 3:4, :], mod_ref[0, 4:5, :]).astype(_BF16)
    return x1, h


def _ffn_apply(x1, h, mod_ref, w1_ref, w2_ref, gfin_ref):
    acc = None
    for c0, cw in FFN_CHUNKS:
        g = _dot(h, w1_ref[:, c0:c0 + cw])
        u = _dot(h, w1_ref[:, D_FF + c0:D_FF + c0 + cw])
        act = (g * _sigmoid(g) * u).astype(_BF16)
        part = _dot(act, w2_ref[c0:c0 + cw, :])
        acc = part if acc is None else acc + part
    x2 = x1 + mod_ref[0, 5:6, :] * acc
    if gfin_ref is not None:
        ms = jnp.mean(x2 * x2, axis=-1, keepdims=True)
        x2 = x2 * lax.rsqrt(ms + EPS) * gfin_ref[...]
    return x2


def _ffn_weight_specs(w1, w2):
    return [_const_spec(w1.shape), _const_spec(w2.shape)]


def _rc_out_kernel(x_ref, hf_ref, hb_ref, gg_ref, bg_ref, um_ref, up_ref, un_ref,
                   xc_ref, hfc_ref, hbc_ref, pc_ref,
                   cbw_ref, wo_ref, mod_ref, gffn_ref, w1_ref, w2_ref, o_ref, *, nt, tm, n_lat, ctx_len):
    i = pl.program_id(0)
    is_ctx = i == n_lat
    pick = lambda lat, ctx: jnp.where(is_ctx, ctx, lat)
    has_prev = jnp.logical_and((i % nt) > 0, jnp.logical_not(is_ctx))
    has_next = jnp.logical_and((i % nt) < nt - 1, jnp.logical_not(is_ctx))
    prev = jnp.where(has_prev, up_ref[...].astype(_F32), 0.0)
    nxt = jnp.where(has_next, un_ref[...].astype(_F32), 0.0)
    u_mid = pick(um_ref[...], pc_ref[:, 2 * D_SC:3 * D_SC]).astype(_F32)
    uext = jnp.concatenate([prev, u_mid, nxt], axis=0)
    cbw = cbw_ref[...]
    sub = min(FFN_SUB_ROWS, tm // 2)
    normed = []
    for r0 in range(0, tm, sub):
        rows = slice(r0, r0 + sub)
        hsum = pick(hf_ref[rows, :], hfc_ref[rows, :]) + pick(hb_ref[rows, :], hbc_ref[rows, :])
        ya = (hsum * pick(gg_ref[rows, :], pc_ref[rows, 0:D_LRU]).astype(_F32)).astype(_BF16)
        seq_row = (lax.broadcasted_iota(jnp.int32, (sub, 1), 0) + r0) % ctx_len
        tap_ok = (jnp.logical_or(jnp.logical_not(is_ctx), seq_row > 0), None,
                  jnp.logical_or(jnp.logical_not(is_ctx), seq_row < ctx_len - 1))
        conv = None
        for k in range(3):
            first = HALO_BF16 - 1 + k + r0
            term = cbw[k:k + 1, :] * uext[first:first + sub, :]
            if tap_ok[k] is not None:
                term = jnp.where(tap_ok[k], term, 0.0)
            conv = term if conv is None else conv + term
        yb = (pick(bg_ref[rows, :], pc_ref[rows, D_LRU:D_LRU + D_SC]).astype(_F32) * conv).astype(_BF16)
        y = _dot(ya, wo_ref[0:D_LRU, :]) + _dot(yb, wo_ref[D_LRU:D_LRU + D_SC, :])
        normed.append(_ffn_norm(pick(x_ref[rows, :], xc_ref[rows, :]), y, mod_ref, gffn_ref))
    for k, r0 in enumerate(range(0, tm, sub)):
        x1, h = normed[k]
        o_ref[r0:r0 + sub, :] = _ffn_apply(x1, h, mod_ref, w1_ref, w2_ref, None)


def _rc_out(x2d, hf, hb, p, xc2d, hf_c, hb_c, p_c, conv_b_w, w_out, mod, gffn, w1, w2, tm, seq, ctx_len):
    n, d = x2d.shape
    assert xc2d.shape[0] == tm and tm % ctx_len == 0
    nt = seq // tm
    n_lat = n // tm
    tb = tm // HALO_BF16
    last = n // HALO_BF16 - 1
    lat = lambda i: jnp.minimum(i, n_lat - 1)

    def col(c):
        return pl.BlockSpec((tm, D_SC), lambda i: (lat(i), c))

    kernel = functools.partial(_rc_out_kernel, nt=nt, tm=tm, n_lat=n_lat, ctx_len=ctx_len)
    return pl.pallas_call(
        kernel,
        grid=(n_lat + 1,),
        in_specs=[
            pl.BlockSpec((tm, d), lambda i: (lat(i), 0)),
            pl.BlockSpec((tm, D_LRU), lambda i: (lat(i), 0)),
            pl.BlockSpec((tm, D_LRU), lambda i: (lat(i), 0)),
            col(0), col(1), col(2),
            pl.BlockSpec((HALO_BF16, D_SC), lambda i: (jnp.maximum(lat(i) * tb - 1, 0), 2)),
            pl.BlockSpec((HALO_BF16, D_SC), lambda i: (jnp.minimum((lat(i) + 1) * tb, last), 2)),
            _const_spec(xc2d.shape), _const_spec(hf_c.shape), _const_spec(hb_c.shape), _const_spec(p_c.shape),
            _const_spec((3, D_SC)),
            _const_spec(w_out.shape),
            pl.BlockSpec((1, N_MOD, d), lambda i: (jnp.where(i == n_lat, 0, 1 + i // nt), 0, 0)),
            _const_spec((1, d)),
        ] + _ffn_weight_specs(w1, w2),
        out_specs=pl.BlockSpec((tm, d), lambda i: (i, 0)),
        out_shape=jax.ShapeDtypeStruct((n + tm, d), _F32),
        compiler_params=_params(("arbitrary",)),
        name="rc_out_ffn",
    )(x2d, hf, hb, p, p, p, p, p, xc2d, hf_c, hb_c, p_c, conv_b_w, w_out, mod, gffn, w1, w2)


def _attn_out_kernel(x_ref, o_in_ref, wo_ref, mod_ref, gffn_ref, gfin_ref, w1_ref, w2_ref, o_ref):
    tm = x_ref.shape[0]
    sub = min(FFN_SUB_ROWS, tm)
    normed = []
    for r0 in range(0, tm, sub):
        rows = slice(r0, r0 + sub)
        o_in = jnp.concatenate([o_in_ref[c, rows, :] for c in range(o_in_ref.shape[0])], axis=1)
        normed.append(_ffn_norm(x_ref[rows, :], _dot(o_in, wo_ref[...]), mod_ref, gffn_ref))
    for k, r0 in enumerate(range(0, tm, sub)):
        x1, h = normed[k]
        o_ref[r0:r0 + sub, :] = _ffn_apply(x1, h, mod_ref, w1_ref, w2_ref, gfin_ref)


def _attn_out(x2d, n, o, w_out, mod, gffn, w1, w2, gfin, tm, row_of_tile):
    d = x2d.shape[1]
    return pl.pallas_call(
        _attn_out_kernel,
        grid=(n // tm,),
        in_specs=[
            pl.BlockSpec((tm, d), lambda i: (i, 0)),
            pl.BlockSpec((o.shape[0], tm, V7X_LANES), lambda i: (0, i, 0)),
            _const_spec(w_out.shape),
            pl.BlockSpec((1, N_MOD, d), lambda i: (row_of_tile(i), 0, 0)),
            _const_spec((1, d)),
            _const_spec((1, d)),
        ] + _ffn_weight_specs(w1, w2),
        out_specs=pl.BlockSpec((tm, d), lambda i: (i, 0)),
        out_shape=jax.ShapeDtypeStruct((n, d), _F32),
        compiler_params=_params(("parallel",)),
        name="attn_out_ffn",
    )(x2d, o, w_out, mod, gffn, gfin, w1, w2)


ROPE_HALF = V7X_LANES // 2
ROPE_HEAD_SPAN = ROPE_HALF // 2


def _rope_lane_moves(shape):
    lane = lax.broadcasted_iota(jnp.int32, shape, 1)
    dst = lane // ROPE_FREQS
    src = ((dst % 4) // 2) * 4 + (dst % 2) * 2 + dst // 4
    return {delta: (src - dst) == delta for delta in range(-3, 4) if delta != 0}


def _to_rope_lanes(x, moves):
    out = x
    for delta, mask in moves.items():
        shifted = pltpu.roll(x, (-delta * ROPE_FREQS) % V7X_LANES, 1)
        out = jnp.where(mask, shifted, out)
    return out


def _rope(x, cos, sin_signed):
    return x * cos + pltpu.roll(x, ROPE_HALF, 1) * sin_signed


def _rope_lane_tables(cs):
    lane = lax.broadcasted_iota(jnp.int32, cs.shape, 1)
    span = ROPE_HEAD_SPAN
    cos = jnp.where(lane < span, cs, pltpu.roll(cs, span, 1))
    sin = jnp.where(lane < span, pltpu.roll(cs, V7X_LANES - span, 1), cs)
    first = lane < ROPE_HALF
    cos = jnp.where(first, cos, pltpu.roll(cos, ROPE_HALF, 1))
    sin = jnp.where(first, sin, pltpu.roll(sin, ROPE_HALF, 1))
    return cos, jnp.where(first, -sin, sin)


def _key_heads_to_slabs(x, ref, c, rows):
    lane = lax.broadcasted_iota(jnp.int32, x.shape, 1)
    is_first = (lane % ROPE_HALF) < ROPE_HEAD_SPAN
    ref[2 * c, rows, :] = jnp.where(is_first, x, pltpu.roll(x, ROPE_HEAD_SPAN, 1)).astype(_BF16)
    ref[2 * c + 1, rows, :] = jnp.where(is_first, pltpu.roll(x, V7X_LANES - ROPE_HEAD_SPAN, 1), x).astype(_BF16)


def _heads_to_slabs(x, ref, c, rows):
    first = lax.broadcasted_iota(jnp.int32, x.shape, 1) < HEAD_DIM
    swapped = pltpu.roll(x, HEAD_DIM, 1)
    ref[2 * c, rows, :] = jnp.where(first, x, swapped).astype(_BF16)
    ref[2 * c + 1, rows, :] = jnp.where(first, swapped, x).astype(_BF16)


def _qkv_kernel(*refs, nt, tm, cast_rope):
    ncast = len(cast_rope)
    x_ref, mod_ref, g_ref, w_ref, tab_ref = refs[:5]
    q_ref, k_ref, v_ref = refs[5 + ncast:8 + ncast]
    _cast_chunks(cast_rope, refs[5:5 + ncast], refs[8 + ncast:])
    h = _rms_mod(x_ref[...], g_ref[...], mod_ref[0, 0:1, :], mod_ref[0, 1:2, :]).astype(_BF16)
    col_terms = tab_ref[0:GRID_W, :]
    first_grid_row = GRID_W + (pl.program_id(0) % nt) * (tm // GRID_W)
    cs = jnp.concatenate([col_terms + tab_ref[pl.ds(first_grid_row + g, 1), :] for g in range(tm // GRID_W)],
                         axis=0)
    cos_t, sin_t = _rope_lane_tables(cs)
    dq = N_Q_HEADS * HEAD_DIM
    dkv = N_KV_HEADS * HEAD_DIM
    scale = HEAD_DIM ** -0.5 * LOG2E
    sub = min(PROJ_SUB_ROWS, tm)
    for r0 in range(0, tm, sub):
        rows = slice(r0, r0 + sub)
        hs, cos, sin = h[rows, :], cos_t[rows, :], sin_t[rows, :]
        q = _dot(hs, w_ref[:, 0:dq])
        for c in range(dq // V7X_LANES):
            cols = slice(c * V7X_LANES, (c + 1) * V7X_LANES)
            q_ref[c, rows, :] = (_rope(q[:, cols], cos, sin) * scale).astype(_BF16)
        k = _dot(hs, w_ref[:, dq:dq + dkv])
        v = _dot(hs, w_ref[:, dq + dkv:dq + 2 * dkv])
        for c in range(dkv // V7X_LANES):
            cols = slice(c * V7X_LANES, (c + 1) * V7X_LANES)
            _key_heads_to_slabs(_rope(k[:, cols], cos, sin), k_ref, c, rows)
            _heads_to_slabs(v[:, cols], v_ref, c, rows)


def _qkv(x2d, n, mod, gain, w, cs_t, tm, seq, row_of_tile, cast_jobs=()):
    d = x2d.shape[1]
    nt = seq // tm
    nq_slab = N_Q_HEADS * HEAD_DIM // V7X_LANES
    assert tm % GRID_W == 0
    cast_in, cast_out, cast_shapes = _cast_specs(cast_jobs, n // tm)
    return pl.pallas_call(
        functools.partial(_qkv_kernel, nt=nt, tm=tm, cast_rope=tuple(job[2] for job in cast_jobs)),
        grid=(n // tm,),
        in_specs=[
            pl.BlockSpec((tm, d), lambda i: (i, 0)),
            pl.BlockSpec((1, N_MOD, d), lambda i: (row_of_tile(i), 0, 0)),
            _const_spec((1, d)),
            _const_spec(w.shape),
            _const_spec(cs_t.shape),
        ] + cast_in,
        out_specs=[
            pl.BlockSpec((nq_slab, tm, V7X_LANES), lambda i: (0, i, 0)),
            pl.BlockSpec((N_KV_HEADS, tm, V7X_LANES), lambda i: (0, i, 0)),
            pl.BlockSpec((N_KV_HEADS, tm, V7X_LANES), lambda i: (0, i, 0)),
        ] + cast_out,
        out_shape=[
            jax.ShapeDtypeStruct((nq_slab, n, V7X_LANES), _BF16),
            jax.ShapeDtypeStruct((N_KV_HEADS, n, V7X_LANES), _BF16),
            jax.ShapeDtypeStruct((N_KV_HEADS, n, V7X_LANES), _BF16),
        ] + cast_shapes,
        compiler_params=_params(("parallel",)),
        name="qkv_rope",
    )(x2d, mod, gain, w, cs_t, *[job[0] for job in cast_jobs])


def _kv_kernel(x_ref, mod_ref, g_ref, w_ref, k_ref, v_ref):
    h = _rms_mod(x_ref[...], g_ref[...], mod_ref[0, 0:1, :], mod_ref[0, 1:2, :]).astype(_BF16)
    dq = N_Q_HEADS * HEAD_DIM
    dkv = N_KV_HEADS * HEAD_DIM
    k = _dot(h, w_ref[:, dq:dq + dkv])
    v = _dot(h, w_ref[:, dq + dkv:dq + 2 * dkv])
    for c in range(dkv // V7X_LANES):
        cols = slice(c * V7X_LANES, (c + 1) * V7X_LANES)
        _key_heads_to_slabs(k[:, cols], k_ref, c, slice(None))
        _heads_to_slabs(v[:, cols], v_ref, c, slice(None))


def _ctx_kv(x2d, first_row, n, mod, gain, w):
    d = x2d.shape[1]
    assert first_row % n == 0
    slab = (N_KV_HEADS, n, V7X_LANES)
    return pl.pallas_call(
        _kv_kernel,
        grid=(1,),
        in_specs=[
            pl.BlockSpec((n, d), lambda i: (first_row // n, 0)),
            pl.BlockSpec((1, N_MOD, d), lambda i: (0, 0, 0)),
            _const_spec((1, d)),
            _const_spec(w.shape),
        ],
        out_specs=[pl.BlockSpec(slab, lambda i: (0, 0, 0)), pl.BlockSpec(slab, lambda i: (0, 0, 0))],
        out_shape=[jax.ShapeDtypeStruct(slab, _BF16), jax.ShapeDtypeStruct(slab, _BF16)],
        compiler_params=_params(("arbitrary",)),
        name="ctx_kv",
    )(x2d, mod, gain, w)


def _attn_kernel(*refs, tq, nq, cast_rope):
    ncast = len(cast_rope)
    sink_ref, q_ref, kp_ref, kc_ref, kn_ref, vp_ref, vc_ref, vn_ref, kx_ref, vx_ref = refs[:10]
    o_ref = refs[10 + ncast]
    kext, vext, s_even, s_odd = refs[11 + 2 * ncast:]
    _cast_chunks(cast_rope, refs[10:10 + ncast], refs[11 + ncast:11 + 2 * ncast])
    n = pl.program_id(1)
    blk = WINDOW
    nsub = tq // blk
    nblocks = nsub * N_KV_HEADS
    ctx_len = kx_ref.shape[1]
    kext[:, 0:blk, :] = kp_ref[...]
    kext[:, blk:blk + tq, :] = kc_ref[...]
    kext[:, blk + tq:2 * blk + tq, :] = kn_ref[...]
    vext[:, 0:blk, :] = vp_ref[...]
    vext[:, blk:blk + tq, :] = vc_ref[...]
    vext[:, blk + tq:2 * blk + tq, :] = vn_ref[...]
    qi = lax.broadcasted_iota(jnp.int32, (blk, blk), 0)
    kj = lax.broadcasted_iota(jnp.int32, (blk, blk), 1)
    lane = lax.broadcasted_iota(jnp.int32, (blk, V7X_LANES), 1)
    lo = lane < HEAD_DIM
    q_first = (lane % ROPE_HALF) < ROPE_HEAD_SPAN
    ones_win = jnp.ones((3 * blk, V7X_LANES), _BF16)
    ones_ctx = jnp.ones((ctx_len, V7X_LANES), _BF16)

    def locate(i):
        j = i // N_KV_HEADS
        return j, i % N_KV_HEADS, j * blk

    def scores(i, s_ref):
        _, hkv, row0 = locate(i)
        parts = []
        for c in range(2):
            qc = q_ref[2 * hkv + c, pl.ds(row0, blk), :]
            parts.append(jnp.where(q_first, qc, jnp.zeros_like(qc)))
            parts.append(jnp.where(q_first, jnp.zeros_like(qc), qc))
        lhs = jnp.concatenate(parts, axis=0)
        s_ref[:, 0:3 * blk] = _dot_t(lhs, kext[hkv, pl.ds(row0, 3 * blk), :])
        s_ref[:, 3 * blk:3 * blk + ctx_len] = _dot_t(lhs, kx_ref[hkv])

    def softmax_pv(i, s_ref):
        j, hkv, row0 = locate(i)
        keep_prev = kj >= qi
        keep_next = kj <= qi
        if j == 0:
            keep_prev = keep_prev & (n > 0)
        if j == nsub - 1:
            keep_next = keep_next & (n < nq - 1)
        vwin = jnp.concatenate([vext[hkv, pl.ds(row0, 3 * blk), :], ones_win], axis=1)
        vx = jnp.concatenate([vx_ref[hkv], ones_ctx], axis=1)
        p_win, p_ctx, esink = [], [], []
        for g in range(GQA_GROUP):
            sink = sink_ref[hkv * GQA_GROUP + g] * LOG2E
            grow = slice(g * blk, (g + 1) * blk)
            s0 = jnp.where(keep_prev, s_ref[grow, 0:blk], NEG)
            s1 = s_ref[grow, blk:2 * blk]
            s2 = jnp.where(keep_next, s_ref[grow, 2 * blk:3 * blk], NEG)
            sc = [s_ref[grow, 3 * blk + t * V7X_LANES:3 * blk + (t + 1) * V7X_LANES]
                  for t in range(ctx_len // V7X_LANES)]
            mm = jnp.maximum(jnp.maximum(s0, s1), s2)
            for t in sc:
                mm = jnp.maximum(mm, t)
            m = jnp.maximum(jnp.max(mm, axis=-1, keepdims=True), sink)
            p_win.append(jnp.concatenate([jnp.exp2(t - m).astype(_BF16) for t in (s0, s1, s2)], axis=1))
            p_ctx.append(jnp.concatenate([jnp.exp2(t - m).astype(_BF16) for t in sc], axis=1))
            esink.append(jnp.exp2(sink - m))
        o = _dot(jnp.concatenate(p_win, axis=0), vwin) + _dot(jnp.concatenate(p_ctx, axis=0), vx)
        og = []
        for g in range(GQA_GROUP):
            grow = slice(g * blk, (g + 1) * blk)
            den = o[grow, V7X_LANES:2 * V7X_LANES] + esink[g]
            og.append(o[grow, 0:V7X_LANES] * (1.0 / den))
        for c in range(2):
            ocol = jnp.where(lo, og[2 * c], og[2 * c + 1])
            o_ref[2 * hkv + c, pl.ds(row0, blk), :] = ocol.astype(_BF16)

    s_bufs = (s_even, s_odd)
    scores(0, s_bufs[0])
    for i in range(nblocks):
        if i + 1 < nblocks:
            scores(i + 1, s_bufs[(i + 1) % 2])
        softmax_pv(i, s_bufs[i % 2])


def _attention(sink, q, kd, vd, kx, vx, batch, seq, ctx_len, tq, cast_jobs=()):
    nq_slab, n, _ = q.shape
    nq = seq // tq
    hb = tq // WINDOW
    sb = seq // WINDOW
    last = n // WINDOW - 1
    kv = N_KV_HEADS

    def cur():
        return pl.BlockSpec((kv, tq, V7X_LANES), lambda b, i: (0, b * nq + i, 0))

    def prev():
        return pl.BlockSpec((kv, WINDOW, V7X_LANES), lambda b, i: (0, jnp.maximum(b * sb + i * hb - 1, 0), 0))

    def nxt():
        return pl.BlockSpec((kv, WINDOW, V7X_LANES), lambda b, i: (0, jnp.minimum(b * sb + (i + 1) * hb, last), 0))

    def ctx():
        return pl.BlockSpec((kv, ctx_len, V7X_LANES), lambda b, i: (0, b, 0))

    cast_in, cast_out, cast_shapes = _cast_specs(cast_jobs, batch * nq, lambda b, i: b * nq + i)
    kernel = functools.partial(_attn_kernel, tq=tq, nq=nq, cast_rope=tuple(job[2] for job in cast_jobs))
    return pl.pallas_call(
        kernel,
        grid=(batch, nq),
        in_specs=[
            pl.BlockSpec(memory_space=pltpu.SMEM),
            pl.BlockSpec((nq_slab, tq, V7X_LANES), lambda b, i: (0, b * nq + i, 0)),
            prev(), cur(), nxt(), prev(), cur(), nxt(), ctx(), ctx(),
        ] + cast_in,
        out_specs=[pl.BlockSpec((nq_slab, tq, V7X_LANES), lambda b, i: (0, b * nq + i, 0))] + cast_out,
        out_shape=[jax.ShapeDtypeStruct(q.shape, _BF16)] + cast_shapes,
        scratch_shapes=[
            pltpu.VMEM((kv, tq + 2 * WINDOW, V7X_LANES), _BF16),
            pltpu.VMEM((kv, tq + 2 * WINDOW, V7X_LANES), _BF16),
            pltpu.VMEM((GQA_GROUP * WINDOW, 3 * WINDOW + ctx_len), _F32),
            pltpu.VMEM((GQA_GROUP * WINDOW, 3 * WINDOW + ctx_len), _F32),
        ],
        compiler_params=_params(("parallel", "parallel")),
        name="band_attn",
    )(sink, q, kd, kd, kd, vd, vd, vd, kx, vx, *[job[0] for job in cast_jobs])


def _gate_weights(r_w, r_b, i_w, i_b):
    heads_per_half = V7X_MXU_DIM // LRU_HEAD_DIM
    eye = jnp.eye(heads_per_half, dtype=_F32)

    def halves(w):
        w = w.reshape(2, D_LRU // V7X_MXU_DIM, heads_per_half, LRU_HEAD_DIM, LRU_HEAD_DIM)
        bd = jnp.einsum('dxhij,hk->dxhikj', w, eye)
        return bd.reshape(2, D_LRU // V7X_MXU_DIM, V7X_MXU_DIM, V7X_MXU_DIM)

    w = (0.5 * jnp.concatenate([halves(r_w), halves(i_w)], axis=-1)).astype(_BF16)
    rb = r_b.reshape(2, D_LRU // V7X_MXU_DIM, 1, V7X_MXU_DIM)
    ib = i_b.reshape(2, D_LRU // V7X_MXU_DIM, 1, V7X_MXU_DIM)
    return w, 0.5 * jnp.concatenate([rb, ib], axis=-1)


def _rope_table(seq):
    grid_rows = seq // GRID_W
    inv_freq = ROPE_BASE ** (-jnp.arange(ROPE_FREQS, dtype=_F32) / ROPE_FREQS)
    pos = jnp.concatenate([jnp.arange(GRID_W), jnp.arange(grid_rows)]).astype(_F32)
    ang = pos[:, None] * inv_freq
    cos, sin = jnp.cos(ang), jnp.sin(ang)
    is_col = (jnp.arange(GRID_W + grid_rows) < GRID_W)[:, None]
    zero = jnp.zeros_like(cos)
    table = jnp.concatenate([jnp.where(is_col, zero, cos), jnp.where(is_col, cos, zero),
                             jnp.where(is_col, zero, sin), jnp.where(is_col, sin, zero)], axis=1)
    return jnp.pad(table, ((0, 0), (0, V7X_LANES - 4 * ROPE_FREQS)))


def kernel(x, c, ctx, c_ctx, ada_w, ada_b, norm_mix_g, norm_ffn_g, norm_final_g, ffn_w_in, ffn_w_out,
           rc_w_in, rc_conv_a_w, rc_conv_a_b, rc_gate_r_w, rc_gate_r_b, rc_gate_i_w, rc_gate_i_b,
           rc_lambda, rc_conv_b_w, rc_w_out, at_w_qkv, at_sink, at_w_out):
    batch, seq, d = x.shape
    ctx_len = ctx.shape[1]
    tm = 1024
    tm_rc = 512
    tm_ffn = 1024
    tq = 1024
    tc = 1024
    assert seq % tm_ffn == 0 and seq % tm_rc == 0 and seq % tm == 0 and seq % tq == 0 and seq % tc == 0
    assert ctx_len % V7X_LANES == 0

    xl = x.reshape(batch * seq, d)
    xc = ctx.reshape(batch * ctx_len, d)

    cond = jnp.concatenate([c_ctx[None], c, jnp.zeros((8 - 1 - batch, d), _F32)], axis=0)
    mod = _modulation(cond, ada_w, ada_b).reshape(ada_w.shape[0], 8, N_MOD, d)

    def lat_row(tile):
        return lambda i: 1 + i // (seq // tile)

    ctx_row = lambda i: 0

    w_gate, b_gate = _gate_weights(rc_gate_r_w[0], rc_gate_r_b[0], rc_gate_i_w[0], rc_gate_i_b[0])
    lam = rc_lambda[0].reshape(2, 1, D_LRU)
    conv_a_b = rc_conv_a_b[0].reshape(1, D_LRU)
    gmix0 = norm_mix_g[0].reshape(1, d)
    gffn0 = norm_ffn_g[0].reshape(1, d)

    xa_c, p_c, w_in = _project(xc, mod[0], gmix0, None, batch * ctx_len, ctx_row, cast_jobs=[(rc_w_in, 0, 0)])
    xa_l, p_l, w1_0, w2_0, w_out0, w_qkv = _project(
        xl, mod[0], gmix0, w_in, tm, lat_row(tm),
        cast_jobs=[(ffn_w_in, 0, 0), (ffn_w_out, 0, 0), (rc_w_out, 0, 0),
                   (at_w_qkv, 0, (N_Q_HEADS + N_KV_HEADS) * HEAD_DIM)])

    h0 = jnp.zeros((batch, 2, V7X_SUBLANES, D_LRU), _F32)
    hf_c, hb_c, h_ctx = _scan(xa_c, h0, rc_conv_a_w[0], conv_a_b, w_gate, b_gate, lam, batch, ctx_len, ctx_len)
    hf_l, hb_l, _ = _scan(xa_l, h_ctx, rc_conv_a_w[0], conv_a_b, w_gate, b_gate, lam, batch, seq, tc)

    n_lat = batch * seq
    x_all = _rc_out(xl, hf_l, hb_l, p_l, xc, hf_c, hb_c, p_c, rc_conv_b_w[0], w_out0, mod[0], gffn0,
                    w1_0, w2_0, tm_rc, seq, ctx_len)

    gmix1 = norm_mix_g[1].reshape(1, d)
    gffn1 = norm_ffn_g[1].reshape(1, d)

    q, kd, vd, w2_1 = _qkv(x_all, n_lat, mod[1], gmix1, w_qkv, _rope_table(seq), tm, seq, lat_row(tm),
                           cast_jobs=[(ffn_w_out, 1, 0)])
    kx, vx = _ctx_kv(x_all, n_lat, batch * ctx_len, mod[1], gmix1, w_qkv)
    o, w1_1, w_out1 = _attention(at_sink[0], q, kd, vd, kx, vx, batch, seq, ctx_len, tq,
                                 cast_jobs=[(ffn_w_in, 1, 0), (at_w_out, 0, 0)])

    out = _attn_out(x_all, n_lat, o, w_out1, mod[1], gffn1, w1_1, w2_1,
                    norm_final_g.reshape(1, d), tm_ffn, lat_row(tm_ffn))
    return out.reshape(batch, seq, d)
```

```python
import functools

import jax
import jax.numpy as jnp
from jax import lax
from jax.experimental import pallas as pl
from jax.experimental.pallas import tpu as pltpu

D_MODEL = 1024
N_MOD = 6
EPS = 1e-6
NEG = -1e30
D_LRU = 512
D_SC = 512
LRU_HEADS = 8
LRU_HEAD_DIM = 64
LRU_C = 8.0
RC_IN_WIDTH = 2 * D_LRU + 3 * D_SC
HEAD_DIM = 64
N_Q_HEADS = 16
N_KV_HEADS = 4
GQA_GROUP = 4
WINDOW = 128
GRID_W = 64
ROPE_BASE = 10000.0
ROPE_FREQS = 16
D_FF = 2816
LOG2E = 1.4426950408889634

V7X_LANES = 128
V7X_SUBLANES = 8
V7X_MXU_DIM = 256
V7X_VMEM_LIMIT = 60 * 1024 * 1024

HALO = V7X_SUBLANES
HALO_BF16 = 2 * V7X_SUBLANES
N_SLAB = D_LRU // V7X_LANES
SCAN_SHIFTS = (1, 2, 4)
FFN_CHUNKS = ((0, 2816),)
FFN_SUB_ROWS = 256
PROJ_SUB_ROWS = 512

_BF16 = jnp.bfloat16
_F32 = jnp.float32


def _dot(a, b):
    return jnp.dot(a, b, preferred_element_type=_F32)


def _dot_t(a, b):
    return lax.dot_general(a, b, (((1,), (1,)), ((), ())), preferred_element_type=_F32)


def _sigmoid(x):
    return 0.5 * jnp.tanh(0.5 * x) + 0.5


def _gelu_tanh(x):
    return 0.5 * x * (1.0 + jnp.tanh(0.7978845608028654 * (x + 0.044715 * (x * x * x))))


def _rms_mod(x, gain, shift, scale):
    ms = jnp.mean(x * x, axis=-1, keepdims=True)
    return (x * lax.rsqrt(ms + EPS) * gain) * (1.0 + scale) + shift


def _const_spec(shape, index=None):
    idx = (0,) * len(shape) if index is None else index
    return pl.BlockSpec(shape, lambda *_: idx, pipeline_mode=pl.Buffered(1))


def _params(sem):
    return pltpu.CompilerParams(dimension_semantics=sem, vmem_limit_bytes=V7X_VMEM_LIMIT)


def _cast_specs(jobs, nsteps, step_of=lambda i: i):
    ins, outs, shapes = [], [], []
    for w, layer, _ in jobs:
        _, r, c = w.shape
        assert r % (nsteps * HALO_BF16) == 0, (w.shape, nsteps)
        chunk = r // nsteps
        ins.append(pl.BlockSpec((1, chunk, c), lambda *idx, layer=layer: (layer, step_of(*idx), 0)))
        outs.append(pl.BlockSpec((chunk, c), lambda *idx: (step_of(*idx), 0)))
        shapes.append(jax.ShapeDtypeStruct((r, c), _BF16))
    return ins, outs, shapes


def _cast_chunks(rope_cols_per_job, cast_in, cast_out):
    for rope_cols, src, dst in zip(rope_cols_per_job, cast_in, cast_out):
        moves = _rope_lane_moves((src.shape[1], V7X_LANES)) if rope_cols else None
        for c0 in range(0, rope_cols, V7X_LANES):
            dst[:, c0:c0 + V7X_LANES] = _to_rope_lanes(src[0, :, c0:c0 + V7X_LANES], moves).astype(_BF16)
        dst[:, rope_cols:] = src[0, :, rope_cols:].astype(_BF16)


def _mod_kernel(c_ref, w_ref, b_ref, o_ref):
    c = c_ref[...]
    s = (c * _sigmoid(c)).astype(_BF16)
    o_ref[0] = _dot(s, w_ref[0].astype(_BF16)) + b_ref[0]


def _modulation(cond, ada_w, ada_b):
    depth, d, n = ada_w.shape
    bn = 1536
    return pl.pallas_call(
        _mod_kernel,
        grid=(depth, n // bn),
        in_specs=[
            pl.BlockSpec((8, d), lambda l, j: (0, 0)),
            pl.BlockSpec((1, d, bn), lambda l, j: (l, 0, j)),
            pl.BlockSpec((1, 1, bn), lambda l, j: (l, 0, j)),
        ],
        out_specs=pl.BlockSpec((1, 8, bn), lambda l, j: (l, 0, j)),
        out_shape=jax.ShapeDtypeStruct((depth, 8, n), _F32),
        compiler_params=_params(("arbitrary", "arbitrary")),
        name="adaln_mod",
    )(cond, ada_w, ada_b.reshape(depth, 1, n))


def _proj_kernel(*refs, cast_rope, own_weight):
    ncast = len(cast_rope)
    nin = 4 if own_weight else 3
    x_ref, mod_ref, g_ref = refs[:3]
    xa_ref, rest_ref = refs[nin + ncast:nin + 2 + ncast]
    cast_out = refs[nin + 2 + ncast:]
    _cast_chunks(cast_rope, refs[nin:nin + ncast], cast_out)
    w_ref = refs[3] if own_weight else cast_out[0]
    h_all = _rms_mod(x_ref[...], g_ref[...], mod_ref[0, 0:1, :], mod_ref[0, 1:2, :]).astype(_BF16)
    tm = h_all.shape[0]
    sub = min(PROJ_SUB_ROWS, tm)
    for r0 in range(0, tm, sub):
        rows = slice(r0, r0 + sub)
        h = h_all[rows, :]
        xa = _dot(h, w_ref[:, 0:D_LRU])
        for c in range(N_SLAB):
            xa_ref[c, rows, :] = xa[:, c * V7X_LANES:(c + 1) * V7X_LANES]
        ga = _dot(h, w_ref[:, D_LRU:2 * D_LRU])
        rest_ref[rows, 0:D_LRU] = _gelu_tanh(ga).astype(_BF16)
        rest_ref[rows, D_LRU:D_LRU + D_SC] = _dot(h, w_ref[:, 2 * D_LRU:2 * D_LRU + D_SC]).astype(_BF16)
        cv = _dot(h, w_ref[:, 2 * D_LRU + D_SC:2 * D_LRU + 3 * D_SC])
        rest_ref[rows, D_LRU + D_SC:D_LRU + 2 * D_SC] = (cv[:, 0:D_SC] * cv[:, D_SC:2 * D_SC]).astype(_BF16)


def _project(x2d, mod, gain, w, tm, row_of_tile, cast_jobs=()):
    n, d = x2d.shape
    nrest = D_LRU + 2 * D_SC
    assert w is not None or n == tm
    cast_in, cast_out, cast_shapes = _cast_specs(cast_jobs, n // tm)
    weight = [] if w is None else [w]
    return pl.pallas_call(
        functools.partial(_proj_kernel, cast_rope=tuple(job[2] for job in cast_jobs), own_weight=w is not None),
        grid=(n // tm,),
        in_specs=[
            pl.BlockSpec((tm, d), lambda i: (i, 0)),
            pl.BlockSpec((1, N_MOD, d), lambda i: (row_of_tile(i), 0, 0)),
            _const_spec((1, d)),
        ] + [_const_spec(a.shape) for a in weight] + cast_in,
        out_specs=[
            pl.BlockSpec((N_SLAB, tm, V7X_LANES), lambda i: (0, i, 0)),
            pl.BlockSpec((tm, nrest), lambda i: (i, 0)),
        ] + cast_out,
        out_shape=[
            jax.ShapeDtypeStruct((N_SLAB, n, V7X_LANES), _F32),
            jax.ShapeDtypeStruct((n, nrest), _BF16),
        ] + cast_shapes,
        compiler_params=_params(("arbitrary",)),
        name="rc_in_proj",
    )(x2d, mod, gain, *weight, *[job[0] for job in cast_jobs])


def _scan_kernel(xm_f, xp_f, xn_f, xm_b, xp_b, xn_b, cw_ref, cb_ref, wg_ref, bg_ref, lam_ref, h0_ref,
                 hf_ref, hb_ref, hl_ref, xs_scr, carry_scr, *level_scr, nc, tc):
    j = pl.program_id(1)
    lo = HALO
    ngroups = tc // V7X_SUBLANES
    nlev = len(SCAN_SHIFTS)
    a_lv = list(level_scr[:nlev])
    b_lv = list(level_scr[nlev:])
    pad_rows = (slice(0, HALO), slice(lo + tc, lo + tc + HALO))

    @pl.when(j == 0)
    def _():
        carry_scr[...] = h0_ref[0]
        for d in range(2):
            for ref in a_lv:
                ref[d, :, pad_rows[d], :] = jnp.ones((N_SLAB, HALO, V7X_LANES), _F32)
            for ref in b_lv:
                ref[d, :, pad_rows[d], :] = jnp.zeros((N_SLAB, HALO, V7X_LANES), _F32)

    cw = cw_ref[...]
    cb = cb_ref[...]

    def coeffs(d, xm, xp, xn, has_prev, has_next):
        xs_scr[d, :, 0:HALO, :] = jnp.where(has_prev, xp[...], 0.0)
        xs_scr[d, :, lo:lo + tc, :] = xm[...]
        xs_scr[d, :, lo + tc:lo + tc + HALO, :] = jnp.where(has_next, xn[...], 0.0)
        xcs = []
        for c in range(N_SLAB):
            lanes = slice(c * V7X_LANES, (c + 1) * V7X_LANES)
            acc = cb[:, lanes]
            for k in range(4):
                acc = acc + cw[k:k + 1, lanes] * xs_scr[d, c, lo - 2 + k:lo - 2 + k + tc, :]
            xcs.append(acc)
        lam = lam_ref[d]
        nlam = -lam
        softplus = jnp.maximum(nlam, 0.0) + jnp.log(1.0 + jnp.exp(-jnp.abs(nlam)))
        hrate = (-0.5 * LRU_C * LOG2E) * softplus
        half = V7X_MXU_DIM
        per_half = half // V7X_LANES
        for hh in range(D_LRU // half):
            xch = jnp.concatenate(xcs[hh * per_half:(hh + 1) * per_half], axis=1)
            z = _dot(xch.astype(_BF16), wg_ref[d, hh]) + bg_ref[d, hh]
            for cc in range(per_half):
                c = hh * per_half + cc
                lanes = slice(c * V7X_LANES, (c + 1) * V7X_LANES)
                tr = jnp.tanh(z[:, cc * V7X_LANES:(cc + 1) * V7X_LANES])
                ig = 0.5 * jnp.tanh(z[:, half + cc * V7X_LANES:half + (cc + 1) * V7X_LANES]) + 0.5
                a = jnp.exp2(tr * hrate[:, lanes] + hrate[:, lanes])
                one_m_a2 = jnp.maximum(1.0 - a * a, 1e-12)
                mult = one_m_a2 * lax.rsqrt(one_m_a2)
                a_lv[0][d, c, lo:lo + tc, :] = a
                b_lv[0][d, c, lo:lo + tc, :] = mult * ig * xcs[c]

    coeffs(0, xm_f, xp_f, xn_f, j > 0, j < nc - 1)
    coeffs(1, xm_b, xp_b, xn_b, j < nc - 1, j > 0)

    def step(g, carry):
        out = []
        for d, h_ref in ((0, hf_ref), (1, hb_ref)):
            grp = g if d == 0 else ngroups - 1 - g
            row0 = pl.multiple_of(lo + grp * V7X_SUBLANES, V7X_SUBLANES)
            cur = pl.ds(row0, V7X_SUBLANES)
            for c in range(N_SLAB):
                a = a_lv[0][d, c, cur, :]
                b = b_lv[0][d, c, cur, :]
                for lvl, shift in enumerate(SCAN_SHIFTS):
                    sh = pl.ds(row0 + (shift if d else -shift), V7X_SUBLANES)
                    b = a * b_lv[lvl][d, c, sh, :] + b
                    a = a * a_lv[lvl][d, c, sh, :]
                    if lvl + 1 < nlev:
                        a_lv[lvl + 1][d, c, cur, :] = a
                        b_lv[lvl + 1][d, c, cur, :] = b
                h = a * carry[d * N_SLAB + c] + b
                out_rows = pl.ds(pl.multiple_of(grp * V7X_SUBLANES, V7X_SUBLANES), V7X_SUBLANES)
                h_ref[out_rows, c * V7X_LANES:(c + 1) * V7X_LANES] = h
                out.append(h)
        return tuple(out)

    init = tuple(carry_scr[d, :, c * V7X_LANES:(c + 1) * V7X_LANES]
                 for d in range(2) for c in range(N_SLAB))
    carry = lax.fori_loop(0, ngroups, step, init, unroll=4)

    edge_row = (V7X_SUBLANES - 1, 0)
    data_edge = (slice(tc, tc + HALO), slice(lo, lo + HALO))
    for d in range(2):
        for c in range(N_SLAB):
            lanes = slice(c * V7X_LANES, (c + 1) * V7X_LANES)
            h = carry[d * N_SLAB + c]
            carry_scr[d, :, lanes] = h
            hl_ref[0, d, :, lanes] = jnp.broadcast_to(h[edge_row[d]:edge_row[d] + 1, :], h.shape)
        for ref in a_lv + b_lv:
            ref[d, :, pad_rows[d], :] = ref[d, :, data_edge[d], :]


def _scan(xa, h0, conv_w, conv_b, w_gate, b_gate, lam, batch, seq, tc):
    n = xa.shape[1]
    nc = seq // tc
    tb = tc // HALO
    sb = seq // HALO
    last = n // HALO - 1

    def fwd(b, j):
        return j

    def bwd(b, j):
        return nc - 1 - j

    def main(cf):
        return pl.BlockSpec((N_SLAB, tc, V7X_LANES), lambda b, j: (0, b * nc + cf(b, j), 0))

    def prev(cf):
        return pl.BlockSpec((N_SLAB, HALO, V7X_LANES),
                            lambda b, j: (0, jnp.maximum(b * sb + cf(b, j) * tb - 1, 0), 0))

    def nxt(cf):
        return pl.BlockSpec((N_SLAB, HALO, V7X_LANES),
                            lambda b, j: (0, jnp.minimum(b * sb + (cf(b, j) + 1) * tb, last), 0))

    kernel = functools.partial(_scan_kernel, nc=nc, tc=tc)
    nlev = len(SCAN_SHIFTS)
    return pl.pallas_call(
        kernel,
        grid=(batch, nc),
        in_specs=[
            main(fwd), prev(fwd), nxt(fwd), main(bwd), prev(bwd), nxt(bwd),
            _const_spec((4, D_LRU)),
            _const_spec((1, D_LRU)),
            _const_spec(w_gate.shape),
            _const_spec(b_gate.shape),
            _const_spec((2, 1, D_LRU)),
            pl.BlockSpec((1, 2, V7X_SUBLANES, D_LRU), lambda b, j: (b, 0, 0, 0)),
        ],
        out_specs=[
            pl.BlockSpec((tc, D_LRU), lambda b, j: (b * nc + j, 0)),
            pl.BlockSpec((tc, D_LRU), lambda b, j: (b * nc + nc - 1 - j, 0)),
            pl.BlockSpec((1, 2, V7X_SUBLANES, D_LRU), lambda b, j: (b, 0, 0, 0)),
        ],
        out_shape=[
            jax.ShapeDtypeStruct((n, D_LRU), _F32),
            jax.ShapeDtypeStruct((n, D_LRU), _F32),
            jax.ShapeDtypeStruct((batch, 2, V7X_SUBLANES, D_LRU), _F32),
        ],
        scratch_shapes=[
            pltpu.VMEM((2, N_SLAB, tc + 2 * HALO, V7X_LANES), _F32),
            pltpu.VMEM((2, V7X_SUBLANES, D_LRU), _F32),
        ] + [pltpu.VMEM((2, N_SLAB, tc + 2 * HALO, V7X_LANES), _F32) for _ in range(2 * nlev)],
        compiler_params=_params(("arbitrary", "arbitrary")),
        name="rglru_scan",
    )(xa, xa, xa, xa, xa, xa, conv_w, conv_b, w_gate, b_gate, lam, h0)


def _ffn_norm(x, y, mod_ref, gffn_ref):
    x1 = x + mod_ref[0, 2:3, :] * y
    h = _rms_mod(x1, gffn_ref[...], mod_ref[0, 3:4, :], mod_ref[0, 4:5, :]).astype(_BF16)
    return x1, h


def _ffn_apply(x1, h, mod_ref, w1_ref, w2_ref, gfin_ref):
    acc = None
    for c0, cw in FFN_CHUNKS:
        g = _dot(h, w1_ref[:, c0:c0 + cw])
        u = _dot(h, w1_ref[:, D_FF + c0:D_FF + c0 + cw])
        act = (g * _sigmoid(g) * u).astype(_BF16)
        part = _dot(act, w2_ref[c0:c0 + cw, :])
        acc = part if acc is None else acc + part
    x2 = x1 + mod_ref[0, 5:6, :] * acc
    if gfin_ref is not None:
        ms = jnp.mean(x2 * x2, axis=-1, keepdims=True)
        x2 = x2 * lax.rsqrt(ms + EPS) * gfin_ref[...]
    return x2


def _ffn_weight_specs(w1, w2):
    return [_const_spec(w1.shape), _const_spec(w2.shape)]


def _rc_out_kernel(x_ref, hf_ref, hb_ref, gg_ref, bg_ref, um_ref, up_ref, un_ref,
                   xc_ref, hfc_ref, hbc_ref, pc_ref,
                   cbw_ref, wo_ref, mod_ref, gffn_ref, w1_ref, w2_ref, o_ref, *, nt, tm, n_lat, ctx_len):
    i = pl.program_id(0)
    is_ctx = i == n_lat
    pick = lambda lat, ctx: jnp.where(is_ctx, ctx, lat)
    has_prev = jnp.logical_and((i % nt) > 0, jnp.logical_not(is_ctx))
    has_next = jnp.logical_and((i % nt) < nt - 1, jnp.logical_not(is_ctx))
    prev = jnp.where(has_prev, up_ref[...].astype(_F32), 0.0)
    nxt = jnp.where(has_next, un_ref[...].astype(_F32), 0.0)
    u_mid = pick(um_ref[...], pc_ref[:, 2 * D_SC:3 * D_SC]).astype(_F32)
    uext = jnp.concatenate([prev, u_mid, nxt], axis=0)
    cbw = cbw_ref[...]
    sub = min(FFN_SUB_ROWS, tm // 2)
    normed = []
    for r0 in range(0, tm, sub):
        rows = slice(r0, r0 + sub)
        hsum = pick(hf_ref[rows, :], hfc_ref[rows, :]) + pick(hb_ref[rows, :], hbc_ref[rows, :])
        ya = (hsum * pick(gg_ref[rows, :], pc_ref[rows, 0:D_LRU]).astype(_F32)).astype(_BF16)
        seq_row = (lax.broadcasted_iota(jnp.int32, (sub, 1), 0) + r0) % ctx_len
        tap_ok = (jnp.logical_or(jnp.logical_not(is_ctx), seq_row > 0), None,
                  jnp.logical_or(jnp.logical_not(is_ctx), seq_row < ctx_len - 1))
        conv = None
        for k in range(3):
            first = HALO_BF16 - 1 + k + r0
            term = cbw[k:k + 1, :] * uext[first:first + sub, :]
            if tap_ok[k] is not None:
                term = jnp.where(tap_ok[k], term, 0.0)
            conv = term if conv is None else conv + term
        yb = (pick(bg_ref[rows, :], pc_ref[rows, D_LRU:D_LRU + D_SC]).astype(_F32) * conv).astype(_BF16)
        y = _dot(ya, wo_ref[0:D_LRU, :]) + _dot(yb, wo_ref[D_LRU:D_LRU + D_SC, :])
        normed.append(_ffn_norm(pick(x_ref[rows, :], xc_ref[rows, :]), y, mod_ref, gffn_ref))
    for k, r0 in enumerate(range(0, tm, sub)):
        x1, h = normed[k]
        o_ref[r0:r0 + sub, :] = _ffn_apply(x1, h, mod_ref, w1_ref, w2_ref, None)


def _rc_out(x2d, hf, hb, p, xc2d, hf_c, hb_c, p_c, conv_b_w, w_out, mod, gffn, w1, w2, tm, seq, ctx_len):
    n, d = x2d.shape
    assert xc2d.shape[0] == tm and tm % ctx_len == 0
    nt = seq // tm
    n_lat = n // tm
    tb = tm // HALO_BF16
    last = n // HALO_BF16 - 1
    lat = lambda i: jnp.minimum(i, n_lat - 1)

    def col(c):
        return pl.BlockSpec((tm, D_SC), lambda i: (lat(i), c))

    kernel = functools.partial(_rc_out_kernel, nt=nt, tm=tm, n_lat=n_lat, ctx_len=ctx_len)
    return pl.pallas_call(
        kernel,
        grid=(n_lat + 1,),
        in_specs=[
            pl.BlockSpec((tm, d), lambda i: (lat(i), 0)),
            pl.BlockSpec((tm, D_LRU), lambda i: (lat(i), 0)),
            pl.BlockSpec((tm, D_LRU), lambda i: (lat(i), 0)),
            col(0), col(1), col(2),
            pl.BlockSpec((HALO_BF16, D_SC), lambda i: (jnp.maximum(lat(i) * tb - 1, 0), 2)),
            pl.BlockSpec((HALO_BF16, D_SC), lambda i: (jnp.minimum((lat(i) + 1) * tb, last), 2)),
            _const_spec(xc2d.shape), _const_spec(hf_c.shape), _const_spec(hb_c.shape), _const_spec(p_c.shape),
            _const_spec((3, D_SC)),
            _const_spec(w_out.shape),
            pl.BlockSpec((1, N_MOD, d), lambda i: (jnp.where(i == n_lat, 0, 1 + i // nt), 0, 0)),
            _const_spec((1, d)),
        ] + _ffn_weight_specs(w1, w2),
        out_specs=pl.BlockSpec((tm, d), lambda i: (i, 0)),
        out_shape=jax.ShapeDtypeStruct((n + tm, d), _F32),
        compiler_params=_params(("arbitrary",)),
        name="rc_out_ffn",
    )(x2d, hf, hb, p, p, p, p, p, xc2d, hf_c, hb_c, p_c, conv_b_w, w_out, mod, gffn, w1, w2)


def _attn_out_kernel(x_ref, o_in_ref, wo_ref, mod_ref, gffn_ref, gfin_ref, w1_ref, w2_ref, o_ref):
    tm = x_ref.shape[0]
    sub = min(FFN_SUB_ROWS, tm)
    normed = []
    for r0 in range(0, tm, sub):
        rows = slice(r0, r0 + sub)
        o_in = jnp.concatenate([o_in_ref[c, rows, :] for c in range(o_in_ref.shape[0])], axis=1)
        normed.append(_ffn_norm(x_ref[rows, :], _dot(o_in, wo_ref[...]), mod_ref, gffn_ref))
    for k, r0 in enumerate(range(0, tm, sub)):
        x1, h = normed[k]
        o_ref[r0:r0 + sub, :] = _ffn_apply(x1, h, mod_ref, w1_ref, w2_ref, gfin_ref)


def _attn_out(x2d, n, o, w_out, mod, gffn, w1, w2, gfin, tm, row_of_tile):
    d = x2d.shape[1]
    return pl.pallas_call(
        _attn_out_kernel,
        grid=(n // tm,),
        in_specs=[
            pl.BlockSpec((tm, d), lambda i: (i, 0)),
            pl.BlockSpec((o.shape[0], tm, V7X_LANES), lambda i: (0, i, 0)),
            _const_spec(w_out.shape),
            pl.BlockSpec((1, N_MOD, d), lambda i: (row_of_tile(i), 0, 0)),
            _const_spec((1, d)),
            _const_spec((1, d)),
        ] + _ffn_weight_specs(w1, w2),
        out_specs=pl.BlockSpec((tm, d), lambda i: (i, 0)),
        out_shape=jax.ShapeDtypeStruct((n, d), _F32),
        compiler_params=_params(("parallel",)),
        name="attn_out_ffn",
    )(x2d, o, w_out, mod, gffn, gfin, w1, w2)


ROPE_HALF = V7X_LANES // 2
ROPE_HEAD_SPAN = ROPE_HALF // 2


def _rope_lane_moves(shape):
    lane = lax.broadcasted_iota(jnp.int32, shape, 1)
    dst = lane // ROPE_FREQS
    src = ((dst % 4) // 2) * 4 + (dst % 2) * 2 + dst // 4
    return {delta: (src - dst) == delta for delta in range(-3, 4) if delta != 0}


def _to_rope_lanes(x, moves):
    out = x
    for delta, mask in moves.items():
        shifted = pltpu.roll(x, (-delta * ROPE_FREQS) % V7X_LANES, 1)
        out = jnp.where(mask, shifted, out)
    return out


def _rope(x, cos, sin_signed):
    return x * cos + pltpu.roll(x, ROPE_HALF, 1) * sin_signed


def _rope_lane_tables(cs):
    lane = lax.broadcasted_iota(jnp.int32, cs.shape, 1)
    span = ROPE_HEAD_SPAN
    cos = jnp.where(lane < span, cs, pltpu.roll(cs, span, 1))
    sin = jnp.where(lane < span, pltpu.roll(cs, V7X_LANES - span, 1), cs)
    first = lane < ROPE_HALF
    cos = jnp.where(first, cos, pltpu.roll(cos, ROPE_HALF, 1))
    sin = jnp.where(first, sin, pltpu.roll(sin, ROPE_HALF, 1))
    return cos, jnp.where(first, -sin, sin)


def _key_heads_to_slabs(x, ref, c, rows):
    lane = lax.broadcasted_iota(jnp.int32, x.shape, 1)
    is_first = (lane % ROPE_HALF) < ROPE_HEAD_SPAN
    ref[2 * c, rows, :] = jnp.where(is_first, x, pltpu.roll(x, ROPE_HEAD_SPAN, 1)).astype(_BF16)
    ref[2 * c + 1, rows, :] = jnp.where(is_first, pltpu.roll(x, V7X_LANES - ROPE_HEAD_SPAN, 1), x).astype(_BF16)


def _heads_to_slabs(x, ref, c, rows):
    first = lax.broadcasted_iota(jnp.int32, x.shape, 1) < HEAD_DIM
    swapped = pltpu.roll(x, HEAD_DIM, 1)
    ref[2 * c, rows, :] = jnp.where(first, x, swapped).astype(_BF16)
    ref[2 * c + 1, rows, :] = jnp.where(first, swapped, x).astype(_BF16)


def _qkv_kernel(*refs, nt, tm, cast_rope):
    ncast = len(cast_rope)
    x_ref, mod_ref, g_ref, w_ref, tab_ref = refs[:5]
    q_ref, k_ref, v_ref = refs[5 + ncast:8 + ncast]
    _cast_chunks(cast_rope, refs[5:5 + ncast], refs[8 + ncast:])
    h = _rms_mod(x_ref[...], g_ref[...], mod_ref[0, 0:1, :], mod_ref[0, 1:2, :]).astype(_BF16)
    col_terms = tab_ref[0:GRID_W, :]
    first_grid_row = GRID_W + (pl.program_id(0) % nt) * (tm // GRID_W)
    cs = jnp.concatenate([col_terms + tab_ref[pl.ds(first_grid_row + g, 1), :] for g in range(tm // GRID_W)],
                         axis=0)
    cos_t, sin_t = _rope_lane_tables(cs)
    dq = N_Q_HEADS * HEAD_DIM
    dkv = N_KV_HEADS * HEAD_DIM
    scale = HEAD_DIM ** -0.5 * LOG2E
    sub = min(PROJ_SUB_ROWS, tm)
    for r0 in range(0, tm, sub):
        rows = slice(r0, r0 + sub)
        hs, cos, sin = h[rows, :], cos_t[rows, :], sin_t[rows, :]
        q = _dot(hs, w_ref[:, 0:dq])
        for c in range(dq // V7X_LANES):
            cols = slice(c * V7X_LANES, (c + 1) * V7X_LANES)
            q_ref[c, rows, :] = (_rope(q[:, cols], cos, sin) * scale).astype(_BF16)
        k = _dot(hs, w_ref[:, dq:dq + dkv])
        v = _dot(hs, w_ref[:, dq + dkv:dq + 2 * dkv])
        for c in range(dkv // V7X_LANES):
            cols = slice(c * V7X_LANES, (c + 1) * V7X_LANES)
            _key_heads_to_slabs(_rope(k[:, cols], cos, sin), k_ref, c, rows)
            _heads_to_slabs(v[:, cols], v_ref, c, rows)


def _qkv(x2d, n, mod, gain, w, cs_t, tm, seq, row_of_tile, cast_jobs=()):
    d = x2d.shape[1]
    nt = seq // tm
    nq_slab = N_Q_HEADS * HEAD_DIM // V7X_LANES
    assert tm % GRID_W == 0
    cast_in, cast_out, cast_shapes = _cast_specs(cast_jobs, n // tm)
    return pl.pallas_call(
        functools.partial(_qkv_kernel, nt=nt, tm=tm, cast_rope=tuple(job[2] for job in cast_jobs)),
        grid=(n // tm,),
        in_specs=[
            pl.BlockSpec((tm, d), lambda i: (i, 0)),
            pl.BlockSpec((1, N_MOD, d), lambda i: (row_of_tile(i), 0, 0)),
            _const_spec((1, d)),
            _const_spec(w.shape),
            _const_spec(cs_t.shape),
        ] + cast_in,
        out_specs=[
            pl.BlockSpec((nq_slab, tm, V7X_LANES), lambda i: (0, i, 0)),
            pl.BlockSpec((N_KV_HEADS, tm, V7X_LANES), lambda i: (0, i, 0)),
            pl.BlockSpec((N_KV_HEADS, tm, V7X_LANES), lambda i: (0, i, 0)),
        ] + cast_out,
        out_shape=[
            jax.ShapeDtypeStruct((nq_slab, n, V7X_LANES), _BF16),
            jax.ShapeDtypeStruct((N_KV_HEADS, n, V7X_LANES), _BF16),
            jax.ShapeDtypeStruct((N_KV_HEADS, n, V7X_LANES), _BF16),
        ] + cast_shapes,
        compiler_params=_params(("parallel",)),
        name="qkv_rope",
    )(x2d, mod, gain, w, cs_t, *[job[0] for job in cast_jobs])


def _kv_kernel(x_ref, mod_ref, g_ref, w_ref, k_ref, v_ref):
    h = _rms_mod(x_ref[...], g_ref[...], mod_ref[0, 0:1, :], mod_ref[0, 1:2, :]).astype(_BF16)
    dq = N_Q_HEADS * HEAD_DIM
    dkv = N_KV_HEADS * HEAD_DIM
    k = _dot(h, w_ref[:, dq:dq + dkv])
    v = _dot(h, w_ref[:, dq + dkv:dq + 2 * dkv])
    for c in range(dkv // V7X_LANES):
        cols = slice(c * V7X_LANES, (c + 1) * V7X_LANES)
        _key_heads_to_slabs(k[:, cols], k_ref, c, slice(None))
        _heads_to_slabs(v[:, cols], v_ref, c, slice(None))


def _ctx_kv(x2d, first_row, n, mod, gain, w):
    d = x2d.shape[1]
    assert first_row % n == 0
    slab = (N_KV_HEADS, n, V7X_LANES)
    return pl.pallas_call(
        _kv_kernel,
        grid=(1,),
        in_specs=[
            pl.BlockSpec((n, d), lambda i: (first_row // n, 0)),
            pl.BlockSpec((1, N_MOD, d), lambda i: (0, 0, 0)),
            _const_spec((1, d)),
            _const_spec(w.shape),
        ],
        out_specs=[pl.BlockSpec(slab, lambda i: (0, 0, 0)), pl.BlockSpec(slab, lambda i: (0, 0, 0))],
        out_shape=[jax.ShapeDtypeStruct(slab, _BF16), jax.ShapeDtypeStruct(slab, _BF16)],
        compiler_params=_params(("arbitrary",)),
        name="ctx_kv",
    )(x2d, mod, gain, w)


def _attn_kernel(*refs, tq, nq, cast_rope):
    ncast = len(cast_rope)
    sink_ref, q_ref, kp_ref, kc_ref, kn_ref, vp_ref, vc_ref, vn_ref, kx_ref, vx_ref = refs[:10]
    o_ref = refs[10 + ncast]
    kext, vext, s_even, s_odd = refs[11 + 2 * ncast:]
    _cast_chunks(cast_rope, refs[10:10 + ncast], refs[11 + ncast:11 + 2 * ncast])
    n = pl.program_id(1)
    blk = WINDOW
    nsub = tq // blk
    nblocks = nsub * N_KV_HEADS
    ctx_len = kx_ref.shape[1]
    kext[:, 0:blk, :] = kp_ref[...]
    kext[:, blk:blk + tq, :] = kc_ref[...]
    kext[:, blk + tq:2 * blk + tq, :] = kn_ref[...]
    vext[:, 0:blk, :] = vp_ref[...]
    vext[:, blk:blk + tq, :] = vc_ref[...]
    vext[:, blk + tq:2 * blk + tq, :] = vn_ref[...]
    qi = lax.broadcasted_iota(jnp.int32, (blk, blk), 0)
    kj = lax.broadcasted_iota(jnp.int32, (blk, blk), 1)
    lane = lax.broadcasted_iota(jnp.int32, (blk, V7X_LANES), 1)
    lo = lane < HEAD_DIM
    q_first = (lane % ROPE_HALF) < ROPE_HEAD_SPAN
    ones_win = jnp.ones((3 * blk, V7X_LANES), _BF16)
    ones_ctx = jnp.ones((ctx_len, V7X_LANES), _BF16)

    def locate(i):
        j = i // N_KV_HEADS
        return j, i % N_KV_HEADS, j * blk

    def scores(i, s_ref):
        _, hkv, row0 = locate(i)
        parts = []
        for c in range(2):
            qc = q_ref[2 * hkv + c, pl.ds(row0, blk), :]
            parts.append(jnp.where(q_first, qc, jnp.zeros_like(qc)))
            parts.append(jnp.where(q_first, jnp.zeros_like(qc), qc))
        lhs = jnp.concatenate(parts, axis=0)
        s_ref[:, 0:3 * blk] = _dot_t(lhs, kext[hkv, pl.ds(row0, 3 * blk), :])
        s_ref[:, 3 * blk:3 * blk + ctx_len] = _dot_t(lhs, kx_ref[hkv])

    def softmax_pv(i, s_ref):
        j, hkv, row0 = locate(i)
        keep_prev = kj >= qi
        keep_next = kj <= qi
        if j == 0:
            keep_prev = keep_prev & (n > 0)
        if j == nsub - 1:
            keep_next = keep_next & (n < nq - 1)
        vwin = jnp.concatenate([vext[hkv, pl.ds(row0, 3 * blk), :], ones_win], axis=1)
        vx = jnp.concatenate([vx_ref[hkv], ones_ctx], axis=1)
        p_win, p_ctx, esink = [], [], []
        for g in range(GQA_GROUP):
            sink = sink_ref[hkv * GQA_GROUP + g] * LOG2E
            grow = slice(g * blk, (g + 1) * blk)
            s0 = jnp.where(keep_prev, s_ref[grow, 0:blk], NEG)
            s1 = s_ref[grow, blk:2 * blk]
            s2 = jnp.where(keep_next, s_ref[grow, 2 * blk:3 * blk], NEG)
            sc = [s_ref[grow, 3 * blk + t * V7X_LANES:3 * blk + (t + 1) * V7X_LANES]
                  for t in range(ctx_len // V7X_LANES)]
            mm = jnp.maximum(jnp.maximum(s0, s1), s2)
            for t in sc:
                mm = jnp.maximum(mm, t)
            m = jnp.maximum(jnp.max(mm, axis=-1, keepdims=True), sink)
            p_win.append(jnp.concatenate([jnp.exp2(t - m).astype(_BF16) for t in (s0, s1, s2)], axis=1))
            p_ctx.append(jnp.concatenate([jnp.exp2(t - m).astype(_BF16) for t in sc], axis=1))
            esink.append(jnp.exp2(sink - m))
        o = _dot(jnp.concatenate(p_win, axis=0), vwin) + _dot(jnp.concatenate(p_ctx, axis=0), vx)
        og = []
        for g in range(GQA_GROUP):
            grow = slice(g * blk, (g + 1) * blk)
            den = o[grow, V7X_LANES:2 * V7X_LANES] + esink[g]
            og.append(o[grow, 0:V7X_LANES] * (1.0 / den))
        for c in range(2):
            ocol = jnp.where(lo, og[2 * c], og[2 * c + 1])
            o_ref[2 * hkv + c, pl.ds(row0, blk), :] = ocol.astype(_BF16)

    s_bufs = (s_even, s_odd)
    scores(0, s_bufs[0])
    for i in range(nblocks):
        if i + 1 < nblocks:
            scores(i + 1, s_bufs[(i + 1) % 2])
        softmax_pv(i, s_bufs[i % 2])


def _attention(sink, q, kd, vd, kx, vx, batch, seq, ctx_len, tq, cast_jobs=()):
    nq_slab, n, _ = q.shape
    nq = seq // tq
    hb = tq // WINDOW
    sb = seq // WINDOW
    last = n // WINDOW - 1
    kv = N_KV_HEADS

    def cur():
        return pl.BlockSpec((kv, tq, V7X_LANES), lambda b, i: (0, b * nq + i, 0))

    def prev():
        return pl.BlockSpec((kv, WINDOW, V7X_LANES), lambda b, i: (0, jnp.maximum(b * sb + i * hb - 1, 0), 0))

    def nxt():
        return pl.BlockSpec((kv, WINDOW, V7X_LANES), lambda b, i: (0, jnp.minimum(b * sb + (i + 1) * hb, last), 0))

    def ctx():
        return pl.BlockSpec((kv, ctx_len, V7X_LANES), lambda b, i: (0, b, 0))

    cast_in, cast_out, cast_shapes = _cast_specs(cast_jobs, batch * nq, lambda b, i: b * nq + i)
    kernel = functools.partial(_attn_kernel, tq=tq, nq=nq, cast_rope=tuple(job[2] for job in cast_jobs))
    return pl.pallas_call(
        kernel,
        grid=(batch, nq),
        in_specs=[
            pl.BlockSpec(memory_space=pltpu.SMEM),
            pl.BlockSpec((nq_slab, tq, V7X_LANES), lambda b, i: (0, b * nq + i, 0)),
            prev(), cur(), nxt(), prev(), cur(), nxt(), ctx(), ctx(),
        ] + cast_in,
        out_specs=[pl.BlockSpec((nq_slab, tq, V7X_LANES), lambda b, i: (0, b * nq + i, 0))] + cast_out,
        out_shape=[jax.ShapeDtypeStruct(q.shape, _BF16)] + cast_shapes,
        scratch_shapes=[
            pltpu.VMEM((kv, tq + 2 * WINDOW, V7X_LANES), _BF16),
            pltpu.VMEM((kv, tq + 2 * WINDOW, V7X_LANES), _BF16),
            pltpu.VMEM((GQA_GROUP * WINDOW, 3 * WINDOW + ctx_len), _F32),
            pltpu.VMEM((GQA_GROUP * WINDOW, 3 * WINDOW + ctx_len), _F32),
        ],
        compiler_params=_params(("parallel", "parallel")),
        name="band_attn",
    )(sink, q, kd, kd, kd, vd, vd, vd, kx, vx, *[job[0] for job in cast_jobs])


def _gate_weights(r_w, r_b, i_w, i_b):
    heads_per_half = V7X_MXU_DIM // LRU_HEAD_DIM
    eye = jnp.eye(heads_per_half, dtype=_F32)

    def halves(w):
        w = w.reshape(2, D_LRU // V7X_MXU_DIM, heads_per_half, LRU_HEAD_DIM, LRU_HEAD_DIM)
        bd = jnp.einsum('dxhij,hk->dxhikj', w, eye)
        return bd.reshape(2, D_LRU // V7X_MXU_DIM, V7X_MXU_DIM, V7X_MXU_DIM)

    w = (0.5 * jnp.concatenate([halves(r_w), halves(i_w)], axis=-1)).astype(_BF16)
    rb = r_b.reshape(2, D_LRU // V7X_MXU_DIM, 1, V7X_MXU_DIM)
    ib = i_b.reshape(2, D_LRU // V7X_MXU_DIM, 1, V7X_MXU_DIM)
    return w, 0.5 * jnp.concatenate([rb, ib], axis=-1)


def _rope_table(seq):
    grid_rows = seq // GRID_W
    inv_freq = ROPE_BASE ** (-jnp.arange(ROPE_FREQS, dtype=_F32) / ROPE_FREQS)
    pos = jnp.concatenate([jnp.arange(GRID_W), jnp.arange(grid_rows)]).astype(_F32)
    ang = pos[:, None] * inv_freq
    cos, sin = jnp.cos(ang), jnp.sin(ang)
    is_col = (jnp.arange(GRID_W + grid_rows) < GRID_W)[:, None]
    zero = jnp.zeros_like(cos)
    table = jnp.concatenate([jnp.where(is_col, zero, cos), jnp.where(is_col, cos, zero),
                             jnp.where(is_col, zero, sin), jnp.where(is_col, sin, zero)], axis=1)
    return jnp.pad(table, ((0, 0), (0, V7X_LANES - 4 * ROPE_FREQS)))


def kernel(x, c, ctx, c_ctx, ada_w, ada_b, norm_mix_g, norm_ffn_g, norm_final_g, ffn_w_in, ffn_w_out,
           rc_w_in, rc_conv_a_w, rc_conv_a_b, rc_gate_r_w, rc_gate_r_b, rc_gate_i_w, rc_gate_i_b,
           rc_lambda, rc_conv_b_w, rc_w_out, at_w_qkv, at_sink, at_w_out):
    batch, seq, d = x.shape
    ctx_len = ctx.shape[1]
    tm = 1024
    tm_rc = 512
    tm_ffn = 1024
    tq = 1024
    tc = 1024
    assert seq % tm_ffn == 0 and seq % tm_rc == 0 and seq % tm == 0 and seq % tq == 0 and seq % tc == 0
    assert ctx_len % V7X_LANES == 0

    xl = x.reshape(batch * seq, d)
    xc = ctx.reshape(batch * ctx_len, d)

    cond = jnp.concatenate([c_ctx[None], c, jnp.zeros((8 - 1 - batch, d), _F32)], axis=0)
    mod = _modulation(cond, ada_w, ada_b).reshape(ada_w.shape[0], 8, N_MOD, d)

    def lat_row(tile):
        return lambda i: 1 + i // (seq // tile)

    ctx_row = lambda i: 0

    w_gate, b_gate = _gate_weights(rc_gate_r_w[0], rc_gate_r_b[0], rc_gate_i_w[0], rc_gate_i_b[0])
    lam = rc_lambda[0].reshape(2, 1, D_LRU)
    conv_a_b = rc_conv_a_b[0].reshape(1, D_LRU)
    gmix0 = norm_mix_g[0].reshape(1, d)
    gffn0 = norm_ffn_g[0].reshape(1, d)

    xa_c, p_c, w_in = _project(xc, mod[0], gmix0, None, batch * ctx_len, ctx_row, cast_jobs=[(rc_w_in, 0, 0)])
    xa_l, p_l, w1_0, w2_0, w_out0, w_qkv = _project(
        xl, mod[0], gmix0, w_in, tm, lat_row(tm),
        cast_jobs=[(ffn_w_in, 0, 0), (ffn_w_out, 0, 0), (rc_w_out, 0, 0),
                   (at_w_qkv, 0, (N_Q_HEADS + N_KV_HEADS) * HEAD_DIM)])

    h0 = jnp.zeros((batch, 2, V7X_SUBLANES, D_LRU), _F32)
    hf_c, hb_c, h_ctx = _scan(xa_c, h0, rc_conv_a_w[0], conv_a_b, w_gate, b_gate, lam, batch, ctx_len, ctx_len)
    hf_l, hb_l, _ = _scan(xa_l, h_ctx, rc_conv_a_w[0], conv_a_b, w_gate, b_gate, lam, batch, seq, tc)

    n_lat = batch * seq
    x_all = _rc_out(xl, hf_l, hb_l, p_l, xc, hf_c, hb_c, p_c, rc_conv_b_w[0], w_out0, mod[0], gffn0,
                    w1_0, w2_0, tm_rc, seq, ctx_len)

    gmix1 = norm_mix_g[1].reshape(1, d)
    gffn1 = norm_ffn_g[1].reshape(1, d)

    q, kd, vd, w2_1 = _qkv(x_all, n_lat, mod[1], gmix1, w_qkv, _rope_table(seq), tm, seq, lat_row(tm),
                           cast_jobs=[(ffn_w_out, 1, 0)])
    kx, vx = _ctx_kv(x_all, n_lat, batch * ctx_len, mod[1], gmix1, w_qkv)
    o, w1_1, w_out1 = _attention(at_sink[0], q, kd, vd, kx, vx, batch, seq, ctx_len, tq,
                                 cast_jobs=[(ffn_w_in, 1, 0), (at_w_out, 0, 0)])

    out = _attn_out(x_all, n_lat, o, w_out1, mod[1], gffn1, w1_1, w2_1,
                    norm_final_g.reshape(1, d), tm_ffn, lat_row(tm_ffn))
    return out.reshape(batch, seq, d)
```

```python
import functools

import jax
import jax.numpy as jnp
from jax import lax
from jax.experimental import pallas as pl
from jax.experimental.pallas import tpu as pltpu

D_MODEL = 1024
N_MOD = 6
EPS = 1e-6
NEG = -1e30
D_LRU = 512
D_SC = 512
LRU_HEADS = 8
LRU_HEAD_DIM = 64
LRU_C = 8.0
RC_IN_WIDTH = 2 * D_LRU + 3 * D_SC
HEAD_DIM = 64
N_Q_HEADS = 16
N_KV_HEADS = 4
GQA_GROUP = 4
WINDOW = 128
GRID_W = 64
ROPE_BASE = 10000.0
ROPE_FREQS = 16
D_FF = 2816
LOG2E = 1.4426950408889634

V7X_LANES = 128
V7X_SUBLANES = 8
V7X_MXU_DIM = 256
V7X_VMEM_LIMIT = 60 * 1024 * 1024

HALO = V7X_SUBLANES
HALO_BF16 = 2 * V7X_SUBLANES
N_SLAB = D_LRU // V7X_LANES
SCAN_SHIFTS = (1, 2, 4)
FFN_CHUNKS = ((0, 2816),)
FFN_SUB_ROWS = 256
PROJ_SUB_ROWS = 256

_BF16 = jnp.bfloat16
_F32 = jnp.float32


def _dot(a, b):
    return jnp.dot(a, b, preferred_element_type=_F32)


def _dot_t(a, b):
    return lax.dot_general(a, b, (((1,), (1,)), ((), ())), preferred_element_type=_F32)


def _sigmoid(x):
    return 0.5 * jnp.tanh(0.5 * x) + 0.5


def _gelu_tanh(x):
    return 0.5 * x * (1.0 + jnp.tanh(0.7978845608028654 * (x + 0.044715 * (x * x * x))))


def _rms_mod(x, gain, shift, scale):
    ms = jnp.mean(x * x, axis=-1, keepdims=True)
    return (x * lax.rsqrt(ms + EPS) * gain) * (1.0 + scale) + shift


def _const_spec(shape, index=None):
    idx = (0,) * len(shape) if index is None else index
    return pl.BlockSpec(shape, lambda *_: idx, pipeline_mode=pl.Buffered(1))


def _params(sem):
    return pltpu.CompilerParams(dimension_semantics=sem, vmem_limit_bytes=V7X_VMEM_LIMIT)


def _cast_specs(jobs, nsteps, step_of=lambda i: i):
    ins, outs, shapes = [], [], []
    for w, layer, _ in jobs:
        _, r, c = w.shape
        assert r % (nsteps * HALO_BF16) == 0, (w.shape, nsteps)
        chunk = r // nsteps
        ins.append(pl.BlockSpec((1, chunk, c), lambda *idx, layer=layer: (layer, step_of(*idx), 0)))
        outs.append(pl.BlockSpec((chunk, c), lambda *idx: (step_of(*idx), 0)))
        shapes.append(jax.ShapeDtypeStruct((r, c), _BF16))
    return ins, outs, shapes


def _cast_chunks(rope_cols_per_job, cast_in, cast_out):
    for rope_cols, src, dst in zip(rope_cols_per_job, cast_in, cast_out):
        moves = _rope_lane_moves((src.shape[1], V7X_LANES)) if rope_cols else None
        for c0 in range(0, rope_cols, V7X_LANES):
            dst[:, c0:c0 + V7X_LANES] = _to_rope_lanes(src[0, :, c0:c0 + V7X_LANES], moves).astype(_BF16)
        dst[:, rope_cols:] = src[0, :, rope_cols:].astype(_BF16)


def _mod_kernel(c_ref, w_ref, b_ref, o_ref):
    c = c_ref[...]
    s = (c * _sigmoid(c)).astype(_BF16)
    o_ref[0] = _dot(s, w_ref[0].astype(_BF16)) + b_ref[0]


def _modulation(cond, ada_w, ada_b):
    depth, d, n = ada_w.shape
    bn = 1536
    return pl.pallas_call(
        _mod_kernel,
        grid=(depth, n // bn),
        in_specs=[
            pl.BlockSpec((8, d), lambda l, j: (0, 0)),
            pl.BlockSpec((1, d, bn), lambda l, j: (l, 0, j)),
            pl.BlockSpec((1, 1, bn), lambda l, j: (l, 0, j)),
        ],
        out_specs=pl.BlockSpec((1, 8, bn), lambda l, j: (l, 0, j)),
        out_shape=jax.ShapeDtypeStruct((depth, 8, n), _F32),
        compiler_params=_params(("arbitrary", "arbitrary")),
        name="adaln_mod",
    )(cond, ada_w, ada_b.reshape(depth, 1, n))


def _proj_kernel(*refs, cast_rope, own_weight):
    ncast = len(cast_rope)
    nin = 4 if own_weight else 3
    x_ref, mod_ref, g_ref = refs[:3]
    xa_ref, rest_ref = refs[nin + ncast:nin + 2 + ncast]
    cast_out = refs[nin + 2 + ncast:]
    _cast_chunks(cast_rope, refs[nin:nin + ncast], cast_out)
    w_ref = refs[3] if own_weight else cast_out[0]
    h_all = _rms_mod(x_ref[...], g_ref[...], mod_ref[0, 0:1, :], mod_ref[0, 1:2, :]).astype(_BF16)
    tm = h_all.shape[0]
    sub = min(PROJ_SUB_ROWS, tm)
    for r0 in range(0, tm, sub):
        rows = slice(r0, r0 + sub)
        h = h_all[rows, :]
        xa = _dot(h, w_ref[:, 0:D_LRU])
        for c in range(N_SLAB):
            xa_ref[c, rows, :] = xa[:, c * V7X_LANES:(c + 1) * V7X_LANES]
        ga = _dot(h, w_ref[:, D_LRU:2 * D_LRU])
        rest_ref[rows, 0:D_LRU] = _gelu_tanh(ga).astype(_BF16)
        rest_ref[rows, D_LRU:D_LRU + D_SC] = _dot(h, w_ref[:, 2 * D_LRU:2 * D_LRU + D_SC]).astype(_BF16)
        cv = _dot(h, w_ref[:, 2 * D_LRU + D_SC:2 * D_LRU + 3 * D_SC])
        rest_ref[rows, D_LRU + D_SC:D_LRU + 2 * D_SC] = (cv[:, 0:D_SC] * cv[:, D_SC:2 * D_SC]).astype(_BF16)


def _project(x2d, mod, gain, w, tm, row_of_tile, cast_jobs=()):
    n, d = x2d.shape
    nrest = D_LRU + 2 * D_SC
    assert w is not None or n == tm
    cast_in, cast_out, cast_shapes = _cast_specs(cast_jobs, n // tm)
    weight = [] if w is None else [w]
    return pl.pallas_call(
        functools.partial(_proj_kernel, cast_rope=tuple(job[2] for job in cast_jobs), own_weight=w is not None),
        grid=(n // tm,),
        in_specs=[
            pl.BlockSpec((tm, d), lambda i: (i, 0)),
            pl.BlockSpec((1, N_MOD, d), lambda i: (row_of_tile(i), 0, 0)),
            _const_spec((1, d)),
        ] + [_const_spec(a.shape) for a in weight] + cast_in,
        out_specs=[
            pl.BlockSpec((N_SLAB, tm, V7X_LANES), lambda i: (0, i, 0)),
            pl.BlockSpec((tm, nrest), lambda i: (i, 0)),
        ] + cast_out,
        out_shape=[
            jax.ShapeDtypeStruct((N_SLAB, n, V7X_LANES), _F32),
            jax.ShapeDtypeStruct((n, nrest), _BF16),
        ] + cast_shapes,
        compiler_params=_params(("arbitrary",)),
        name="rc_in_proj",
    )(x2d, mod, gain, *weight, *[job[0] for job in cast_jobs])


def _scan_kernel(xm_f, xp_f, xn_f, xm_b, xp_b, xn_b, cw_ref, cb_ref, wg_ref, bg_ref, lam_ref, h0_ref,
                 hf_ref, hb_ref, hl_ref, xs_scr, carry_scr, *level_scr, nc, tc):
    j = pl.program_id(1)
    lo = HALO
    ngroups = tc // V7X_SUBLANES
    nlev = len(SCAN_SHIFTS)
    a_lv = list(level_scr[:nlev])
    b_lv = list(level_scr[nlev:])
    pad_rows = (slice(0, HALO), slice(lo + tc, lo + tc + HALO))

    @pl.when(j == 0)
    def _():
        carry_scr[...] = h0_ref[0]
        for d in range(2):
            for ref in a_lv:
                ref[d, :, pad_rows[d], :] = jnp.ones((N_SLAB, HALO, V7X_LANES), _F32)
            for ref in b_lv:
                ref[d, :, pad_rows[d], :] = jnp.zeros((N_SLAB, HALO, V7X_LANES), _F32)

    cw = cw_ref[...]
    cb = cb_ref[...]

    def coeffs(d, xm, xp, xn, has_prev, has_next):
        xs_scr[d, :, 0:HALO, :] = jnp.where(has_prev, xp[...], 0.0)
        xs_scr[d, :, lo:lo + tc, :] = xm[...]
        xs_scr[d, :, lo + tc:lo + tc + HALO, :] = jnp.where(has_next, xn[...], 0.0)
        xcs = []
        for c in range(N_SLAB):
            lanes = slice(c * V7X_LANES, (c + 1) * V7X_LANES)
            acc = cb[:, lanes]
            for k in range(4):
                acc = acc + cw[k:k + 1, lanes] * xs_scr[d, c, lo - 2 + k:lo - 2 + k + tc, :]
            xcs.append(acc)
        lam = lam_ref[d]
        nlam = -lam
        softplus = jnp.maximum(nlam, 0.0) + jnp.log(1.0 + jnp.exp(-jnp.abs(nlam)))
        hrate = (-0.5 * LRU_C * LOG2E) * softplus
        half = V7X_MXU_DIM
        per_half = half // V7X_LANES
        for hh in range(D_LRU // half):
            xch = jnp.concatenate(xcs[hh * per_half:(hh + 1) * per_half], axis=1)
            z = _dot(xch.astype(_BF16), wg_ref[d, hh]) + bg_ref[d, hh]
            for cc in range(per_half):
                c = hh * per_half + cc
                lanes = slice(c * V7X_LANES, (c + 1) * V7X_LANES)
                tr = jnp.tanh(z[:, cc * V7X_LANES:(cc + 1) * V7X_LANES])
                ig = 0.5 * jnp.tanh(z[:, half + cc * V7X_LANES:half + (cc + 1) * V7X_LANES]) + 0.5
                a = jnp.exp2(tr * hrate[:, lanes] + hrate[:, lanes])
                one_m_a2 = jnp.maximum(1.0 - a * a, 1e-12)
                mult = one_m_a2 * lax.rsqrt(one_m_a2)
                a_lv[0][d, c, lo:lo + tc, :] = a
                b_lv[0][d, c, lo:lo + tc, :] = mult * ig * xcs[c]

    coeffs(0, xm_f, xp_f, xn_f, j > 0, j < nc - 1)
    coeffs(1, xm_b, xp_b, xn_b, j < nc - 1, j > 0)

    def step(g, carry):
        out = []
        for d, h_ref in ((0, hf_ref), (1, hb_ref)):
            grp = g if d == 0 else ngroups - 1 - g
            row0 = pl.multiple_of(lo + grp * V7X_SUBLANES, V7X_SUBLANES)
            cur = pl.ds(row0, V7X_SUBLANES)
            for c in range(N_SLAB):
                a = a_lv[0][d, c, cur, :]
                b = b_lv[0][d, c, cur, :]
                for lvl, shift in enumerate(SCAN_SHIFTS):
                    sh = pl.ds(row0 + (shift if d else -shift), V7X_SUBLANES)
                    b = a * b_lv[lvl][d, c, sh, :] + b
                    a = a * a_lv[lvl][d, c, sh, :]
                    if lvl + 1 < nlev:
                        a_lv[lvl + 1][d, c, cur, :] = a
                        b_lv[lvl + 1][d, c, cur, :] = b
                h = a * carry[d * N_SLAB + c] + b
                out_rows = pl.ds(pl.multiple_of(grp * V7X_SUBLANES, V7X_SUBLANES), V7X_SUBLANES)
                h_ref[out_rows, c * V7X_LANES:(c + 1) * V7X_LANES] = h
                out.append(h)
        return tuple(out)

    init = tuple(carry_scr[d, :, c * V7X_LANES:(c + 1) * V7X_LANES]
                 for d in range(2) for c in range(N_SLAB))
    carry = lax.fori_loop(0, ngroups, step, init, unroll=8)

    edge_row = (V7X_SUBLANES - 1, 0)
    data_edge = (slice(tc, tc + HALO), slice(lo, lo + HALO))
    for d in range(2):
        for c in range(N_SLAB):
            lanes = slice(c * V7X_LANES, (c + 1) * V7X_LANES)
            h = carry[d * N_SLAB + c]
            carry_scr[d, :, lanes] = h
            hl_ref[0, d, :, lanes] = jnp.broadcast_to(h[edge_row[d]:edge_row[d] + 1, :], h.shape)
        for ref in a_lv + b_lv:
            ref[d, :, pad_rows[d], :] = ref[d, :, data_edge[d], :]


def _scan(xa, h0, conv_w, conv_b, w_gate, b_gate, lam, batch, seq, tc):
    n = xa.shape[1]
    nc = seq // tc
    tb = tc // HALO
    sb = seq // HALO
    last = n // HALO - 1

    def fwd(b, j):
        return j

    def bwd(b, j):
        return nc - 1 - j

    def main(cf):
        return pl.BlockSpec((N_SLAB, tc, V7X_LANES), lambda b, j: (0, b * nc + cf(b, j), 0))

    def prev(cf):
        return pl.BlockSpec((N_SLAB, HALO, V7X_LANES),
                            lambda b, j: (0, jnp.maximum(b * sb + cf(b, j) * tb - 1, 0), 0))

    def nxt(cf):
        return pl.BlockSpec((N_SLAB, HALO, V7X_LANES),
                            lambda b, j: (0, jnp.minimum(b * sb + (cf(b, j) + 1) * tb, last), 0))

    kernel = functools.partial(_scan_kernel, nc=nc, tc=tc)
    nlev = len(SCAN_SHIFTS)
    return pl.pallas_call(
        kernel,
        grid=(batch, nc),
        in_specs=[
            main(fwd), prev(fwd), nxt(fwd), main(bwd), prev(bwd), nxt(bwd),
            _const_spec((4, D_LRU)),
            _const_spec((1, D_LRU)),
            _const_spec(w_gate.shape),
            _const_spec(b_gate.shape),
            _const_spec((2, 1, D_LRU)),
            pl.BlockSpec((1, 2, V7X_SUBLANES, D_LRU), lambda b, j: (b, 0, 0, 0)),
        ],
        out_specs=[
            pl.BlockSpec((tc, D_LRU), lambda b, j: (b * nc + j, 0)),
            pl.BlockSpec((tc, D_LRU), lambda b, j: (b * nc + nc - 1 - j, 0)),
            pl.BlockSpec((1, 2, V7X_SUBLANES, D_LRU), lambda b, j: (b, 0, 0, 0)),
        ],
        out_shape=[
            jax.ShapeDtypeStruct((n, D_LRU), _F32),
            jax.ShapeDtypeStruct((n, D_LRU), _F32),
            jax.ShapeDtypeStruct((batch, 2, V7X_SUBLANES, D_LRU), _F32),
        ],
        scratch_shapes=[
            pltpu.VMEM((2, N_SLAB, tc + 2 * HALO, V7X_LANES), _F32),
            pltpu.VMEM((2, V7X_SUBLANES, D_LRU), _F32),
        ] + [pltpu.VMEM((2, N_SLAB, tc + 2 * HALO, V7X_LANES), _F32) for _ in range(2 * nlev)],
        compiler_params=_params(("arbitrary", "arbitrary")),
        name="rglru_scan",
    )(xa, xa, xa, xa, xa, xa, conv_w, conv_b, w_gate, b_gate, lam, h0)


def _ffn_norm(x, y, mod_ref, gffn_ref):
    x1 = x + mod_ref[0, 2:3, :] * y
    h = _rms_mod(x1, gffn_ref[...], mod_ref[0, 3:4, :], mod_ref[0, 4:5, :]).astype(_BF16)
    return x1, h


def _ffn_apply(x1, h, mod_ref, w1_ref, w2_ref, gfin_ref):
    acc = None
    for c0, cw in FFN_CHUNKS:
        g = _dot(h, w1_ref[:, c0:c0 + cw])
        u = _dot(h, w1_ref[:, D_FF + c0:D_FF + c0 + cw])
        act = (g * _sigmoid(g) * u).astype(_BF16)
        part = _dot(act, w2_ref[c0:c0 + cw, :])
        acc = part if acc is None else acc + part
    x2 = x1 + mod_ref[0, 5:6, :] * acc
    if gfin_ref is not None:
        ms = jnp.mean(x2 * x2, axis=-1, keepdims=True)
        x2 = x2 * lax.rsqrt(ms + EPS) * gfin_ref[...]
    return x2


def _ffn_weight_specs(w1, w2):
    return [_const_spec(w1.shape), _const_spec(w2.shape)]


def _rc_out_kernel(x_ref, hf_ref, hb_ref, gg_ref, bg_ref, um_ref, up_ref, un_ref,
                   xc_ref, hfc_ref, hbc_ref, pc_ref,
                   cbw_ref, wo_ref, mod_ref, gffn_ref, w1_ref, w2_ref, o_ref, *, nt, tm, n_lat, ctx_len):
    i = pl.program_id(0)
    is_ctx = i == n_lat
    pick = lambda lat, ctx: jnp.where(is_ctx, ctx, lat)
    has_prev = jnp.logical_and((i % nt) > 0, jnp.logical_not(is_ctx))
    has_next = jnp.logical_and((i % nt) < nt - 1, jnp.logical_not(is_ctx))
    prev = jnp.where(has_prev, up_ref[...].astype(_F32), 0.0)
    nxt = jnp.where(has_next, un_ref[...].astype(_F32), 0.0)
    u_mid = pick(um_ref[...], pc_ref[:, 2 * D_SC:3 * D_SC]).astype(_F32)
    uext = jnp.concatenate([prev, u_mid, nxt], axis=0)
    cbw = cbw_ref[...]
    sub = min(FFN_SUB_ROWS, tm // 2)
    normed = []
    for r0 in range(0, tm, sub):
        rows = slice(r0, r0 + sub)
        hsum = pick(hf_ref[rows, :], hfc_ref[rows, :]) + pick(hb_ref[rows, :], hbc_ref[rows, :])
        ya = (hsum * pick(gg_ref[rows, :], pc_ref[rows, 0:D_LRU]).astype(_F32)).astype(_BF16)
        seq_row = (lax.broadcasted_iota(jnp.int32, (sub, 1), 0) + r0) % ctx_len
        tap_ok = (jnp.logical_or(jnp.logical_not(is_ctx), seq_row > 0), None,
                  jnp.logical_or(jnp.logical_not(is_ctx), seq_row < ctx_len - 1))
        conv = None
        for k in range(3):
            first = HALO_BF16 - 1 + k + r0
            term = cbw[k:k + 1, :] * uext[first:first + sub, :]
            if tap_ok[k] is not None:
                term = jnp.where(tap_ok[k], term, 0.0)
            conv = term if conv is None else conv + term
        yb = (pick(bg_ref[rows, :], pc_ref[rows, D_LRU:D_LRU + D_SC]).astype(_F32) * conv).astype(_BF16)
        y = _dot(ya, wo_ref[0:D_LRU, :]) + _dot(yb, wo_ref[D_LRU:D_LRU + D_SC, :])
        normed.append(_ffn_norm(pick(x_ref[rows, :], xc_ref[rows, :]), y, mod_ref, gffn_ref))
    for k, r0 in enumerate(range(0, tm, sub)):
        x1, h = normed[k]
        o_ref[r0:r0 + sub, :] = _ffn_apply(x1, h, mod_ref, w1_ref, w2_ref, None)


def _rc_out(x2d, hf, hb, p, xc2d, hf_c, hb_c, p_c, conv_b_w, w_out, mod, gffn, w1, w2, tm, seq, ctx_len):
    n, d = x2d.shape
    assert xc2d.shape[0] == tm and tm % ctx_len == 0
    nt = seq // tm
    n_lat = n // tm
    tb = tm // HALO_BF16
    last = n // HALO_BF16 - 1
    lat = lambda i: jnp.minimum(i, n_lat - 1)

    def col(c):
        return pl.BlockSpec((tm, D_SC), lambda i: (lat(i), c))

    kernel = functools.partial(_rc_out_kernel, nt=nt, tm=tm, n_lat=n_lat, ctx_len=ctx_len)
    return pl.pallas_call(
        kernel,
        grid=(n_lat + 1,),
        in_specs=[
            pl.BlockSpec((tm, d), lambda i: (lat(i), 0)),
            pl.BlockSpec((tm, D_LRU), lambda i: (lat(i), 0)),
            pl.BlockSpec((tm, D_LRU), lambda i: (lat(i), 0)),
            col(0), col(1), col(2),
            pl.BlockSpec((HALO_BF16, D_SC), lambda i: (jnp.maximum(lat(i) * tb - 1, 0), 2)),
            pl.BlockSpec((HALO_BF16, D_SC), lambda i: (jnp.minimum((lat(i) + 1) * tb, last), 2)),
            _const_spec(xc2d.shape), _const_spec(hf_c.shape), _const_spec(hb_c.shape), _const_spec(p_c.shape),
            _const_spec((3, D_SC)),
            _const_spec(w_out.shape),
            pl.BlockSpec((1, N_MOD, d), lambda i: (jnp.where(i == n_lat, 0, 1 + i // nt), 0, 0)),
            _const_spec((1, d)),
        ] + _ffn_weight_specs(w1, w2),
        out_specs=pl.BlockSpec((tm, d), lambda i: (i, 0)),
        out_shape=jax.ShapeDtypeStruct((n + tm, d), _F32),
        compiler_params=_params(("arbitrary",)),
        name="rc_out_ffn",
    )(x2d, hf, hb, p, p, p, p, p, xc2d, hf_c, hb_c, p_c, conv_b_w, w_out, mod, gffn, w1, w2)


def _attn_out_kernel(x_ref, o_in_ref, wo_ref, mod_ref, gffn_ref, gfin_ref, w1_ref, w2_ref, o_ref):
    tm = x_ref.shape[0]
    sub = min(FFN_SUB_ROWS, tm)
    normed = []
    for r0 in range(0, tm, sub):
        rows = slice(r0, r0 + sub)
        o_in = jnp.concatenate([o_in_ref[c, rows, :] for c in range(o_in_ref.shape[0])], axis=1)
        normed.append(_ffn_norm(x_ref[rows, :], _dot(o_in, wo_ref[...]), mod_ref, gffn_ref))
    for k, r0 in enumerate(range(0, tm, sub)):
        x1, h = normed[k]
        o_ref[r0:r0 + sub, :] = _ffn_apply(x1, h, mod_ref, w1_ref, w2_ref, gfin_ref)


def _attn_out(x2d, n, o, w_out, mod, gffn, w1, w2, gfin, tm, row_of_tile):
    d = x2d.shape[1]
    return pl.pallas_call(
        _attn_out_kernel,
        grid=(n // tm,),
        in_specs=[
            pl.BlockSpec((tm, d), lambda i: (i, 0)),
            pl.BlockSpec((o.shape[0], tm, V7X_LANES), lambda i: (0, i, 0)),
            _const_spec(w_out.shape),
            pl.BlockSpec((1, N_MOD, d), lambda i: (row_of_tile(i), 0, 0)),
            _const_spec((1, d)),
            _const_spec((1, d)),
        ] + _ffn_weight_specs(w1, w2),
        out_specs=pl.BlockSpec((tm, d), lambda i: (i, 0)),
        out_shape=jax.ShapeDtypeStruct((n, d), _F32),
        compiler_params=_params(("parallel",)),
        name="attn_out_ffn",
    )(x2d, o, w_out, mod, gffn, gfin, w1, w2)


ROPE_HALF = V7X_LANES // 2
ROPE_HEAD_SPAN = ROPE_HALF // 2


def _rope_lane_moves(shape):
    lane = lax.broadcasted_iota(jnp.int32, shape, 1)
    dst = lane // ROPE_FREQS
    src = ((dst % 4) // 2) * 4 + (dst % 2) * 2 + dst // 4
    return {delta: (src - dst) == delta for delta in range(-3, 4) if delta != 0}


def _to_rope_lanes(x, moves):
    out = x
    for delta, mask in moves.items():
        shifted = pltpu.roll(x, (-delta * ROPE_FREQS) % V7X_LANES, 1)
        out = jnp.where(mask, shifted, out)
    return out


def _rope(x, cos, sin_signed):
    return x * cos + pltpu.roll(x, ROPE_HALF, 1) * sin_signed


def _rope_lane_tables(cs):
    lane = lax.broadcasted_iota(jnp.int32, cs.shape, 1)
    span = ROPE_HEAD_SPAN
    cos = jnp.where(lane < span, cs, pltpu.roll(cs, span, 1))
    sin = jnp.where(lane < span, pltpu.roll(cs, V7X_LANES - span, 1), cs)
    first = lane < ROPE_HALF
    cos = jnp.where(first, cos, pltpu.roll(cos, ROPE_HALF, 1))
    sin = jnp.where(first, sin, pltpu.roll(sin, ROPE_HALF, 1))
    return cos, jnp.where(first, -sin, sin)


def _key_heads_to_slabs(x, ref, c, rows):
    lane = lax.broadcasted_iota(jnp.int32, x.shape, 1)
    is_first = (lane % ROPE_HALF) < ROPE_HEAD_SPAN
    ref[2 * c, rows, :] = jnp.where(is_first, x, pltpu.roll(x, ROPE_HEAD_SPAN, 1)).astype(_BF16)
    ref[2 * c + 1, rows, :] = jnp.where(is_first, pltpu.roll(x, V7X_LANES - ROPE_HEAD_SPAN, 1), x).astype(_BF16)


def _heads_to_slabs(x, ref, c, rows):
    first = lax.broadcasted_iota(jnp.int32, x.shape, 1) < HEAD_DIM
    swapped = pltpu.roll(x, HEAD_DIM, 1)
    ref[2 * c, rows, :] = jnp.where(first, x, swapped).astype(_BF16)
    ref[2 * c + 1, rows, :] = jnp.where(first, swapped, x).astype(_BF16)


def _qkv_kernel(*refs, nt, tm, cast_rope):
    ncast = len(cast_rope)
    x_ref, mod_ref, g_ref, w_ref, tab_ref = refs[:5]
    q_ref, k_ref, v_ref = refs[5 + ncast:8 + ncast]
    _cast_chunks(cast_rope, refs[5:5 + ncast], refs[8 + ncast:])
    h = _rms_mod(x_ref[...], g_ref[...], mod_ref[0, 0:1, :], mod_ref[0, 1:2, :]).astype(_BF16)
    col_terms = tab_ref[0:GRID_W, :]
    first_grid_row = GRID_W + (pl.program_id(0) % nt) * (tm // GRID_W)
    cs = jnp.concatenate([col_terms + tab_ref[pl.ds(first_grid_row + g, 1), :] for g in range(tm // GRID_W)],
                         axis=0)
    cos_t, sin_t = _rope_lane_tables(cs)
    dq = N_Q_HEADS * HEAD_DIM
    dkv = N_KV_HEADS * HEAD_DIM
    scale = HEAD_DIM ** -0.5 * LOG2E
    sub = min(PROJ_SUB_ROWS, tm)
    for r0 in range(0, tm, sub):
        rows = slice(r0, r0 + sub)
        hs, cos, sin = h[rows, :], cos_t[rows, :], sin_t[rows, :]
        q = _dot(hs, w_ref[:, 0:dq])
        for c in range(dq // V7X_LANES):
            cols = slice(c * V7X_LANES, (c + 1) * V7X_LANES)
            q_ref[c, rows, :] = (_rope(q[:, cols], cos, sin) * scale).astype(_BF16)
        k = _dot(hs, w_ref[:, dq:dq + dkv])
        v = _dot(hs, w_ref[:, dq + dkv:dq + 2 * dkv])
        for c in range(dkv // V7X_LANES):
            cols = slice(c * V7X_LANES, (c + 1) * V7X_LANES)
            _key_heads_to_slabs(_rope(k[:, cols], cos, sin), k_ref, c, rows)
            _heads_to_slabs(v[:, cols], v_ref, c, rows)


def _qkv(x2d, n, mod, gain, w, cs_t, tm, seq, row_of_tile, cast_jobs=()):
    d = x2d.shape[1]
    nt = seq // tm
    nq_slab = N_Q_HEADS * HEAD_DIM // V7X_LANES
    assert tm % GRID_W == 0
    cast_in, cast_out, cast_shapes = _cast_specs(cast_jobs, n // tm)
    return pl.pallas_call(
        functools.partial(_qkv_kernel, nt=nt, tm=tm, cast_rope=tuple(job[2] for job in cast_jobs)),
        grid=(n // tm,),
        in_specs=[
            pl.BlockSpec((tm, d), lambda i: (i, 0)),
            pl.BlockSpec((1, N_MOD, d), lambda i: (row_of_tile(i), 0, 0)),
            _const_spec((1, d)),
            _const_spec(w.shape),
            _const_spec(cs_t.shape),
        ] + cast_in,
        out_specs=[
            pl.BlockSpec((nq_slab, tm, V7X_LANES), lambda i: (0, i, 0)),
            pl.BlockSpec((N_KV_HEADS, tm, V7X_LANES), lambda i: (0, i, 0)),
            pl.BlockSpec((N_KV_HEADS, tm, V7X_LANES), lambda i: (0, i, 0)),
        ] + cast_out,
        out_shape=[
            jax.ShapeDtypeStruct((nq_slab, n, V7X_LANES), _BF16),
            jax.ShapeDtypeStruct((N_KV_HEADS, n, V7X_LANES), _BF16),
            jax.ShapeDtypeStruct((N_KV_HEADS, n, V7X_LANES), _BF16),
        ] + cast_shapes,
        compiler_params=_params(("parallel",)),
        name="qkv_rope",
    )(x2d, mod, gain, w, cs_t, *[job[0] for job in cast_jobs])


def _kv_kernel(x_ref, mod_ref, g_ref, w_ref, k_ref, v_ref):
    h = _rms_mod(x_ref[...], g_ref[...], mod_ref[0, 0:1, :], mod_ref[0, 1:2, :]).astype(_BF16)
    dq = N_Q_HEADS * HEAD_DIM
    dkv = N_KV_HEADS * HEAD_DIM
    k = _dot(h, w_ref[:, dq:dq + dkv])
    v = _dot(h, w_ref[:, dq + dkv:dq + 2 * dkv])
    for c in range(dkv // V7X_LANES):
        cols = slice(c * V7X_LANES, (c + 1) * V7X_LANES)
        _key_heads_to_slabs(k[:, cols], k_ref, c, slice(None))
        _heads_to_slabs(v[:, cols], v_ref, c, slice(None))


def _ctx_kv(x2d, first_row, n, mod, gain, w):
    d = x2d.shape[1]
    assert first_row % n == 0
    slab = (N_KV_HEADS, n, V7X_LANES)
    return pl.pallas_call(
        _kv_kernel,
        grid=(1,),
        in_specs=[
            pl.BlockSpec((n, d), lambda i: (first_row // n, 0)),
            pl.BlockSpec((1, N_MOD, d), lambda i: (0, 0, 0)),
            _const_spec((1, d)),
            _const_spec(w.shape),
        ],
        out_specs=[pl.BlockSpec(slab, lambda i: (0, 0, 0)), pl.BlockSpec(slab, lambda i: (0, 0, 0))],
        out_shape=[jax.ShapeDtypeStruct(slab, _BF16), jax.ShapeDtypeStruct(slab, _BF16)],
        compiler_params=_params(("arbitrary",)),
        name="ctx_kv",
    )(x2d, mod, gain, w)


def _attn_kernel(*refs, tq, nq, cast_rope):
    ncast = len(cast_rope)
    sink_ref, q_ref, kp_ref, kc_ref, kn_ref, vp_ref, vc_ref, vn_ref, kx_ref, vx_ref = refs[:10]
    o_ref = refs[10 + ncast]
    kext, vext, s_even, s_odd = refs[11 + 2 * ncast:]
    _cast_chunks(cast_rope, refs[10:10 + ncast], refs[11 + ncast:11 + 2 * ncast])
    n = pl.program_id(1)
    blk = WINDOW
    nsub = tq // blk
    nblocks = nsub * N_KV_HEADS
    ctx_len = kx_ref.shape[1]
    kext[:, 0:blk, :] = kp_ref[...]
    kext[:, blk:blk + tq, :] = kc_ref[...]
    kext[:, blk + tq:2 * blk + tq, :] = kn_ref[...]
    vext[:, 0:blk, :] = vp_ref[...]
    vext[:, blk:blk + tq, :] = vc_ref[...]
    vext[:, blk + tq:2 * blk + tq, :] = vn_ref[...]
    qi = lax.broadcasted_iota(jnp.int32, (blk, blk), 0)
    kj = lax.broadcasted_iota(jnp.int32, (blk, blk), 1)
    lane = lax.broadcasted_iota(jnp.int32, (blk, V7X_LANES), 1)
    lo = lane < HEAD_DIM
    q_first = (lane % ROPE_HALF) < ROPE_HEAD_SPAN
    ones_win = jnp.ones((3 * blk, V7X_LANES), _BF16)
    ones_ctx = jnp.ones((ctx_len, V7X_LANES), _BF16)

    def locate(i):
        j = i // N_KV_HEADS
        return j, i % N_KV_HEADS, j * blk

    def scores(i, s_ref):
        _, hkv, row0 = locate(i)
        parts = []
        for c in range(2):
            qc = q_ref[2 * hkv + c, pl.ds(row0, blk), :]
            parts.append(jnp.where(q_first, qc, jnp.zeros_like(qc)))
            parts.append(jnp.where(q_first, jnp.zeros_like(qc), qc))
        lhs = jnp.concatenate(parts, axis=0)
        s_ref[:, 0:3 * blk] = _dot_t(lhs, kext[hkv, pl.ds(row0, 3 * blk), :])
        s_ref[:, 3 * blk:3 * blk + ctx_len] = _dot_t(lhs, kx_ref[hkv])

    def softmax_pv(i, s_ref):
        j, hkv, row0 = locate(i)
        keep_prev = kj >= qi
        keep_next = kj <= qi
        if j == 0:
            keep_prev = keep_prev & (n > 0)
        if j == nsub - 1:
            keep_next = keep_next & (n < nq - 1)
        vwin = jnp.concatenate([vext[hkv, pl.ds(row0, 3 * blk), :], ones_win], axis=1)
        vx = jnp.concatenate([vx_ref[hkv], ones_ctx], axis=1)
        p_win, p_ctx, esink = [], [], []
        for g in range(GQA_GROUP):
            sink = sink_ref[hkv * GQA_GROUP + g] * LOG2E
            grow = slice(g * blk, (g + 1) * blk)
            s0 = jnp.where(keep_prev, s_ref[grow, 0:blk], NEG)
            s1 = s_ref[grow, blk:2 * blk]
            s2 = jnp.where(keep_next, s_ref[grow, 2 * blk:3 * blk], NEG)
            sc = [s_ref[grow, 3 * blk + t * V7X_LANES:3 * blk + (t + 1) * V7X_LANES]
                  for t in range(ctx_len // V7X_LANES)]
            mm = jnp.maximum(jnp.maximum(s0, s1), s2)
            for t in sc:
                mm = jnp.maximum(mm, t)
            m = jnp.maximum(jnp.max(mm, axis=-1, keepdims=True), sink)
            p_win.append(jnp.concatenate([jnp.exp2(t - m).astype(_BF16) for t in (s0, s1, s2)], axis=1))
            p_ctx.append(jnp.concatenate([jnp.exp2(t - m).astype(_BF16) for t in sc], axis=1))
            esink.append(jnp.exp2(sink - m))
        o = _dot(jnp.concatenate(p_win, axis=0), vwin) + _dot(jnp.concatenate(p_ctx, axis=0), vx)
        og = []
        for g in range(GQA_GROUP):
            grow = slice(g * blk, (g + 1) * blk)
            den = o[grow, V7X_LANES:2 * V7X_LANES] + esink[g]
            og.append(o[grow, 0:V7X_LANES] * (1.0 / den))
        for c in range(2):
            ocol = jnp.where(lo, og[2 * c], og[2 * c + 1])
            o_ref[2 * hkv + c, pl.ds(row0, blk), :] = ocol.astype(_BF16)

    s_bufs = (s_even, s_odd)
    scores(0, s_bufs[0])
    for i in range(nblocks):
        if i + 1 < nblocks:
            scores(i + 1, s_bufs[(i + 1) % 2])
        softmax_pv(i, s_bufs[i % 2])


def _attention(sink, q, kd, vd, kx, vx, batch, seq, ctx_len, tq, cast_jobs=()):
    nq_slab, n, _ = q.shape
    nq = seq // tq
    hb = tq // WINDOW
    sb = seq // WINDOW
    last = n // WINDOW - 1
    kv = N_KV_HEADS

    def cur():
        return pl.BlockSpec((kv, tq, V7X_LANES), lambda b, i: (0, b * nq + i, 0))

    def prev():
        return pl.BlockSpec((kv, WINDOW, V7X_LANES), lambda b, i: (0, jnp.maximum(b * sb + i * hb - 1, 0), 0))

    def nxt():
        return pl.BlockSpec((kv, WINDOW, V7X_LANES), lambda b, i: (0, jnp.minimum(b * sb + (i + 1) * hb, last), 0))

    def ctx():
        return pl.BlockSpec((kv, ctx_len, V7X_LANES), lambda b, i: (0, b, 0))

    cast_in, cast_out, cast_shapes = _cast_specs(cast_jobs, batch * nq, lambda b, i: b * nq + i)
    kernel = functools.partial(_attn_kernel, tq=tq, nq=nq, cast_rope=tuple(job[2] for job in cast_jobs))
    return pl.pallas_call(
        kernel,
        grid=(batch, nq),
        in_specs=[
            pl.BlockSpec(memory_space=pltpu.SMEM),
            pl.BlockSpec((nq_slab, tq, V7X_LANES), lambda b, i: (0, b * nq + i, 0)),
            prev(), cur(), nxt(), prev(), cur(), nxt(), ctx(), ctx(),
        ] + cast_in,
        out_specs=[pl.BlockSpec((nq_slab, tq, V7X_LANES), lambda b, i: (0, b * nq + i, 0))] + cast_out,
        out_shape=[jax.ShapeDtypeStruct(q.shape, _BF16)] + cast_shapes,
        scratch_shapes=[
            pltpu.VMEM((kv, tq + 2 * WINDOW, V7X_LANES), _BF16),
            pltpu.VMEM((kv, tq + 2 * WINDOW, V7X_LANES), _BF16),
            pltpu.VMEM((GQA_GROUP * WINDOW, 3 * WINDOW + ctx_len), _F32),
            pltpu.VMEM((GQA_GROUP * WINDOW, 3 * WINDOW + ctx_len), _F32),
        ],
        compiler_params=_params(("parallel", "parallel")),
        name="band_attn",
    )(sink, q, kd, kd, kd, vd, vd, vd, kx, vx, *[job[0] for job in cast_jobs])


def _gate_weights(r_w, r_b, i_w, i_b):
    heads_per_half = V7X_MXU_DIM // LRU_HEAD_DIM
    eye = jnp.eye(heads_per_half, dtype=_F32)

    def halves(w):
        w = w.reshape(2, D_LRU // V7X_MXU_DIM, heads_per_half, LRU_HEAD_DIM, LRU_HEAD_DIM)
        bd = jnp.einsum('dxhij,hk->dxhikj', w, eye)
        return bd.reshape(2, D_LRU // V7X_MXU_DIM, V7X_MXU_DIM, V7X_MXU_DIM)

    w = (0.5 * jnp.concatenate([halves(r_w), halves(i_w)], axis=-1)).astype(_BF16)
    rb = r_b.reshape(2, D_LRU // V7X_MXU_DIM, 1, V7X_MXU_DIM)
    ib = i_b.reshape(2, D_LRU // V7X_MXU_DIM, 1, V7X_MXU_DIM)
    return w, 0.5 * jnp.concatenate([rb, ib], axis=-1)


def _rope_table(seq):
    grid_rows = seq // GRID_W
    inv_freq = ROPE_BASE ** (-jnp.arange(ROPE_FREQS, dtype=_F32) / ROPE_FREQS)
    pos = jnp.concatenate([jnp.arange(GRID_W), jnp.arange(grid_rows)]).astype(_F32)
    ang = pos[:, None] * inv_freq
    cos, sin = jnp.cos(ang), jnp.sin(ang)
    is_col = (jnp.arange(GRID_W + grid_rows) < GRID_W)[:, None]
    zero = jnp.zeros_like(cos)
    table = jnp.concatenate([jnp.where(is_col, zero, cos), jnp.where(is_col, cos, zero),
                             jnp.where(is_col, zero, sin), jnp.where(is_col, sin, zero)], axis=1)
    return jnp.pad(table, ((0, 0), (0, V7X_LANES - 4 * ROPE_FREQS)))


def kernel(x, c, ctx, c_ctx, ada_w, ada_b, norm_mix_g, norm_ffn_g, norm_final_g, ffn_w_in, ffn_w_out,
           rc_w_in, rc_conv_a_w, rc_conv_a_b, rc_gate_r_w, rc_gate_r_b, rc_gate_i_w, rc_gate_i_b,
           rc_lambda, rc_conv_b_w, rc_w_out, at_w_qkv, at_sink, at_w_out):
    batch, seq, d = x.shape
    ctx_len = ctx.shape[1]
    tm = 1024
    tm_rc = 512
    tm_ffn = 1024
    tq = 1024
    tc = 1024
    assert seq % tm_ffn == 0 and seq % tm_rc == 0 and seq % tm == 0 and seq % tq == 0 and seq % tc == 0
    assert ctx_len % V7X_LANES == 0

    xl = x.reshape(batch * seq, d)
    xc = ctx.reshape(batch * ctx_len, d)

    cond = jnp.concatenate([c_ctx[None], c, jnp.zeros((8 - 1 - batch, d), _F32)], axis=0)
    mod = _modulation(cond, ada_w, ada_b).reshape(ada_w.shape[0], 8, N_MOD, d)

    def lat_row(tile):
        return lambda i: 1 + i // (seq // tile)

    ctx_row = lambda i: 0

    w_gate, b_gate = _gate_weights(rc_gate_r_w[0], rc_gate_r_b[0], rc_gate_i_w[0], rc_gate_i_b[0])
    lam = rc_lambda[0].reshape(2, 1, D_LRU)
    conv_a_b = rc_conv_a_b[0].reshape(1, D_LRU)
    gmix0 = norm_mix_g[0].reshape(1, d)
    gffn0 = norm_ffn_g[0].reshape(1, d)

    xa_c, p_c, w_in = _project(xc, mod[0], gmix0, None, batch * ctx_len, ctx_row, cast_jobs=[(rc_w_in, 0, 0)])
    xa_l, p_l, w1_0, w2_0, w_out0, w_qkv = _project(
        xl, mod[0], gmix0, w_in, tm, lat_row(tm),
        cast_jobs=[(ffn_w_in, 0, 0), (ffn_w_out, 0, 0), (rc_w_out, 0, 0),
                   (at_w_qkv, 0, (N_Q_HEADS + N_KV_HEADS) * HEAD_DIM)])

    h0 = jnp.zeros((batch, 2, V7X_SUBLANES, D_LRU), _F32)
    hf_c, hb_c, h_ctx = _scan(xa_c, h0, rc_conv_a_w[0], conv_a_b, w_gate, b_gate, lam, batch, ctx_len, ctx_len)
    hf_l, hb_l, _ = _scan(xa_l, h_ctx, rc_conv_a_w[0], conv_a_b, w_gate, b_gate, lam, batch, seq, tc)

    n_lat = batch * seq
    x_all = _rc_out(xl, hf_l, hb_l, p_l, xc, hf_c, hb_c, p_c, rc_conv_b_w[0], w_out0, mod[0], gffn0,
                    w1_0, w2_0, tm_rc, seq, ctx_len)

    gmix1 = norm_mix_g[1].reshape(1, d)
    gffn1 = norm_ffn_g[1].reshape(1, d)

    q, kd, vd, w2_1 = _qkv(x_all, n_lat, mod[1], gmix1, w_qkv, _rope_table(seq), tm, seq, lat_row(tm),
                           cast_jobs=[(ffn_w_out, 1, 0)])
    kx, vx = _ctx_kv(x_all, n_lat, batch * ctx_len, mod[1], gmix1, w_qkv)
    o, w1_1, w_out1 = _attention(at_sink[0], q, kd, vd, kx, vx, batch, seq, ctx_len, tq,
                                 cast_jobs=[(ffn_w_in, 1, 0), (at_w_out, 0, 0)])

    out = _attn_out(x_all, n_lat, o, w_out1, mod[1], gffn1, w1_1, w2_1,
                    norm_final_g.reshape(1, d), tm_ffn, lat_row(tm_ffn))
    return out.reshape(batch, seq, d)
```

```python
import functools

import jax
import jax.numpy as jnp
from jax import lax
from jax.experimental import pallas as pl
from jax.experimental.pallas import tpu as pltpu

D_MODEL = 1024
N_MOD = 6
EPS = 1e-6
NEG = -1e30
D_LRU = 512
D_SC = 512
LRU_HEADS = 8
LRU_HEAD_DIM = 64
LRU_C = 8.0
RC_IN_WIDTH = 2 * D_LRU + 3 * D_SC
HEAD_DIM = 64
N_Q_HEADS = 16
N_KV_HEADS = 4
GQA_GROUP = 4
WINDOW = 128
GRID_W = 64
ROPE_BASE = 10000.0
ROPE_FREQS = 16
D_FF = 2816
LOG2E = 1.4426950408889634

V7X_LANES = 128
V7X_SUBLANES = 8
V7X_MXU_DIM = 256
V7X_VMEM_LIMIT = 60 * 1024 * 1024

HALO = V7X_SUBLANES
HALO_BF16 = 2 * V7X_SUBLANES
N_SLAB = D_LRU // V7X_LANES
SCAN_SHIFTS = (1, 2, 4)
FFN_CHUNKS = ((0, 2816),)
FFN_SUB_ROWS = 256
PROJ_SUB_ROWS = 256

_BF16 = jnp.bfloat16
_F32 = jnp.float32


def _dot(a, b):
    return jnp.dot(a, b, preferred_element_type=_F32)


def _dot_t(a, b):
    return lax.dot_general(a, b, (((1,), (1,)), ((), ())), preferred_element_type=_F32)


def _sigmoid(x):
    return 0.5 * jnp.tanh(0.5 * x) + 0.5


def _gelu_tanh(x):
    return 0.5 * x * (1.0 + jnp.tanh(0.7978845608028654 * (x + 0.044715 * (x * x * x))))


def _rms_mod(x, gain, shift, scale):
    ms = jnp.mean(x * x, axis=-1, keepdims=True)
    return (x * lax.rsqrt(ms + EPS) * gain) * (1.0 + scale) + shift


def _const_spec(shape, index=None):
    idx = (0,) * len(shape) if index is None else index
    return pl.BlockSpec(shape, lambda *_: idx, pipeline_mode=pl.Buffered(1))


def _params(sem):
    return pltpu.CompilerParams(dimension_semantics=sem, vmem_limit_bytes=V7X_VMEM_LIMIT)


def _cast_specs(jobs, nsteps, step_of=lambda i: i):
    ins, outs, shapes = [], [], []
    for w, layer, _ in jobs:
        _, r, c = w.shape
        assert r % (nsteps * HALO_BF16) == 0, (w.shape, nsteps)
        chunk = r // nsteps
        ins.append(pl.BlockSpec((1, chunk, c), lambda *idx, layer=layer: (layer, step_of(*idx), 0)))
        outs.append(pl.BlockSpec((chunk, c), lambda *idx: (step_of(*idx), 0)))
        shapes.append(jax.ShapeDtypeStruct((r, c), _BF16))
    return ins, outs, shapes


def _cast_chunks(rope_cols_per_job, cast_in, cast_out):
    for rope_cols, src, dst in zip(rope_cols_per_job, cast_in, cast_out):
        moves = _rope_lane_moves((src.shape[1], V7X_LANES)) if rope_cols else None
        for c0 in range(0, rope_cols, V7X_LANES):
            dst[:, c0:c0 + V7X_LANES] = _to_rope_lanes(src[0, :, c0:c0 + V7X_LANES], moves).astype(_BF16)
        dst[:, rope_cols:] = src[0, :, rope_cols:].astype(_BF16)


def _mod_kernel(c_ref, w_ref, b_ref, o_ref):
    c = c_ref[...]
    s = (c * _sigmoid(c)).astype(_BF16)
    o_ref[0] = _dot(s, w_ref[0].astype(_BF16)) + b_ref[0]


def _modulation(cond, ada_w, ada_b):
    depth, d, n = ada_w.shape
    bn = 1536
    return pl.pallas_call(
        _mod_kernel,
        grid=(depth, n // bn),
        in_specs=[
            pl.BlockSpec((8, d), lambda l, j: (0, 0)),
            pl.BlockSpec((1, d, bn), lambda l, j: (l, 0, j)),
            pl.BlockSpec((1, 1, bn), lambda l, j: (l, 0, j)),
        ],
        out_specs=pl.BlockSpec((1, 8, bn), lambda l, j: (l, 0, j)),
        out_shape=jax.ShapeDtypeStruct((depth, 8, n), _F32),
        compiler_params=_params(("arbitrary", "arbitrary")),
        name="adaln_mod",
    )(cond, ada_w, ada_b.reshape(depth, 1, n))


def _proj_kernel(*refs, cast_rope, own_weight):
    ncast = len(cast_rope)
    nin = 4 if own_weight else 3
    x_ref, mod_ref, g_ref = refs[:3]
    xa_ref, rest_ref = refs[nin + ncast:nin + 2 + ncast]
    cast_out = refs[nin + 2 + ncast:]
    _cast_chunks(cast_rope, refs[nin:nin + ncast], cast_out)
    w_ref = refs[3] if own_weight else cast_out[0]
    h_all = _rms_mod(x_ref[...], g_ref[...], mod_ref[0, 0:1, :], mod_ref[0, 1:2, :]).astype(_BF16)
    tm = h_all.shape[0]
    sub = min(PROJ_SUB_ROWS, tm)
    for r0 in range(0, tm, sub):
        rows = slice(r0, r0 + sub)
        h = h_all[rows, :]
        xa = _dot(h, w_ref[:, 0:D_LRU])
        for c in range(N_SLAB):
            xa_ref[c, rows, :] = xa[:, c * V7X_LANES:(c + 1) * V7X_LANES]
        ga = _dot(h, w_ref[:, D_LRU:2 * D_LRU])
        rest_ref[rows, 0:D_LRU] = _gelu_tanh(ga).astype(_BF16)
        rest_ref[rows, D_LRU:D_LRU + D_SC] = _dot(h, w_ref[:, 2 * D_LRU:2 * D_LRU + D_SC]).astype(_BF16)
        cv = _dot(h, w_ref[:, 2 * D_LRU + D_SC:2 * D_LRU + 3 * D_SC])
        rest_ref[rows, D_LRU + D_SC:D_LRU + 2 * D_SC] = (cv[:, 0:D_SC] * cv[:, D_SC:2 * D_SC]).astype(_BF16)


def _project(x2d, mod, gain, w, tm, row_of_tile, cast_jobs=()):
    n, d = x2d.shape
    nrest = D_LRU + 2 * D_SC
    assert w is not None or n == tm
    cast_in, cast_out, cast_shapes = _cast_specs(cast_jobs, n // tm)
    weight = [] if w is None else [w]
    return pl.pallas_call(
        functools.partial(_proj_kernel, cast_rope=tuple(job[2] for job in cast_jobs), own_weight=w is not None),
        grid=(n // tm,),
        in_specs=[
            pl.BlockSpec((tm, d), lambda i: (i, 0)),
            pl.BlockSpec((1, N_MOD, d), lambda i: (row_of_tile(i), 0, 0)),
            _const_spec((1, d)),
        ] + [_const_spec(a.shape) for a in weight] + cast_in,
        out_specs=[
            pl.BlockSpec((N_SLAB, tm, V7X_LANES), lambda i: (0, i, 0)),
            pl.BlockSpec((tm, nrest), lambda i: (i, 0)),
        ] + cast_out,
        out_shape=[
            jax.ShapeDtypeStruct((N_SLAB, n, V7X_LANES), _F32),
            jax.ShapeDtypeStruct((n, nrest), _BF16),
        ] + cast_shapes,
        compiler_params=_params(("arbitrary",)),
        name="rc_in_proj",
    )(x2d, mod, gain, *weight, *[job[0] for job in cast_jobs])


def _scan_kernel(xm_f, xp_f, xn_f, xm_b, xp_b, xn_b, cw_ref, cb_ref, wg_ref, bg_ref, lam_ref, h0_ref,
                 hf_ref, hb_ref, hl_ref, xs_scr, carry_scr, *level_scr, nc, tc):
    j = pl.program_id(1)
    lo = HALO
    ngroups = tc // V7X_SUBLANES
    nlev = len(SCAN_SHIFTS)
    a_lv = list(level_scr[:nlev])
    b_lv = list(level_scr[nlev:])
    pad_rows = (slice(0, HALO), slice(lo + tc, lo + tc + HALO))

    @pl.when(j == 0)
    def _():
        carry_scr[...] = h0_ref[0]
        for d in range(2):
            for ref in a_lv:
                ref[d, :, pad_rows[d], :] = jnp.ones((N_SLAB, HALO, V7X_LANES), _F32)
            for ref in b_lv:
                ref[d, :, pad_rows[d], :] = jnp.zeros((N_SLAB, HALO, V7X_LANES), _F32)

    cw = cw_ref[...]
    cb = cb_ref[...]

    def coeffs(d, xm, xp, xn, has_prev, has_next):
        xs_scr[d, :, 0:HALO, :] = jnp.where(has_prev, xp[...], 0.0)
        xs_scr[d, :, lo:lo + tc, :] = xm[...]
        xs_scr[d, :, lo + tc:lo + tc + HALO, :] = jnp.where(has_next, xn[...], 0.0)
        xcs = []
        for c in range(N_SLAB):
            lanes = slice(c * V7X_LANES, (c + 1) * V7X_LANES)
            acc = cb[:, lanes]
            for k in range(4):
                acc = acc + cw[k:k + 1, lanes] * xs_scr[d, c, lo - 2 + k:lo - 2 + k + tc, :]
            xcs.append(acc)
        lam = lam_ref[d]
        nlam = -lam
        softplus = jnp.maximum(nlam, 0.0) + jnp.log(1.0 + jnp.exp(-jnp.abs(nlam)))
        hrate = (-0.5 * LRU_C * LOG2E) * softplus
        half = V7X_MXU_DIM
        per_half = half // V7X_LANES
        for hh in range(D_LRU // half):
            xch = jnp.concatenate(xcs[hh * per_half:(hh + 1) * per_half], axis=1)
            z = _dot(xch.astype(_BF16), wg_ref[d, hh]) + bg_ref[d, hh]
            for cc in range(per_half):
                c = hh * per_half + cc
                lanes = slice(c * V7X_LANES, (c + 1) * V7X_LANES)
                tr = jnp.tanh(z[:, cc * V7X_LANES:(cc + 1) * V7X_LANES])
                ig = 0.5 * jnp.tanh(z[:, half + cc * V7X_LANES:half + (cc + 1) * V7X_LANES]) + 0.5
                a = jnp.exp2(tr * hrate[:, lanes] + hrate[:, lanes])
                one_m_a2 = jnp.maximum(1.0 - a * a, 1e-12)
                mult = one_m_a2 * lax.rsqrt(one_m_a2)
                a_lv[0][d, c, lo:lo + tc, :] = a
                b_lv[0][d, c, lo:lo + tc, :] = mult * ig * xcs[c]

    coeffs(0, xm_f, xp_f, xn_f, j > 0, j < nc - 1)
    coeffs(1, xm_b, xp_b, xn_b, j < nc - 1, j > 0)

    def step(g, carry):
        out = []
        for d, h_ref in ((0, hf_ref), (1, hb_ref)):
            grp = g if d == 0 else ngroups - 1 - g
            row0 = pl.multiple_of(lo + grp * V7X_SUBLANES, V7X_SUBLANES)
            cur = pl.ds(row0, V7X_SUBLANES)
            for c in range(N_SLAB):
                a = a_lv[0][d, c, cur, :]
                b = b_lv[0][d, c, cur, :]
                for lvl, shift in enumerate(SCAN_SHIFTS):
                    sh = pl.ds(row0 + (shift if d else -shift), V7X_SUBLANES)
                    b = a * b_lv[lvl][d, c, sh, :] + b
                    a = a * a_lv[lvl][d, c, sh, :]
                    if lvl + 1 < nlev:
                        a_lv[lvl + 1][d, c, cur, :] = a
                        b_lv[lvl + 1][d, c, cur, :] = b
                h = a * carry[d * N_SLAB + c] + b
                out_rows = pl.ds(pl.multiple_of(grp * V7X_SUBLANES, V7X_SUBLANES), V7X_SUBLANES)
                h_ref[out_rows, c * V7X_LANES:(c + 1) * V7X_LANES] = h
                out.append(h)
        return tuple(out)

    init = tuple(carry_scr[d, :, c * V7X_LANES:(c + 1) * V7X_LANES]
                 for d in range(2) for c in range(N_SLAB))
    carry = lax.fori_loop(0, ngroups, step, init, unroll=8)

    edge_row = (V7X_SUBLANES - 1, 0)
    data_edge = (slice(tc, tc + HALO), slice(lo, lo + HALO))
    for d in range(2):
        for c in range(N_SLAB):
            lanes = slice(c * V7X_LANES, (c + 1) * V7X_LANES)
            h = carry[d * N_SLAB + c]
            carry_scr[d, :, lanes] = h
            hl_ref[0, d, :, lanes] = jnp.broadcast_to(h[edge_row[d]:edge_row[d] + 1, :], h.shape)
        for ref in a_lv + b_lv:
            ref[d, :, pad_rows[d], :] = ref[d, :, data_edge[d], :]


def _scan(xa, h0, conv_w, conv_b, w_gate, b_gate, lam, batch, seq, tc):
    n = xa.shape[1]
    nc = seq // tc
    tb = tc // HALO
    sb = seq // HALO
    last = n // HALO - 1

    def fwd(b, j):
        return j

    def bwd(b, j):
        return nc - 1 - j

    def main(cf):
        return pl.BlockSpec((N_SLAB, tc, V7X_LANES), lambda b, j: (0, b * nc + cf(b, j), 0))

    def prev(cf):
        return pl.BlockSpec((N_SLAB, HALO, V7X_LANES),
                            lambda b, j: (0, jnp.maximum(b * sb + cf(b, j) * tb - 1, 0), 0))

    def nxt(cf):
        return pl.BlockSpec((N_SLAB, HALO, V7X_LANES),
                            lambda b, j: (0, jnp.minimum(b * sb + (cf(b, j) + 1) * tb, last), 0))

    kernel = functools.partial(_scan_kernel, nc=nc, tc=tc)
    nlev = len(SCAN_SHIFTS)
    return pl.pallas_call(
        kernel,
        grid=(batch, nc),
        in_specs=[
            main(fwd), prev(fwd), nxt(fwd), main(bwd), prev(bwd), nxt(bwd),
            _const_spec((4, D_LRU)),
            _const_spec((1, D_LRU)),
            _const_spec(w_gate.shape),
            _const_spec(b_gate.shape),
            _const_spec((2, 1, D_LRU)),
            pl.BlockSpec((1, 2, V7X_SUBLANES, D_LRU), lambda b, j: (b, 0, 0, 0)),
        ],
        out_specs=[
            pl.BlockSpec((tc, D_LRU), lambda b, j: (b * nc + j, 0)),
            pl.BlockSpec((tc, D_LRU), lambda b, j: (b * nc + nc - 1 - j, 0)),
            pl.BlockSpec((1, 2, V7X_SUBLANES, D_LRU), lambda b, j: (b, 0, 0, 0)),
        ],
        out_shape=[
            jax.ShapeDtypeStruct((n, D_LRU), _F32),
            jax.ShapeDtypeStruct((n, D_LRU), _F32),
            jax.ShapeDtypeStruct((batch, 2, V7X_SUBLANES, D_LRU), _F32),
        ],
        scratch_shapes=[
            pltpu.VMEM((2, N_SLAB, tc + 2 * HALO, V7X_LANES), _F32),
            pltpu.VMEM((2, V7X_SUBLANES, D_LRU), _F32),
        ] + [pltpu.VMEM((2, N_SLAB, tc + 2 * HALO, V7X_LANES), _F32) for _ in range(2 * nlev)],
        compiler_params=_params(("arbitrary", "arbitrary")),
        name="rglru_scan",
    )(xa, xa, xa, xa, xa, xa, conv_w, conv_b, w_gate, b_gate, lam, h0)


def _ffn_norm(x, y, mod_ref, gffn_ref):
    x1 = x + mod_ref[0, 2:3, :] * y
    h = _rms_mod(x1, gffn_ref[...], mod_ref[0, 3:4, :], mod_ref[0, 4:5, :]).astype(_BF16)
    return x1, h


def _ffn_apply(x1, h, mod_ref, w1_ref, w2_ref, gfin_ref):
    acc = None
    for c0, cw in FFN_CHUNKS:
        g = _dot(h, w1_ref[:, c0:c0 + cw])
        u = _dot(h, w1_ref[:, D_FF + c0:D_FF + c0 + cw])
        act = (g * _sigmoid(g) * u).astype(_BF16)
        part = _dot(act, w2_ref[c0:c0 + cw, :])
        acc = part if acc is None else acc + part
    x2 = x1 + mod_ref[0, 5:6, :] * acc
    if gfin_ref is not None:
        ms = jnp.mean(x2 * x2, axis=-1, keepdims=True)
        x2 = x2 * lax.rsqrt(ms + EPS) * gfin_ref[...]
    return x2


def _ffn_weight_specs(w1, w2):
    return [_const_spec(w1.shape), _const_spec(w2.shape)]


def _rc_out_kernel(x_ref, hf_ref, hb_ref, gg_ref, bg_ref, um_ref, up_ref, un_ref,
                   xc_ref, hfc_ref, hbc_ref, pc_ref,
                   cbw_ref, wo_ref, mod_ref, gffn_ref, w1_ref, w2_ref, o_ref, *, nt, tm, n_lat, ctx_len):
    i = pl.program_id(0)
    is_ctx = i == n_lat
    pick = lambda lat, ctx: jnp.where(is_ctx, ctx, lat)
    has_prev = jnp.logical_and((i % nt) > 0, jnp.logical_not(is_ctx))
    has_next = jnp.logical_and((i % nt) < nt - 1, jnp.logical_not(is_ctx))
    prev = jnp.where(has_prev, up_ref[...].astype(_F32), 0.0)
    nxt = jnp.where(has_next, un_ref[...].astype(_F32), 0.0)
    u_mid = pick(um_ref[...], pc_ref[:, 2 * D_SC:3 * D_SC]).astype(_F32)
    uext = jnp.concatenate([prev, u_mid, nxt], axis=0)
    cbw = cbw_ref[...]
    sub = min(FFN_SUB_ROWS, tm // 2)
    normed = []
    for r0 in range(0, tm, sub):
        rows = slice(r0, r0 + sub)
        hsum = pick(hf_ref[rows, :], hfc_ref[rows, :]) + pick(hb_ref[rows, :], hbc_ref[rows, :])
        ya = (hsum * pick(gg_ref[rows, :], pc_ref[rows, 0:D_LRU]).astype(_F32)).astype(_BF16)
        seq_row = (lax.broadcasted_iota(jnp.int32, (sub, 1), 0) + r0) % ctx_len
        tap_ok = (jnp.logical_or(jnp.logical_not(is_ctx), seq_row > 0), None,
                  jnp.logical_or(jnp.logical_not(is_ctx), seq_row < ctx_len - 1))
        conv = None
        for k in range(3):
            first = HALO_BF16 - 1 + k + r0
            term = cbw[k:k + 1, :] * uext[first:first + sub, :]
            if tap_ok[k] is not None:
                term = jnp.where(tap_ok[k], term, 0.0)
            conv = term if conv is None else conv + term
        yb = (pick(bg_ref[rows, :], pc_ref[rows, D_LRU:D_LRU + D_SC]).astype(_F32) * conv).astype(_BF16)
        y = _dot(jnp.concatenate([ya, yb], axis=1), wo_ref[...])
        normed.append(_ffn_norm(pick(x_ref[rows, :], xc_ref[rows, :]), y, mod_ref, gffn_ref))
    for k, r0 in enumerate(range(0, tm, sub)):
        x1, h = normed[k]
        o_ref[r0:r0 + sub, :] = _ffn_apply(x1, h, mod_ref, w1_ref, w2_ref, None)


def _rc_out(x2d, hf, hb, p, xc2d, hf_c, hb_c, p_c, conv_b_w, w_out, mod, gffn, w1, w2, tm, seq, ctx_len):
    n, d = x2d.shape
    assert xc2d.shape[0] == tm and tm % ctx_len == 0
    nt = seq // tm
    n_lat = n // tm
    tb = tm // HALO_BF16
    last = n // HALO_BF16 - 1
    lat = lambda i: jnp.minimum(i, n_lat - 1)

    def col(c):
        return pl.BlockSpec((tm, D_SC), lambda i: (lat(i), c))

    kernel = functools.partial(_rc_out_kernel, nt=nt, tm=tm, n_lat=n_lat, ctx_len=ctx_len)
    return pl.pallas_call(
        kernel,
        grid=(n_lat + 1,),
        in_specs=[
            pl.BlockSpec((tm, d), lambda i: (lat(i), 0)),
            pl.BlockSpec((tm, D_LRU), lambda i: (lat(i), 0)),
            pl.BlockSpec((tm, D_LRU), lambda i: (lat(i), 0)),
            col(0), col(1), col(2),
            pl.BlockSpec((HALO_BF16, D_SC), lambda i: (jnp.maximum(lat(i) * tb - 1, 0), 2)),
            pl.BlockSpec((HALO_BF16, D_SC), lambda i: (jnp.minimum((lat(i) + 1) * tb, last), 2)),
            _const_spec(xc2d.shape), _const_spec(hf_c.shape), _const_spec(hb_c.shape), _const_spec(p_c.shape),
            _const_spec((3, D_SC)),
            _const_spec(w_out.shape),
            pl.BlockSpec((1, N_MOD, d), lambda i: (jnp.where(i == n_lat, 0, 1 + i // nt), 0, 0)),
            _const_spec((1, d)),
        ] + _ffn_weight_specs(w1, w2),
        out_specs=pl.BlockSpec((tm, d), lambda i: (i, 0)),
        out_shape=jax.ShapeDtypeStruct((n + tm, d), _F32),
        compiler_params=_params(("arbitrary",)),
        name="rc_out_ffn",
    )(x2d, hf, hb, p, p, p, p, p, xc2d, hf_c, hb_c, p_c, conv_b_w, w_out, mod, gffn, w1, w2)


def _attn_out_kernel(x_ref, o_in_ref, wo_ref, mod_ref, gffn_ref, gfin_ref, w1_ref, w2_ref, o_ref):
    tm = x_ref.shape[0]
    sub = min(FFN_SUB_ROWS, tm)
    normed = []
    for r0 in range(0, tm, sub):
        rows = slice(r0, r0 + sub)
        o_in = jnp.concatenate([o_in_ref[c, rows, :] for c in range(o_in_ref.shape[0])], axis=1)
        normed.append(_ffn_norm(x_ref[rows, :], _dot(o_in, wo_ref[...]), mod_ref, gffn_ref))
    for k, r0 in enumerate(range(0, tm, sub)):
        x1, h = normed[k]
        o_ref[r0:r0 + sub, :] = _ffn_apply(x1, h, mod_ref, w1_ref, w2_ref, gfin_ref)


def _attn_out(x2d, n, o, w_out, mod, gffn, w1, w2, gfin, tm, row_of_tile):
    d = x2d.shape[1]
    return pl.pallas_call(
        _attn_out_kernel,
        grid=(n // tm,),
        in_specs=[
            pl.BlockSpec((tm, d), lambda i: (i, 0)),
            pl.BlockSpec((o.shape[0], tm, V7X_LANES), lambda i: (0, i, 0)),
            _const_spec(w_out.shape),
            pl.BlockSpec((1, N_MOD, d), lambda i: (row_of_tile(i), 0, 0)),
            _const_spec((1, d)),
            _const_spec((1, d)),
        ] + _ffn_weight_specs(w1, w2),
        out_specs=pl.BlockSpec((tm, d), lambda i: (i, 0)),
        out_shape=jax.ShapeDtypeStruct((n, d), _F32),
        compiler_params=_params(("parallel",)),
        name="attn_out_ffn",
    )(x2d, o, w_out, mod, gffn, gfin, w1, w2)


ROPE_HALF = V7X_LANES // 2
ROPE_HEAD_SPAN = ROPE_HALF // 2


def _rope_lane_moves(shape):
    lane = lax.broadcasted_iota(jnp.int32, shape, 1)
    dst = lane // ROPE_FREQS
    src = ((dst % 4) // 2) * 4 + (dst % 2) * 2 + dst // 4
    return {delta: (src - dst) == delta for delta in range(-3, 4) if delta != 0}


def _to_rope_lanes(x, moves):
    out = x
    for delta, mask in moves.items():
        shifted = pltpu.roll(x, (-delta * ROPE_FREQS) % V7X_LANES, 1)
        out = jnp.where(mask, shifted, out)
    return out


def _rope(x, cos, sin_signed):
    return x * cos + pltpu.roll(x, ROPE_HALF, 1) * sin_signed


def _rope_lane_tables(cs):
    lane = lax.broadcasted_iota(jnp.int32, cs.shape, 1)
    span = ROPE_HEAD_SPAN
    cos = jnp.where(lane < span, cs, pltpu.roll(cs, span, 1))
    sin = jnp.where(lane < span, pltpu.roll(cs, V7X_LANES - span, 1), cs)
    first = lane < ROPE_HALF
    cos = jnp.where(first, cos, pltpu.roll(cos, ROPE_HALF, 1))
    sin = jnp.where(first, sin, pltpu.roll(sin, ROPE_HALF, 1))
    return cos, jnp.where(first, -sin, sin)


def _key_heads_to_slabs(x, ref, c, rows):
    lane = lax.broadcasted_iota(jnp.int32, x.shape, 1)
    is_first = (lane % ROPE_HALF) < ROPE_HEAD_SPAN
    ref[2 * c, rows, :] = jnp.where(is_first, x, pltpu.roll(x, ROPE_HEAD_SPAN, 1)).astype(_BF16)
    ref[2 * c + 1, rows, :] = jnp.where(is_first, pltpu.roll(x, V7X_LANES - ROPE_HEAD_SPAN, 1), x).astype(_BF16)


def _heads_to_slabs(x, ref, c, rows):
    first = lax.broadcasted_iota(jnp.int32, x.shape, 1) < HEAD_DIM
    swapped = pltpu.roll(x, HEAD_DIM, 1)
    ref[2 * c, rows, :] = jnp.where(first, x, swapped).astype(_BF16)
    ref[2 * c + 1, rows, :] = jnp.where(first, swapped, x).astype(_BF16)


def _qkv_kernel(*refs, nt, tm, cast_rope):
    ncast = len(cast_rope)
    x_ref, mod_ref, g_ref, w_ref, tab_ref = refs[:5]
    q_ref, k_ref, v_ref = refs[5 + ncast:8 + ncast]
    _cast_chunks(cast_rope, refs[5:5 + ncast], refs[8 + ncast:])
    h = _rms_mod(x_ref[...], g_ref[...], mod_ref[0, 0:1, :], mod_ref[0, 1:2, :]).astype(_BF16)
    col_terms = tab_ref[0:GRID_W, :]
    first_grid_row = GRID_W + (pl.program_id(0) % nt) * (tm // GRID_W)
    cs = jnp.concatenate([col_terms + tab_ref[pl.ds(first_grid_row + g, 1), :] for g in range(tm // GRID_W)],
                         axis=0)
    cos_t, sin_t = _rope_lane_tables(cs)
    dq = N_Q_HEADS * HEAD_DIM
    dkv = N_KV_HEADS * HEAD_DIM
    scale = HEAD_DIM ** -0.5 * LOG2E
    sub = min(PROJ_SUB_ROWS, tm)
    for r0 in range(0, tm, sub):
        rows = slice(r0, r0 + sub)
        hs, cos, sin = h[rows, :], cos_t[rows, :], sin_t[rows, :]
        q = _dot(hs, w_ref[:, 0:dq])
        for c in range(dq // V7X_LANES):
            cols = slice(c * V7X_LANES, (c + 1) * V7X_LANES)
            q_ref[c, rows, :] = (_rope(q[:, cols], cos, sin) * scale).astype(_BF16)
        k = _dot(hs, w_ref[:, dq:dq + dkv])
        v = _dot(hs, w_ref[:, dq + dkv:dq + 2 * dkv])
        for c in range(dkv // V7X_LANES):
            cols = slice(c * V7X_LANES, (c + 1) * V7X_LANES)
            _key_heads_to_slabs(_rope(k[:, cols], cos, sin), k_ref, c, rows)
            _heads_to_slabs(v[:, cols], v_ref, c, rows)


def _qkv(x2d, n, mod, gain, w, cs_t, tm, seq, row_of_tile, cast_jobs=()):
    d = x2d.shape[1]
    nt = seq // tm
    nq_slab = N_Q_HEADS * HEAD_DIM // V7X_LANES
    assert tm % GRID_W == 0
    cast_in, cast_out, cast_shapes = _cast_specs(cast_jobs, n // tm)
    return pl.pallas_call(
        functools.partial(_qkv_kernel, nt=nt, tm=tm, cast_rope=tuple(job[2] for job in cast_jobs)),
        grid=(n // tm,),
        in_specs=[
            pl.BlockSpec((tm, d), lambda i: (i, 0)),
            pl.BlockSpec((1, N_MOD, d), lambda i: (row_of_tile(i), 0, 0)),
            _const_spec((1, d)),
            _const_spec(w.shape),
            _const_spec(cs_t.shape),
        ] + cast_in,
        out_specs=[
            pl.BlockSpec((nq_slab, tm, V7X_LANES), lambda i: (0, i, 0)),
            pl.BlockSpec((N_KV_HEADS, tm, V7X_LANES), lambda i: (0, i, 0)),
            pl.BlockSpec((N_KV_HEADS, tm, V7X_LANES), lambda i: (0, i, 0)),
        ] + cast_out,
        out_shape=[
            jax.ShapeDtypeStruct((nq_slab, n, V7X_LANES), _BF16),
            jax.ShapeDtypeStruct((N_KV_HEADS, n, V7X_LANES), _BF16),
            jax.ShapeDtypeStruct((N_KV_HEADS, n, V7X_LANES), _BF16),
        ] + cast_shapes,
        compiler_params=_params(("parallel",)),
        name="qkv_rope",
    )(x2d, mod, gain, w, cs_t, *[job[0] for job in cast_jobs])


def _kv_kernel(x_ref, mod_ref, g_ref, w_ref, k_ref, v_ref):
    h = _rms_mod(x_ref[...], g_ref[...], mod_ref[0, 0:1, :], mod_ref[0, 1:2, :]).astype(_BF16)
    dq = N_Q_HEADS * HEAD_DIM
    dkv = N_KV_HEADS * HEAD_DIM
    k = _dot(h, w_ref[:, dq:dq + dkv])
    v = _dot(h, w_ref[:, dq + dkv:dq + 2 * dkv])
    for c in range(dkv // V7X_LANES):
        cols = slice(c * V7X_LANES, (c + 1) * V7X_LANES)
        _key_heads_to_slabs(k[:, cols], k_ref, c, slice(None))
        _heads_to_slabs(v[:, cols], v_ref, c, slice(None))


def _ctx_kv(x2d, first_row, n, mod, gain, w):
    d = x2d.shape[1]
    assert first_row % n == 0
    slab = (N_KV_HEADS, n, V7X_LANES)
    return pl.pallas_call(
        _kv_kernel,
        grid=(1,),
        in_specs=[
            pl.BlockSpec((n, d), lambda i: (first_row // n, 0)),
            pl.BlockSpec((1, N_MOD, d), lambda i: (0, 0, 0)),
            _const_spec((1, d)),
            _const_spec(w.shape),
        ],
        out_specs=[pl.BlockSpec(slab, lambda i: (0, 0, 0)), pl.BlockSpec(slab, lambda i: (0, 0, 0))],
        out_shape=[jax.ShapeDtypeStruct(slab, _BF16), jax.ShapeDtypeStruct(slab, _BF16)],
        compiler_params=_params(("arbitrary",)),
        name="ctx_kv",
    )(x2d, mod, gain, w)


def _attn_kernel(*refs, tq, nq, cast_rope):
    ncast = len(cast_rope)
    sink_ref, q_ref, kp_ref, kc_ref, kn_ref, vp_ref, vc_ref, vn_ref, kx_ref, vx_ref = refs[:10]
    o_ref = refs[10 + ncast]
    kext, vext, s_even, s_odd = refs[11 + 2 * ncast:]
    _cast_chunks(cast_rope, refs[10:10 + ncast], refs[11 + ncast:11 + 2 * ncast])
    n = pl.program_id(1)
    blk = WINDOW
    nsub = tq // blk
    nblocks = nsub * N_KV_HEADS
    ctx_len = kx_ref.shape[1]
    kext[:, 0:blk, :] = kp_ref[...]
    kext[:, blk:blk + tq, :] = kc_ref[...]
    kext[:, blk + tq:2 * blk + tq, :] = kn_ref[...]
    vext[:, 0:blk, :] = vp_ref[...]
    vext[:, blk:blk + tq, :] = vc_ref[...]
    vext[:, blk + tq:2 * blk + tq, :] = vn_ref[...]
    qi = lax.broadcasted_iota(jnp.int32, (blk, blk), 0)
    kj = lax.broadcasted_iota(jnp.int32, (blk, blk), 1)
    lane = lax.broadcasted_iota(jnp.int32, (blk, V7X_LANES), 1)
    lo = lane < HEAD_DIM
    q_first = (lane % ROPE_HALF) < ROPE_HEAD_SPAN
    ones_win = jnp.ones((3 * blk, V7X_LANES), _BF16)
    ones_ctx = jnp.ones((ctx_len, V7X_LANES), _BF16)

    def locate(i):
        j = i // N_KV_HEADS
        return j, i % N_KV_HEADS, j * blk

    def scores(i, s_ref):
        _, hkv, row0 = locate(i)
        parts = []
        for c in range(2):
            qc = q_ref[2 * hkv + c, pl.ds(row0, blk), :]
            parts.append(jnp.where(q_first, qc, jnp.zeros_like(qc)))
            parts.append(jnp.where(q_first, jnp.zeros_like(qc), qc))
        lhs = jnp.concatenate(parts, axis=0)
        s_ref[:, 0:3 * blk] = _dot_t(lhs, kext[hkv, pl.ds(row0, 3 * blk), :])
        s_ref[:, 3 * blk:3 * blk + ctx_len] = _dot_t(lhs, kx_ref[hkv])

    def softmax_pv(i, s_ref):
        j, hkv, row0 = locate(i)
        keep_prev = kj >= qi
        keep_next = kj <= qi
        if j == 0:
            keep_prev = keep_prev & (n > 0)
        if j == nsub - 1:
            keep_next = keep_next & (n < nq - 1)
        vwin = jnp.concatenate([vext[hkv, pl.ds(row0, 3 * blk), :], ones_win], axis=1)
        vx = jnp.concatenate([vx_ref[hkv], ones_ctx], axis=1)
        p_win, p_ctx, esink = [], [], []
        for g in range(GQA_GROUP):
            sink = sink_ref[hkv * GQA_GROUP + g] * LOG2E
            grow = slice(g * blk, (g + 1) * blk)
            s0 = jnp.where(keep_prev, s_ref[grow, 0:blk], NEG)
            s1 = s_ref[grow, blk:2 * blk]
            s2 = jnp.where(keep_next, s_ref[grow, 2 * blk:3 * blk], NEG)
            sc = [s_ref[grow, 3 * blk + t * V7X_LANES:3 * blk + (t + 1) * V7X_LANES]
                  for t in range(ctx_len // V7X_LANES)]
            mm = jnp.maximum(jnp.maximum(s0, s1), s2)
            for t in sc:
                mm = jnp.maximum(mm, t)
            m = jnp.maximum(jnp.max(mm, axis=-1, keepdims=True), sink)
            p_win.append(jnp.concatenate([jnp.exp2(t - m).astype(_BF16) for t in (s0, s1, s2)], axis=1))
            p_ctx.append(jnp.concatenate([jnp.exp2(t - m).astype(_BF16) for t in sc], axis=1))
            esink.append(jnp.exp2(sink - m))
        o = _dot(jnp.concatenate(p_win, axis=0), vwin) + _dot(jnp.concatenate(p_ctx, axis=0), vx)
        og = []
        for g in range(GQA_GROUP):
            grow = slice(g * blk, (g + 1) * blk)
            den = o[grow, V7X_LANES:2 * V7X_LANES] + esink[g]
            og.append(o[grow, 0:V7X_LANES] * (1.0 / den))
        for c in range(2):
            ocol = jnp.where(lo, og[2 * c], og[2 * c + 1])
            o_ref[2 * hkv + c, pl.ds(row0, blk), :] = ocol.astype(_BF16)

    s_bufs = (s_even, s_odd)
    scores(0, s_bufs[0])
    for i in range(nblocks):
        if i + 1 < nblocks:
            scores(i + 1, s_bufs[(i + 1) % 2])
        softmax_pv(i, s_bufs[i % 2])


def _attention(sink, q, kd, vd, kx, vx, batch, seq, ctx_len, tq, cast_jobs=()):
    nq_slab, n, _ = q.shape
    nq = seq // tq
    hb = tq // WINDOW
    sb = seq // WINDOW
    last = n // WINDOW - 1
    kv = N_KV_HEADS

    def cur():
        return pl.BlockSpec((kv, tq, V7X_LANES), lambda b, i: (0, b * nq + i, 0))

    def prev():
        return pl.BlockSpec((kv, WINDOW, V7X_LANES), lambda b, i: (0, jnp.maximum(b * sb + i * hb - 1, 0), 0))

    def nxt():
        return pl.BlockSpec((kv, WINDOW, V7X_LANES), lambda b, i: (0, jnp.minimum(b * sb + (i + 1) * hb, last), 0))

    def ctx():
        return pl.BlockSpec((kv, ctx_len, V7X_LANES), lambda b, i: (0, b, 0))

    cast_in, cast_out, cast_shapes = _cast_specs(cast_jobs, batch * nq, lambda b, i: b * nq + i)
    kernel = functools.partial(_attn_kernel, tq=tq, nq=nq, cast_rope=tuple(job[2] for job in cast_jobs))
    return pl.pallas_call(
        kernel,
        grid=(batch, nq),
        in_specs=[
            pl.BlockSpec(memory_space=pltpu.SMEM),
            pl.BlockSpec((nq_slab, tq, V7X_LANES), lambda b, i: (0, b * nq + i, 0)),
            prev(), cur(), nxt(), prev(), cur(), nxt(), ctx(), ctx(),
        ] + cast_in,
        out_specs=[pl.BlockSpec((nq_slab, tq, V7X_LANES), lambda b, i: (0, b * nq + i, 0))] + cast_out,
        out_shape=[jax.ShapeDtypeStruct(q.shape, _BF16)] + cast_shapes,
        scratch_shapes=[
            pltpu.VMEM((kv, tq + 2 * WINDOW, V7X_LANES), _BF16),
            pltpu.VMEM((kv, tq + 2 * WINDOW, V7X_LANES), _BF16),
            pltpu.VMEM((GQA_GROUP * WINDOW, 3 * WINDOW + ctx_len), _F32),
            pltpu.VMEM((GQA_GROUP * WINDOW, 3 * WINDOW + ctx_len), _F32),
        ],
        compiler_params=_params(("parallel", "parallel")),
        name="band_attn",
    )(sink, q, kd, kd, kd, vd, vd, vd, kx, vx, *[job[0] for job in cast_jobs])


def _gate_weights(r_w, r_b, i_w, i_b):
    heads_per_half = V7X_MXU_DIM // LRU_HEAD_DIM
    eye = jnp.eye(heads_per_half, dtype=_F32)

    def halves(w):
        w = w.reshape(2, D_LRU // V7X_MXU_DIM, heads_per_half, LRU_HEAD_DIM, LRU_HEAD_DIM)
        bd = jnp.einsum('dxhij,hk->dxhikj', w, eye)
        return bd.reshape(2, D_LRU // V7X_MXU_DIM, V7X_MXU_DIM, V7X_MXU_DIM)

    w = (0.5 * jnp.concatenate([halves(r_w), halves(i_w)], axis=-1)).astype(_BF16)
    rb = r_b.reshape(2, D_LRU // V7X_MXU_DIM, 1, V7X_MXU_DIM)
    ib = i_b.reshape(2, D_LRU // V7X_MXU_DIM, 1, V7X_MXU_DIM)
    return w, 0.5 * jnp.concatenate([rb, ib], axis=-1)


def _rope_table(seq):
    grid_rows = seq // GRID_W
    inv_freq = ROPE_BASE ** (-jnp.arange(ROPE_FREQS, dtype=_F32) / ROPE_FREQS)
    pos = jnp.concatenate([jnp.arange(GRID_W), jnp.arange(grid_rows)]).astype(_F32)
    ang = pos[:, None] * inv_freq
    cos, sin = jnp.cos(ang), jnp.sin(ang)
    is_col = (jnp.arange(GRID_W + grid_rows) < GRID_W)[:, None]
    zero = jnp.zeros_like(cos)
    table = jnp.concatenate([jnp.where(is_col, zero, cos), jnp.where(is_col, cos, zero),
                             jnp.where(is_col, zero, sin), jnp.where(is_col, sin, zero)], axis=1)
    return jnp.pad(table, ((0, 0), (0, V7X_LANES - 4 * ROPE_FREQS)))


def kernel(x, c, ctx, c_ctx, ada_w, ada_b, norm_mix_g, norm_ffn_g, norm_final_g, ffn_w_in, ffn_w_out,
           rc_w_in, rc_conv_a_w, rc_conv_a_b, rc_gate_r_w, rc_gate_r_b, rc_gate_i_w, rc_gate_i_b,
           rc_lambda, rc_conv_b_w, rc_w_out, at_w_qkv, at_sink, at_w_out):
    batch, seq, d = x.shape
    ctx_len = ctx.shape[1]
    tm = 1024
    tm_rc = 512
    tm_ffn = 1024
    tq = 1024
    tc = 1024
    assert seq % tm_ffn == 0 and seq % tm_rc == 0 and seq % tm == 0 and seq % tq == 0 and seq % tc == 0
    assert ctx_len % V7X_LANES == 0

    xl = x.reshape(batch * seq, d)
    xc = ctx.reshape(batch * ctx_len, d)

    cond = jnp.concatenate([c_ctx[None], c, jnp.zeros((8 - 1 - batch, d), _F32)], axis=0)
    mod = _modulation(cond, ada_w, ada_b).reshape(ada_w.shape[0], 8, N_MOD, d)

    def lat_row(tile):
        return lambda i: 1 + i // (seq // tile)

    ctx_row = lambda i: 0

    w_gate, b_gate = _gate_weights(rc_gate_r_w[0], rc_gate_r_b[0], rc_gate_i_w[0], rc_gate_i_b[0])
    lam = rc_lambda[0].reshape(2, 1, D_LRU)
    conv_a_b = rc_conv_a_b[0].reshape(1, D_LRU)
    gmix0 = norm_mix_g[0].reshape(1, d)
    gffn0 = norm_ffn_g[0].reshape(1, d)

    xa_c, p_c, w_in = _project(xc, mod[0], gmix0, None, batch * ctx_len, ctx_row, cast_jobs=[(rc_w_in, 0, 0)])
    xa_l, p_l, w1_0, w2_0, w_out0, w_qkv = _project(
        xl, mod[0], gmix0, w_in, tm, lat_row(tm),
        cast_jobs=[(ffn_w_in, 0, 0), (ffn_w_out, 0, 0), (rc_w_out, 0, 0),
                   (at_w_qkv, 0, (N_Q_HEADS + N_KV_HEADS) * HEAD_DIM)])

    h0 = jnp.zeros((batch, 2, V7X_SUBLANES, D_LRU), _F32)
    hf_c, hb_c, h_ctx = _scan(xa_c, h0, rc_conv_a_w[0], conv_a_b, w_gate, b_gate, lam, batch, ctx_len, ctx_len)
    hf_l, hb_l, _ = _scan(xa_l, h_ctx, rc_conv_a_w[0], conv_a_b, w_gate, b_gate, lam, batch, seq, tc)

    n_lat = batch * seq
    x_all = _rc_out(xl, hf_l, hb_l, p_l, xc, hf_c, hb_c, p_c, rc_conv_b_w[0], w_out0, mod[0], gffn0,
                    w1_0, w2_0, tm_rc, seq, ctx_len)

    gmix1 = norm_mix_g[1].reshape(1, d)
    gffn1 = norm_ffn_g[1].reshape(1, d)

    q, kd, vd, w2_1 = _qkv(x_all, n_lat, mod[1], gmix1, w_qkv, _rope_table(seq), tm, seq, lat_row(tm),
                           cast_jobs=[(ffn_w_out, 1, 0)])
    kx, vx = _ctx_kv(x_all, n_lat, batch * ctx_len, mod[1], gmix1, w_qkv)
    o, w1_1, w_out1 = _attention(at_sink[0], q, kd, vd, kx, vx, batch, seq, ctx_len, tq,
                                 cast_jobs=[(ffn_w_in, 1, 0), (at_w_out, 0, 0)])

    out = _attn_out(x_all, n_lat, o, w_out1, mod[1], gffn1, w1_1, w2_1,
                    norm_final_g.reshape(1, d), tm_ffn, lat_row(tm_ffn))
    return out.reshape(batch, seq, d)
```
